```python
import math
import jax
import jax.numpy as jnp
from jax import lax
import numpy as np

D_MODEL = 2048
BATCH = 4
SEQ = 4096
DEPTH = 2

GRID_W = 64
CTX_LEN = 256
HEAD_DIM = 128
D_MIX = D_MODEL
GROUP_W = D_MIX // 4
N_GROUP_HEADS = GROUP_W // HEAD_DIM
D_FF = 4 * D_MODEL
Q_BLOCK = 128
ROPE_THETA = 10000.0
NORM_EPS = 1e-6

MLA_HEADS = N_GROUP_HEADS
MLA_Q_RANK = GROUP_W
MLA_KV_RANK = GROUP_W // 2
MLA_NOPE = 128
MLA_ROPE = 64
MLA_V = GROUP_W // MLA_HEADS
GQA_Q_HEADS = N_GROUP_HEADS
GQA_KV_HEADS = N_GROUP_HEADS // 2
GQA_GROUP = GQA_Q_HEADS // GQA_KV_HEADS
DIFF_HEADS = N_GROUP_HEADS
DIFF_HALF = HEAD_DIM // 2
NA_HEADS = N_GROUP_HEADS
NA_WIN_H = 8
NA_WIN_W = 16

A_COLS = MLA_Q_RANK + MLA_KV_RANK + MLA_ROPE
B_COLS = (GQA_Q_HEADS + 2 * GQA_KV_HEADS) * HEAD_DIM
C_COLS = 3 * DIFF_HEADS * HEAD_DIM
D_COLS = 3 * NA_HEADS * HEAD_DIM
IN_COLS = A_COLS + B_COLS + C_COLS + D_COLS

kernel_name = 'hybrid_parallel_mixer_dit_block'


def rmsnorm(x, g):
    xf = x.astype(jnp.float32)
    y = xf * lax.rsqrt(jnp.mean(xf * xf, axis=-1, keepdims=True) + NORM_EPS)
    return (y * g.astype(jnp.float32)).astype(x.dtype)


def axial_angles(n_tok, rot_dim):
    t = jnp.arange(n_tok)
    row = (t // GRID_W).astype(jnp.float32)
    col = (t % GRID_W).astype(jnp.float32)
    half = rot_dim // 2
    inv_freq = ROPE_THETA ** (-jnp.arange(0, half, 2, dtype=jnp.float32) / half)
    return row[:, None] * inv_freq, col[:, None] * inv_freq


def _rope_1d(x, ang):
    x1, x2 = jnp.split(x, 2, axis=-1)
    cos = jnp.cos(ang).astype(x.dtype)
    sin = jnp.sin(ang).astype(x.dtype)
    return jnp.concatenate([x1 * cos - x2 * sin, x2 * cos + x1 * sin], axis=-1)


def axial_rope(x, angs):
    ang_row, ang_col = angs
    half = x.shape[-1] // 2
    return jnp.concatenate([_rope_1d(x[..., :half], ang_row), _rope_1d(x[..., half:], ang_col)], axis=-1)


def to_blocks(a):
    nb = a.shape[-2] // Q_BLOCK
    a = a.reshape(a.shape[:-2] + (nb, Q_BLOCK, a.shape[-1]))
    return jnp.moveaxis(a, -3, 0)


def from_blocks(o):
    o = jnp.moveaxis(o, 0, -3)
    return o.reshape(o.shape[:-3] + (o.shape[-3] * o.shape[-2], o.shape[-1]))


def merge_heads(o):
    b, h, n, d = o.shape
    return o.transpose(0, 2, 1, 3).reshape(b, n, h * d)


def _softmax(s):
    return jax.nn.softmax(s.astype(jnp.float32), axis=-1)


def sq_relu_mlp(h, w_up, w_down):
    return jnp.square(jax.nn.relu(h @ w_up)) @ w_down


def mla_mixer(p_lat, p_ctx, g_qa, g_kva, w_uq, w_ukv, g_q, g_k, angs, need_ctx):
    def project(p):
        cq = rmsnorm(p[..., :MLA_Q_RANK], g_qa)
        ckv = rmsnorm(p[..., MLA_Q_RANK:MLA_Q_RANK + MLA_KV_RANK], g_kva)
        k_pe = rmsnorm(p[..., MLA_Q_RANK + MLA_KV_RANK:], g_k[MLA_NOPE:])
        q = jnp.einsum('bnr,rhd->bhnd', cq, w_uq.reshape(MLA_Q_RANK, MLA_HEADS, MLA_NOPE + MLA_ROPE))
        kv = jnp.einsum('bnr,rhd->bhnd', ckv, w_ukv.reshape(MLA_KV_RANK, MLA_HEADS, MLA_NOPE + MLA_V))
        q_nope = rmsnorm(q[..., :MLA_NOPE], g_q[:MLA_NOPE])
        q_pe = rmsnorm(q[..., MLA_NOPE:], g_q[MLA_NOPE:])
        k_nope = rmsnorm(kv[..., :MLA_NOPE], g_k[:MLA_NOPE])
        return q_nope, q_pe, k_nope, k_pe, kv[..., MLA_NOPE:]

    scale = (MLA_NOPE + MLA_ROPE) ** -0.5

    def attend(qn, qr, kn, kr, v):
        s = (jnp.einsum('bhqd,bhkd->bhqk', qn, kn) + jnp.einsum('bhqr,bkr->bhqk', qr, kr)).astype(jnp.float32) * scale
        return jnp.einsum('bhqk,bhkd->bhqd', _softmax(s).astype(v.dtype), v)

    qn, qr, kn, kr, v = project(p_lat)
    qr = axial_rope(qr, angs)
    kr = axial_rope(kr, angs)
    cqn, cqr, ckn, ckr, cv = project(p_ctx)
    kn_all = jnp.concatenate([kn, ckn], axis=2)
    kr_all = jnp.concatenate([kr, ckr], axis=1)
    v_all = jnp.concatenate([v, cv], axis=2)
    o = lax.map(lambda qb: attend(qb[0], qb[1], kn_all, kr_all, v_all), (to_blocks(qn), to_blocks(qr)))
    o_lat = merge_heads(from_blocks(o))
    o_ctx = merge_heads(attend(cqn, cqr, ckn, ckr, cv)) if need_ctx else None
    return o_lat, o_ctx


def gqa_mixer(p_lat, p_ctx, g_q, g_k, angs, need_ctx):
    qd = GQA_Q_HEADS * HEAD_DIM
    kd = GQA_KV_HEADS * HEAD_DIM

    def project(p):
        b, n, _ = p.shape
        q = p[..., :qd].reshape(b, n, GQA_KV_HEADS, GQA_GROUP, HEAD_DIM).transpose(0, 2, 3, 1, 4)
        k = p[..., qd:qd + kd].reshape(b, n, GQA_KV_HEADS, HEAD_DIM).transpose(0, 2, 1, 3)
        v = p[..., qd + kd:].reshape(b, n, GQA_KV_HEADS, HEAD_DIM).transpose(0, 2, 1, 3)
        return rmsnorm(q, g_q), rmsnorm(k, g_k), v

    def attend(q, k, v):
        s = jnp.einsum('bkgqd,bknd->bkgqn', q, k).astype(jnp.float32) * HEAD_DIM ** -0.5
        return jnp.einsum('bkgqn,bknd->bkgqd', _softmax(s).astype(v.dtype), v)

    def heads_out(o):
        b, kh, g, n, d = o.shape
        return merge_heads(o.reshape(b, kh * g, n, d))

    q, k, v = project(p_lat)
    q = axial_rope(q, angs)
    k = axial_rope(k, angs)
    cq, ck, cv = project(p_ctx)
    k_all = jnp.concatenate([k, ck], axis=2)
    v_all = jnp.concatenate([v, cv], axis=2)
    o = lax.map(lambda qb: attend(qb, k_all, v_all), to_blocks(q))
    o_lat = heads_out(from_blocks(o))
    o_ctx = heads_out(attend(cq, ck, cv)) if need_ctx else None
    return o_lat, o_ctx


def diff_mixer(p_lat, p_ctx, g_q, g_k, lq1, lk1, lq2, lk2, g_out, angs, lam_init, need_ctx):
    hd = DIFF_HEADS * HEAD_DIM

    def project(p):
        b, n, _ = p.shape
        q = p[..., :hd].reshape(b, n, DIFF_HEADS, 2, DIFF_HALF).transpose(0, 2, 3, 1, 4)
        k = p[..., hd:2 * hd].reshape(b, n, DIFF_HEADS, 2, DIFF_HALF).transpose(0, 2, 3, 1, 4)
        v = p[..., 2 * hd:].reshape(b, n, DIFF_HEADS, HEAD_DIM).transpose(0, 2, 1, 3)
        return rmsnorm(q, g_q), rmsnorm(k, g_k), v

    f32 = jnp.float32
    lam = (jnp.exp(jnp.sum(lq1.astype(f32) * lk1.astype(f32)))
           - jnp.exp(jnp.sum(lq2.astype(f32) * lk2.astype(f32))) + lam_init)

    def attend(q, k, v):
        s = jnp.einsum('bhiqd,bhind->bhiqn', q, k).astype(f32) * DIFF_HALF ** -0.5
        a = _softmax(s)
        w = a[:, :, 0] - lam * a[:, :, 1]
        o = jnp.einsum('bhqn,bhnd->bhqd', w.astype(v.dtype), v)
        return rmsnorm(o, g_out) * (1.0 - lam_init)

    q, k, v = project(p_lat)
    q = axial_rope(q, angs)
    k = axial_rope(k, angs)
    cq, ck, cv = project(p_ctx)
    k_all = jnp.concatenate([k, ck], axis=3)
    v_all = jnp.concatenate([v, cv], axis=2)
    o = lax.map(lambda qb: attend(qb, k_all, v_all), to_blocks(q))
    o_lat = merge_heads(from_blocks(o))
    o_ctx = merge_heads(attend(cq, ck, cv)) if need_ctx else None
    return o_lat, o_ctx


def na_mixer(p_lat, p_ctx, g_q, g_k, rpb, need_ctx):
    hd = NA_HEADS * HEAD_DIM
    f32 = jnp.float32
    scale = HEAD_DIM ** -0.5

    def project(p):
        b, n, _ = p.shape
        q = p[..., :hd].reshape(b, n, NA_HEADS, HEAD_DIM).transpose(0, 2, 1, 3)
        k = p[..., hd:2 * hd].reshape(b, n, NA_HEADS, HEAD_DIM).transpose(0, 2, 1, 3)
        v = p[..., 2 * hd:].reshape(b, n, NA_HEADS, HEAD_DIM).transpose(0, 2, 1, 3)
        return rmsnorm(q, g_q), rmsnorm(k, g_k), v

    q, k, v = project(p_lat)
    cq, ck, cv = project(p_ctx)
    n_tok = q.shape[2]
    rows = n_tok // GRID_W
    wh = min(NA_WIN_H, rows)
    bh = min(wh + 1, rows)
    band = bh * GRID_W
    rows_per_block = Q_BLOCK // GRID_W
    q_loc = jnp.arange(Q_BLOCK)
    q_roff = q_loc // GRID_W
    q_col = q_loc % GRID_W
    k_roff = jnp.repeat(jnp.arange(bh), GRID_W)
    k_col = jnp.tile(jnp.arange(GRID_W), bh)
    col_start = jnp.clip(q_col - NA_WIN_W // 2, 0, GRID_W - NA_WIN_W)
    col_mask = (k_col[None, :] >= col_start[:, None]) & (k_col[None, :] < col_start[:, None] + NA_WIN_W)
    col_idx = jnp.clip(k_col[None, :] - q_col[:, None] + NA_WIN_W - 1, 0, 2 * NA_WIN_W - 2)

    def block(args):
        i, qb = args
        r0 = i * rows_per_block
        q_row = r0 + q_roff
        row_start = jnp.clip(q_row - wh // 2, 0, rows - wh)
        band_start = jnp.minimum(jnp.clip(r0 - wh // 2, 0, rows - wh), rows - bh)
        kb = lax.dynamic_slice_in_dim(k, band_start * GRID_W, band, axis=2)
        vb = lax.dynamic_slice_in_dim(v, band_start * GRID_W, band, axis=2)
        k_row = band_start + k_roff
        mask = col_mask & (k_row[None, :] >= row_start[:, None]) & (k_row[None, :] < row_start[:, None] + wh)
        row_idx = jnp.clip(k_row[None, :] - q_row[:, None] + NA_WIN_H - 1, 0, 2 * NA_WIN_H - 2)
        bias = rpb[:, row_idx, col_idx].astype(f32)
        s_band = jnp.einsum('bhqd,bhkd->bhqk', qb, kb).astype(f32) * scale + bias
        s_band = jnp.where(mask, s_band, -jnp.inf)
        s_ctx = jnp.einsum('bhqd,bhkd->bhqk', qb, ck).astype(f32) * scale
        pr = _softmax(jnp.concatenate([s_band, s_ctx], axis=-1)).astype(v.dtype)
        return (jnp.einsum('bhqk,bhkd->bhqd', pr[..., :band], vb)
                + jnp.einsum('bhqk,bhkd->bhqd', pr[..., band:], cv))

    o = lax.map(block, (jnp.arange(n_tok // Q_BLOCK), to_blocks(q)))
    o_lat = merge_heads(from_blocks(o))
    if need_ctx:
        s = jnp.einsum('bhqd,bhkd->bhqk', cq, ck).astype(f32) * scale
        o_ctx = merge_heads(jnp.einsum('bhqk,bhkd->bhqd', _softmax(s).astype(cv.dtype), cv))
    else:
        o_ctx = None
    return o_lat, o_ctx


def setup_inputs(seed: int = 0) -> dict:
    key = jax.random.key(seed)
    ks = jax.random.split(key, 32)
    f32 = jnp.float32

    def nrm(k, shape, s):
        return jax.random.normal(k, shape, f32) * s

    def gain(k, shape):
        return 1.0 + 0.05 * jax.random.normal(k, shape, f32)

    return {
        'x': nrm(ks[0], (BATCH, SEQ, D_MODEL), 1.0),
        'c': nrm(ks[1], (BATCH, D_MODEL), 1.0),
        'ctx': nrm(ks[2], (BATCH, CTX_LEN, D_MODEL), 1.0),
        'c_ctx': nrm(ks[3], (D_MODEL,), 1.0),
        'w_mod': nrm(ks[4], (DEPTH, D_MODEL, 6 * D_MODEL), D_MODEL ** -0.5),
        'b_mod': nrm(ks[5], (DEPTH, 6 * D_MODEL), 0.02),
        'g_norm_mix': gain(ks[6], (DEPTH, D_MODEL)),
        'g_norm_mlp': gain(ks[7], (DEPTH, D_MODEL)),
        'w_in': nrm(ks[8], (DEPTH, D_MODEL, IN_COLS), D_MODEL ** -0.5),
        'mla_g_qa': gain(ks[9], (DEPTH, MLA_Q_RANK)),
        'mla_g_kva': gain(ks[10], (DEPTH, MLA_KV_RANK)),
        'mla_w_uq': nrm(ks[11], (DEPTH, MLA_Q_RANK, MLA_HEADS * (MLA_NOPE + MLA_ROPE)), MLA_Q_RANK ** -0.5),
        'mla_w_ukv': nrm(ks[12], (DEPTH, MLA_KV_RANK, MLA_HEADS * (MLA_NOPE + MLA_V)), MLA_KV_RANK ** -0.5),
        'mla_g_q': gain(ks[13], (DEPTH, MLA_NOPE + MLA_ROPE)),
        'mla_g_k': gain(ks[14], (DEPTH, MLA_NOPE + MLA_ROPE)),
        'gqa_g_q': gain(ks[15], (DEPTH, HEAD_DIM)),
        'gqa_g_k': gain(ks[16], (DEPTH, HEAD_DIM)),
        'diff_g_q': gain(ks[17], (DEPTH, DIFF_HALF)),
        'diff_g_k': gain(ks[18], (DEPTH, DIFF_HALF)),
        'diff_lq1': nrm(ks[19], (DEPTH, DIFF_HALF), 0.1),
        'diff_lk1': nrm(ks[20], (DEPTH, DIFF_HALF), 0.1),
        'diff_lq2': nrm(ks[21], (DEPTH, DIFF_HALF), 0.1),
        'diff_lk2': nrm(ks[22], (DEPTH, DIFF_HALF), 0.1),
        'diff_g_out': gain(ks[23], (DEPTH, HEAD_DIM)),
        'na_g_q': gain(ks[24], (DEPTH, HEAD_DIM)),
        'na_g_k': gain(ks[25], (DEPTH, HEAD_DIM)),
        'na_rpb': nrm(ks[26], (DEPTH, NA_HEADS, 2 * NA_WIN_H - 1, 2 * NA_WIN_W - 1), 0.1),
        'w_out': nrm(ks[27], (DEPTH, D_MIX, D_MODEL), D_MIX ** -0.5),
        'w_up': nrm(ks[28], (DEPTH, D_MODEL, D_FF), D_MODEL ** -0.5),
        'w_down': nrm(ks[29], (DEPTH, D_FF, D_MODEL), D_FF ** -0.5),
    }


def reference(x, c, ctx, c_ctx, w_mod, b_mod, g_norm_mix, g_norm_mlp, w_in, mla_g_qa, mla_g_kva,
              mla_w_uq, mla_w_ukv, mla_g_q, mla_g_k, gqa_g_q, gqa_g_k, diff_g_q, diff_g_k,
              diff_lq1, diff_lk1, diff_lq2, diff_lk2, diff_g_out, na_g_q, na_g_k, na_rpb,
              w_out, w_up, w_down):
    n_tok = x.shape[1]
    angs_mla = axial_angles(n_tok, MLA_ROPE)
    angs_gqa = axial_angles(n_tok, HEAD_DIM)
    angs_diff = axial_angles(n_tok, DIFF_HALF)
    ca, cb, cc = A_COLS, A_COLS + B_COLS, A_COLS + B_COLS + C_COLS
    for l in range(DEPTH):
        need_ctx = l < DEPTH - 1
        mod = jax.nn.silu(c) @ w_mod[l] + b_mod[l]
        mod_c = jax.nn.silu(c_ctx) @ w_mod[l] + b_mod[l]
        sh_a, sc_a, gt_a, sh_m, sc_m, gt_m = jnp.split(mod[:, None, :], 6, axis=-1)
        csh_a, csc_a, cgt_a, csh_m, csc_m, cgt_m = jnp.split(mod_c, 6, axis=-1)
        p = (rmsnorm(x, g_norm_mix[l]) * (1 + sc_a) + sh_a) @ w_in[l]
        pc = (rmsnorm(ctx, g_norm_mix[l]) * (1 + csc_a) + csh_a) @ w_in[l]
        o_a, oc_a = mla_mixer(p[..., :ca], pc[..., :ca], mla_g_qa[l], mla_g_kva[l], mla_w_uq[l],
                              mla_w_ukv[l], mla_g_q[l], mla_g_k[l], angs_mla, need_ctx)
        o_b, oc_b = gqa_mixer(p[..., ca:cb], pc[..., ca:cb], gqa_g_q[l], gqa_g_k[l], angs_gqa, need_ctx)
        o_c, oc_c = diff_mixer(p[..., cb:cc], pc[..., cb:cc], diff_g_q[l], diff_g_k[l], diff_lq1[l],
                               diff_lk1[l], diff_lq2[l], diff_lk2[l], diff_g_out[l], angs_diff,
                               0.8 - 0.6 * math.exp(-0.3 * l), need_ctx)
        o_d, oc_d = na_mixer(p[..., cc:], pc[..., cc:], na_g_q[l], na_g_k[l], na_rpb[l], need_ctx)
        x = x + gt_a * (jnp.concatenate([o_a, o_b, o_c, o_d], axis=-1) @ w_out[l])
        x = x + gt_m * sq_relu_mlp(rmsnorm(x, g_norm_mlp[l]) * (1 + sc_m) + sh_m, w_up[l], w_down[l])
        if need_ctx:
            ctx = ctx + cgt_a * (jnp.concatenate([oc_a, oc_b, oc_c, oc_d], axis=-1) @ w_out[l])
            ctx = ctx + cgt_m * sq_relu_mlp(rmsnorm(ctx, g_norm_mlp[l]) * (1 + csc_m) + csh_m,
                                            w_up[l], w_down[l])
    return x
```

```python
import functools
import math

import numpy as np
import jax
import jax.numpy as jnp
from jax import lax
from jax.experimental import pallas as pl
from jax.experimental.pallas import tpu as pltpu

D_MODEL = 2048
BATCH = 4
SEQ = 4096
DEPTH = 2
GRID_W = 64
CTX_LEN = 256
HEAD_DIM = 128
GROUP_W = D_MODEL // 4
N_HEADS = GROUP_W // HEAD_DIM
D_FF = 4 * D_MODEL
ROPE_THETA = 10000.0
NORM_EPS = 1e-6

MLA_Q_RANK = GROUP_W
MLA_KV_RANK = GROUP_W // 2
MLA_NOPE = 128
MLA_ROPE = 64
GQA_KV_HEADS = N_HEADS // 2
DIFF_HALF = HEAD_DIM // 2
NA_WIN_H = 8
NA_WIN_W = 16
NA_BAND_ROWS = NA_WIN_H + 1
NA_BAND = NA_BAND_ROWS * GRID_W
NA_QBLOCK = 128
NA_PATTERNS = 5

A_COLS = MLA_Q_RANK + MLA_KV_RANK + MLA_ROPE
B_COLS = (N_HEADS + 2 * GQA_KV_HEADS) * HEAD_DIM
C_COLS = 3 * N_HEADS * HEAD_DIM
D_COLS = 3 * N_HEADS * HEAD_DIM

OFF_AQ = 0
OFF_AKV = OFF_AQ + MLA_Q_RANK
OFF_B = OFF_AKV + MLA_KV_RANK
OFF_C = OFF_B + B_COLS
OFF_D = OFF_C + C_COLS
OFF_KPE = OFF_D + D_COLS
IN_COLS_PAD = OFF_KPE + 128

LANES = 128
VMEM_LIMIT = 56 * 1024 * 1024
NEG_BIG = -1e30

F32 = jnp.float32
BF16 = jnp.bfloat16

G_A_QN, G_A_QR, G_A_KN, G_A_KR, G_B_Q, G_B_K, G_C_Q, G_C_K, G_D_Q, G_D_K, G_C_OUT = range(11)
GAIN_ROWS = 16


def _lane_iota(shape):
    return lax.broadcasted_iota(jnp.int32, shape, len(shape) - 1)


def _rms(t):
    return t * lax.rsqrt(jnp.mean(t * t, axis=-1, keepdims=True) + NORM_EPS)


def _rms64(t):
    lo = _lane_iota(t.shape) < 64
    sq = t * t
    s_lo = jnp.sum(jnp.where(lo, sq, 0.0), axis=-1, keepdims=True)
    s_hi = jnp.sum(jnp.where(lo, 0.0, sq), axis=-1, keepdims=True)
    ms = jnp.where(lo, s_lo, s_hi) * (1.0 / 64.0)
    return t * lax.rsqrt(ms + NORM_EPS)


def _rope(t, cos, sin_signed, half):
    first = (_lane_iota(t.shape) % (2 * half)) < half
    partner = jnp.where(first, pltpu.roll(t, LANES - half, 1), pltpu.roll(t, half, 1))
    return t * cos + partner * sin_signed


def _dot(a, b):
    return jnp.dot(a, b, preferred_element_type=F32)


def _dot_nt(a, b):
    return lax.dot_general(a, b, (((1,), (1,)), ((), ())), preferred_element_type=F32)


def _mod_kernel(c_ref, w_ref, b_ref, o_ref):
    c = c_ref[...]
    a = (c * jax.nn.sigmoid(c)).astype(BF16)
    o_ref[0] = _dot(a, w_ref[0].astype(BF16)) + b_ref[0]


def _modulation(cc, w_mod, b_mod):
    tn = 1536
    n = 6 * D_MODEL
    return pl.pallas_call(
        _mod_kernel,
        grid=(DEPTH, n // tn),
        in_specs=[
            pl.BlockSpec((8, D_MODEL), lambda l, j: (0, 0)),
            pl.BlockSpec((1, D_MODEL, tn), lambda l, j: (l, 0, j)),
            pl.BlockSpec((1, 1, tn), lambda l, j: (l, 0, j)),
        ],
        out_specs=pl.BlockSpec((1, 8, tn), lambda l, j: (l, 0, j)),
        out_shape=jax.ShapeDtypeStruct((DEPTH, 8, n), F32),
        compiler_params=pltpu.CompilerParams(
            dimension_semantics=("arbitrary", "arbitrary"), vmem_limit_bytes=VMEM_LIMIT),
        name="modulation",
    )(cc, w_mod, b_mod.reshape(DEPTH, 1, n))


def _front_kernel(x_ref, gmix_ref, sc_ref, sh_ref, w_in_ref, w_uq_ref, w_ukv_ref,
                  gqa_ref, gkva_ref, gains_ref, cos128_ref, sin128_ref, cos64_ref, sin64_ref,
                  qa_ref, ka_ref, va_ref, qb_ref, kb_ref, vb_ref,
                  qc_ref, kc_ref, vc_ref, qd_ref, kd_ref, vd_ref):
    x = x_ref[...]
    gm = gmix_ref[...] * (1.0 + sc_ref[0])
    h = (_rms(x) * gm + sh_ref[0]).astype(BF16)

    gains = gains_ref[...]

    def gain(row):
        return gains[row:row + 1, :]

    cos128 = cos128_ref[...]
    sin128 = sin128_ref[...]
    cos64 = cos64_ref[...]
    sin64 = sin64_ref[...]
    lo = _lane_iota((x.shape[0], LANES)) < 64

    def proj(off, width):
        return _dot(h, w_in_ref[:, off:off + width])

    def tile(t, j):
        return t[:, j * LANES:(j + 1) * LANES]

    cq = (_rms(proj(OFF_AQ, MLA_Q_RANK)) * gqa_ref[...]).astype(BF16)
    ckv = (_rms(proj(OFF_AKV, MLA_KV_RANK)) * gkva_ref[...]).astype(BF16)
    kr = _rope(_rms64(proj(OFF_KPE, LANES)) * gain(G_A_KR), cos64, sin64, 16).astype(BF16)
    q = _dot(cq, w_uq_ref[...])
    kv = _dot(ckv, w_ukv_ref[...])
    qr_pairs = [
        _rope(_rms64(tile(q, N_HEADS + j)) * gain(G_A_QR), cos64, sin64, 16)
        for j in range(N_HEADS // 2)
    ]
    for hd in range(N_HEADS):
        qa_ref[hd, :, 0:LANES] = (_rms(tile(q, hd)) * gain(G_A_QN)).astype(BF16)
        pair = qr_pairs[hd // 2]
        keep = lo if hd % 2 == 0 else jnp.logical_not(lo)
        qa_ref[hd, :, LANES:2 * LANES] = jnp.where(keep, pair, 0.0).astype(BF16)
        ka_ref[hd, :, 0:LANES] = (_rms(tile(kv, hd)) * gain(G_A_KN)).astype(BF16)
        ka_ref[hd, :, LANES:2 * LANES] = kr
        va_ref[hd] = tile(kv, N_HEADS + hd).astype(BF16)

    pb = proj(OFF_B, B_COLS)
    for hd in range(N_HEADS):
        qb_ref[hd] = _rope(_rms(tile(pb, hd)) * gain(G_B_Q), cos128, sin128, 32).astype(BF16)
    for hd in range(GQA_KV_HEADS):
        kb_ref[hd] = _rope(_rms(tile(pb, N_HEADS + hd)) * gain(G_B_K),
                           cos128, sin128, 32).astype(BF16)
        vb_ref[hd] = tile(pb, N_HEADS + GQA_KV_HEADS + hd).astype(BF16)

    pc = proj(OFF_C, C_COLS)
    for hd in range(N_HEADS):
        qh = _rope(_rms64(tile(pc, hd)) * gain(G_C_Q), cos64, sin64, 16)
        qc_ref[2 * hd] = jnp.where(lo, qh, 0.0).astype(BF16)
        qc_ref[2 * hd + 1] = jnp.where(lo, 0.0, qh).astype(BF16)
        kc_ref[hd] = _rope(_rms64(tile(pc, N_HEADS + hd)) * gain(G_C_K),
                           cos64, sin64, 16).astype(BF16)
        vc_ref[hd] = tile(pc, 2 * N_HEADS + hd).astype(BF16)

    pd = proj(OFF_D, D_COLS)
    for hd in range(N_HEADS):
        qd_ref[hd] = (_rms(tile(pd, hd)) * gain(G_D_Q)).astype(BF16)
        kd_ref[hd] = (_rms(tile(pd, N_HEADS + hd)) * gain(G_D_K)).astype(BF16)
        vd_ref[hd] = tile(pd, 2 * N_HEADS + hd).astype(BF16)


_FRONT_OUT = (
    (N_HEADS, 2 * LANES), (N_HEADS, 2 * LANES), (N_HEADS, LANES),
    (N_HEADS, LANES), (GQA_KV_HEADS, LANES), (GQA_KV_HEADS, LANES),
    (2 * N_HEADS, LANES), (N_HEADS, LANES), (N_HEADS, LANES),
    (N_HEADS, LANES), (N_HEADS, LANES), (N_HEADS, LANES),
)


def _front(x2d, sc, sh, lw, tabs, *, rows_per_mod, tm=256):
    t = x2d.shape[0]
    n_pos_tiles = tabs[0].shape[0] // tm
    tiles_per_mod = rows_per_mod // tm

    def const(shape):
        return pl.BlockSpec(shape, lambda i: (0,) * len(shape), pipeline_mode=pl.Buffered(1))

    mod_spec = pl.BlockSpec((1, 1, D_MODEL), lambda i: (i // tiles_per_mod, 0, 0))
    tab_spec = pl.BlockSpec((tm, LANES), lambda i: (i % n_pos_tiles, 0))
    in_specs = [
        pl.BlockSpec((tm, D_MODEL), lambda i: (i, 0)),
        const((1, D_MODEL)), mod_spec, mod_spec,
        const((D_MODEL, IN_COLS_PAD)),
        const((MLA_Q_RANK, N_HEADS * (MLA_NOPE + MLA_ROPE))),
        const((MLA_KV_RANK, 2 * N_HEADS * LANES)),
        const((1, MLA_Q_RANK)), const((1, MLA_KV_RANK)), const((GAIN_ROWS, LANES)),
        tab_spec, tab_spec, tab_spec, tab_spec,
    ]
    out_specs = [pl.BlockSpec((nh, tm, w), lambda i: (0, i, 0)) for nh, w in _FRONT_OUT]
    out_shape = [jax.ShapeDtypeStruct((nh, t, w), BF16) for nh, w in _FRONT_OUT]
    return pl.pallas_call(
        _front_kernel,
        grid=(t // tm,),
        in_specs=in_specs,
        out_specs=out_specs,
        out_shape=out_shape,
        compiler_params=pltpu.CompilerParams(
            dimension_semantics=("arbitrary",), vmem_limit_bytes=VMEM_LIMIT),
        name="front",
    )(x2d, lw["g_mix"], sc, sh, lw["w_in"], lw["w_uq"], lw["w_ukv"],
      lw["g_qa"], lw["g_kva"], lw["gains"], *tabs)


def _softmax_step(q, k, v, m, l, acc):
    s = _dot_nt(q, k)
    m_new = jnp.maximum(m, jnp.max(s, axis=-1, keepdims=True))
    alpha = jnp.exp(m - m_new)
    p = jnp.exp(s - m_new)
    l_new = alpha * l + jnp.sum(p, axis=-1, keepdims=True)
    acc_new = alpha * acc + _dot(p.astype(BF16), v)
    return m_new, l_new, acc_new


def _flash(q, k_ref, v_ref, kc_ref, vc_ref, tk):
    rows = q.shape[0]
    dv = v_ref.shape[-1]
    m = jnp.full((rows, 1), NEG_BIG, F32)
    l = jnp.zeros((rows, 1), F32)
    acc = jnp.zeros((rows, dv), F32)
    if k_ref is not None:
        for c in range(k_ref.shape[0] // tk):
            m, l, acc = _softmax_step(q, k_ref[c * tk:(c + 1) * tk, :],
                                      v_ref[c * tk:(c + 1) * tk, :], m, l, acc)
    m, l, acc = _softmax_step(q, kc_ref[...], vc_ref[...], m, l, acc)
    return acc / l


def _diff_lambda(lam_ref, lam_init):
    lam1 = jnp.exp(jnp.sum(lam_ref[0:1, :] * lam_ref[1:2, :], axis=-1, keepdims=True))
    lam2 = jnp.exp(jnp.sum(lam_ref[2:3, :] * lam_ref[3:4, :], axis=-1, keepdims=True))
    return lam1 - lam2 + lam_init


def _diff_combine(o, tq, lam, g_out, lam_init):
    d = o[:tq] - lam * o[tq:]
    return _rms(d) * (g_out * (1.0 - lam_init))


def _attn_kernel(q_ref, k_ref, v_ref, kc_ref, vc_ref, *rest, stacked, diff, lam_init, tk):
    if diff:
        lam_ref, gout_ref, o_ref = rest
    else:
        (o_ref,) = rest
    tq = q_ref.shape[1]
    if stacked:
        q = jnp.concatenate([q_ref[0], q_ref[1]], axis=0)
    else:
        q = q_ref[0]
    o = _flash(q, k_ref.at[0], v_ref.at[0], kc_ref.at[0], vc_ref.at[0], tk)
    if diff:
        lam = _diff_lambda(lam_ref, lam_init)
        o_ref[0] = _diff_combine(o, tq, lam, gout_ref[...], lam_init).astype(o_ref.dtype)
    elif stacked:
        o_ref[0] = o[:tq].astype(o_ref.dtype)
        o_ref[1] = o[tq:].astype(o_ref.dtype)
    else:
        o_ref[0] = o.astype(o_ref.dtype)


def _attention(q, k, v, kc, vc, *, kind, extra=(), lam_init=0.0, tq=512, tk=512):
    nq = SEQ // tq
    dk = q.shape[-1]
    stacked = kind in ("gqa", "diff")
    n_outer = GQA_KV_HEADS if kind == "gqa" else N_HEADS
    q_heads = 2 if stacked else 1
    o_heads = 2 if kind == "gqa" else 1
    in_specs = [
        pl.BlockSpec((q_heads, tq, dk), lambda b, h, i: (h, b * nq + i, 0)),
        pl.BlockSpec((1, SEQ, dk), lambda b, h, i: (h, b, 0)),
        pl.BlockSpec((1, SEQ, LANES), lambda b, h, i: (h, b, 0)),
        pl.BlockSpec((1, CTX_LEN, dk), lambda b, h, i: (h, b, 0)),
        pl.BlockSpec((1, CTX_LEN, LANES), lambda b, h, i: (h, b, 0)),
    ]
    for e in extra:
        in_specs.append(pl.BlockSpec(e.shape, lambda b, h, i, nd=e.ndim: (0,) * nd))
    return pl.pallas_call(
        functools.partial(_attn_kernel, stacked=stacked, diff=(kind == "diff"),
                          lam_init=lam_init, tk=tk),
        grid=(BATCH, n_outer, nq),
        in_specs=in_specs,
        out_specs=pl.BlockSpec((o_heads, tq, LANES), lambda b, h, i: (h, b * nq + i, 0)),
        out_shape=jax.ShapeDtypeStruct((N_HEADS, BATCH * SEQ, LANES), BF16),
        compiler_params=pltpu.CompilerParams(
            dimension_semantics=("arbitrary", "arbitrary", "arbitrary"),
            vmem_limit_bytes=VMEM_LIMIT),
        name="attn_" + kind,
    )(q, k, v, kc, vc, *extra)


def _na_pattern_of(i):
    n_blocks = SEQ // NA_QBLOCK
    return jnp.where(i == 0, 0, jnp.where(i == 1, 1, jnp.where(
        i == n_blocks - 2, 3, jnp.where(i == n_blocks - 1, 4, 2))))


def _na_geometry(i):
    rows = SEQ // GRID_W
    rows_per_block = NA_QBLOCK // GRID_W
    r0 = i * rows_per_block
    band_start = min(min(max(r0 - NA_WIN_H // 2, 0), rows - NA_WIN_H), rows - NA_BAND_ROWS)
    rel = np.zeros((rows_per_block, NA_BAND_ROWS), np.int64)
    valid = np.zeros((rows_per_block, NA_BAND_ROWS), bool)
    for a in range(rows_per_block):
        q_row = r0 + a
        row_start = min(max(q_row - NA_WIN_H // 2, 0), rows - NA_WIN_H)
        for j in range(NA_BAND_ROWS):
            k_row = band_start + j
            valid[a, j] = row_start <= k_row < row_start + NA_WIN_H
            rel[a, j] = min(max(k_row - q_row + NA_WIN_H - 1, 0), 2 * NA_WIN_H - 2)
    return band_start, rel, valid


_NA_REPRESENTATIVE = (0, 1, 2, SEQ // NA_QBLOCK - 2, SEQ // NA_QBLOCK - 1)


def _na_bias_kernel(rpb_ref, o_ref):
    hd = pl.program_id(0)
    n_r = 2 * NA_WIN_H - 1
    n_c = 2 * NA_WIN_W - 1
    q_col = lax.broadcasted_iota(jnp.int32, (GRID_W, GRID_W), 0)
    k_col = lax.broadcasted_iota(jnp.int32, (GRID_W, GRID_W), 1)
    col_idx = jnp.clip(k_col - q_col + NA_WIN_W - 1, 0, n_c - 1)
    col_start = jnp.clip(q_col - NA_WIN_W // 2, 0, GRID_W - NA_WIN_W)
    col_ok = (k_col >= col_start) & (k_col < col_start + NA_WIN_W)
    tables = []
    for r in range(n_r):
        t = jnp.zeros((GRID_W, GRID_W), F32)
        for c in range(n_c):
            t = jnp.where(col_idx == c, rpb_ref[hd * (n_r * n_c) + r * n_c + c], t)
        tables.append(jnp.where(col_ok, t, NEG_BIG))
    masked = jnp.full((GRID_W, GRID_W), NEG_BIG, F32)
    for p, i in enumerate(_NA_REPRESENTATIVE):
        _, rel, valid = _na_geometry(i)
        for a in range(rel.shape[0]):
            for j in range(rel.shape[1]):
                o_ref[0, p, a, j] = tables[int(rel[a, j])] if valid[a, j] else masked


def _na_bias(rpb):
    rows_per_block = NA_QBLOCK // GRID_W
    out = pl.pallas_call(
        _na_bias_kernel,
        grid=(N_HEADS,),
        in_specs=[pl.BlockSpec(memory_space=pltpu.SMEM)],
        out_specs=pl.BlockSpec((1, NA_PATTERNS, rows_per_block, NA_BAND_ROWS, GRID_W, GRID_W),
                               lambda h: (h, 0, 0, 0, 0, 0)),
        out_shape=jax.ShapeDtypeStruct(
            (N_HEADS, NA_PATTERNS, rows_per_block, NA_BAND_ROWS, GRID_W, GRID_W), F32),
        name="na_bias",
    )(rpb.reshape(-1))
    return out.transpose(0, 1, 2, 4, 3, 5).reshape(N_HEADS, NA_PATTERNS, NA_QBLOCK, NA_BAND)


def _na_kernel(q_ref, k_ref, v_ref, kc_ref, vc_ref, bias_ref, o_ref):
    kc = kc_ref[0]
    vc = vc_ref[0]
    rows = SEQ // GRID_W
    rows_per_block = NA_QBLOCK // GRID_W

    def body(i, carry):
        r0 = i * rows_per_block
        band_start = jnp.minimum(jnp.clip(r0 - NA_WIN_H // 2, 0, rows - NA_WIN_H),
                                 rows - NA_BAND_ROWS)
        k0 = pl.multiple_of(band_start * GRID_W, GRID_W)
        q0 = pl.multiple_of(i * NA_QBLOCK, NA_QBLOCK)
        q = q_ref[0, pl.ds(q0, NA_QBLOCK), :]
        kb = k_ref[0, pl.ds(k0, NA_BAND), :]
        vb = v_ref[0, pl.ds(k0, NA_BAND), :]
        s_band = _dot_nt(q, kb) + bias_ref[0, _na_pattern_of(i)]
        s_ctx = _dot_nt(q, kc)
        m = jnp.maximum(jnp.max(s_band, axis=-1, keepdims=True),
                        jnp.max(s_ctx, axis=-1, keepdims=True))
        p_band = jnp.exp(s_band - m)
        p_ctx = jnp.exp(s_ctx - m)
        l = jnp.sum(p_band, axis=-1, keepdims=True) + jnp.sum(p_ctx, axis=-1, keepdims=True)
        o = _dot(p_band.astype(BF16), vb) + _dot(p_ctx.astype(BF16), vc)
        o_ref[0, pl.ds(q0, NA_QBLOCK), :] = (o / l).astype(o_ref.dtype)
        return carry

    lax.fori_loop(0, SEQ // NA_QBLOCK, body, 0)


def _na_attention(q, k, v, kc, vc, bias):
    return pl.pallas_call(
        _na_kernel,
        grid=(BATCH, N_HEADS),
        in_specs=[
            pl.BlockSpec((1, SEQ, LANES), lambda b, h: (h, b, 0)),
            pl.BlockSpec((1, SEQ, LANES), lambda b, h: (h, b, 0)),
            pl.BlockSpec((1, SEQ, LANES), lambda b, h: (h, b, 0)),
            pl.BlockSpec((1, CTX_LEN, LANES), lambda b, h: (h, b, 0)),
            pl.BlockSpec((1, CTX_LEN, LANES), lambda b, h: (h, b, 0)),
            pl.BlockSpec((1, NA_PATTERNS, NA_QBLOCK, NA_BAND), lambda b, h: (h, 0, 0, 0)),
        ],
        out_specs=pl.BlockSpec((1, SEQ, LANES), lambda b, h: (h, b, 0)),
        out_shape=jax.ShapeDtypeStruct((N_HEADS, BATCH * SEQ, LANES), BF16),
        compiler_params=pltpu.CompilerParams(
            dimension_semantics=("arbitrary", "arbitrary"), vmem_limit_bytes=VMEM_LIMIT),
        name="attn_na",
    )(q, k, v, kc, vc, bias)


def _ctx_attn_kernel(qa_ref, ka_ref, va_ref, qb_ref, kb_ref, vb_ref, qc_ref, kc_ref, vc_ref,
                     qd_ref, kd_ref, vd_ref, lam_ref, gout_ref,
                     oa_ref, ob_ref, oc_ref, od_ref, *, lam_init):
    n = CTX_LEN
    lam = _diff_lambda(lam_ref, lam_init)
    for hd in range(N_HEADS):
        oa_ref[hd] = _flash(qa_ref[hd], None, va_ref.at[hd], ka_ref.at[hd], va_ref.at[hd],
                            n).astype(BF16)
        kvh = hd // (N_HEADS // GQA_KV_HEADS)
        ob_ref[hd] = _flash(qb_ref[hd], None, vb_ref.at[kvh], kb_ref.at[kvh], vb_ref.at[kvh],
                            n).astype(BF16)
        q2 = jnp.concatenate([qc_ref[2 * hd], qc_ref[2 * hd + 1]], axis=0)
        o2 = _flash(q2, None, vc_ref.at[hd], kc_ref.at[hd], vc_ref.at[hd], n)
        oc_ref[hd] = _diff_combine(o2, n, lam, gout_ref[...], lam_init).astype(BF16)
        od_ref[hd] = _flash(qd_ref[hd], None, vd_ref.at[hd], kd_ref.at[hd], vd_ref.at[hd],
                            n).astype(BF16)


def _ctx_attention(fc, lam_rows, g_out, lam_init):
    in_specs = [pl.BlockSpec((a.shape[0], CTX_LEN, a.shape[2]), lambda b: (0, b, 0)) for a in fc]
    in_specs += [pl.BlockSpec(lam_rows.shape, lambda b: (0, 0)),
                 pl.BlockSpec(g_out.shape, lambda b: (0, 0))]
    o_spec = pl.BlockSpec((N_HEADS, CTX_LEN, LANES), lambda b: (0, b, 0))
    o_shape = jax.ShapeDtypeStruct((N_HEADS, BATCH * CTX_LEN, LANES), BF16)
    return pl.pallas_call(
        functools.partial(_ctx_attn_kernel, lam_init=lam_init),
        grid=(BATCH,),
        in_specs=in_specs,
        out_specs=[o_spec] * 4,
        out_shape=[o_shape] * 4,
        compiler_params=pltpu.CompilerParams(
            dimension_semantics=("arbitrary",), vmem_limit_bytes=VMEM_LIMIT),
        name="attn_ctx",
    )(*fc, lam_rows, g_out)


def _out_kernel(x_ref, oa_ref, ob_ref, oc_ref, od_ref, w_ref, gt_ref, g_ref, sc_ref, sh_ref,
                xo_ref, h_ref):
    parts = [r[hd] for r in (oa_ref, ob_ref, oc_ref, od_ref) for hd in range(N_HEADS)]
    o = jnp.concatenate(parts, axis=-1)
    x = x_ref[...] + gt_ref[0] * _dot(o, w_ref[...])
    xo_ref[...] = x
    h_ref[...] = (_rms(x) * (g_ref[...] * (1.0 + sc_ref[0])) + sh_ref[0]).astype(BF16)


def _out_proj(x2d, outs, w_out, gt, g_mlp, sc, sh, *, rows_per_mod, tm=512):
    t = x2d.shape[0]
    tiles_per_mod = rows_per_mod // tm
    mod_spec = pl.BlockSpec((1, 1, D_MODEL), lambda i: (i // tiles_per_mod, 0, 0))
    o_spec = pl.BlockSpec((N_HEADS, tm, LANES), lambda i: (0, i, 0))
    row_spec = pl.BlockSpec((tm, D_MODEL), lambda i: (i, 0))
    return pl.pallas_call(
        _out_kernel,
        grid=(t // tm,),
        in_specs=[row_spec, o_spec, o_spec, o_spec, o_spec,
                  pl.BlockSpec((D_MODEL, D_MODEL), lambda i: (0, 0), pipeline_mode=pl.Buffered(1)),
                  mod_spec, pl.BlockSpec((1, D_MODEL), lambda i: (0, 0)), mod_spec, mod_spec],
        out_specs=[row_spec, row_spec],
        out_shape=[jax.ShapeDtypeStruct((t, D_MODEL), F32),
                   jax.ShapeDtypeStruct((t, D_MODEL), BF16)],
        compiler_params=pltpu.CompilerParams(
            dimension_semantics=("arbitrary",), vmem_limit_bytes=VMEM_LIMIT),
        name="out_proj",
    )(x2d, *outs, w_out, gt, g_mlp, sc, sh)


def _mlp_kernel(h_ref, wu_ref, wd_ref, x_ref, gt_ref, o_ref):
    j = pl.program_id(1)
    u = jnp.maximum(_dot(h_ref[...], wu_ref[...]), 0.0)
    y = _dot((u * u).astype(BF16), wd_ref[...])

    @pl.when(j == 0)
    def _():
        o_ref[...] = y

    @pl.when(j > 0)
    def _():
        o_ref[...] += y

    @pl.when(j == pl.num_programs(1) - 1)
    def _():
        o_ref[...] = x_ref[...] + gt_ref[0] * o_ref[...]


def _mlp(h2d, x2d, w_up, w_down, gt, *, rows_per_mod, tm=1024, tf=512):
    t = x2d.shape[0]
    tiles_per_mod = rows_per_mod // tm
    return pl.pallas_call(
        _mlp_kernel,
        grid=(t // tm, D_FF // tf),
        in_specs=[
            pl.BlockSpec((tm, D_MODEL), lambda i, j: (i, 0)),
            pl.BlockSpec((D_MODEL, tf), lambda i, j: (0, j)),
            pl.BlockSpec((tf, D_MODEL), lambda i, j: (j, 0)),
            pl.BlockSpec((tm, D_MODEL), lambda i, j: (i, 0), pipeline_mode=pl.Buffered(1)),
            pl.BlockSpec((1, 1, D_MODEL), lambda i, j: (i // tiles_per_mod, 0, 0)),
        ],
        out_specs=pl.BlockSpec((tm, D_MODEL), lambda i, j: (i, 0)),
        out_shape=jax.ShapeDtypeStruct((t, D_MODEL), F32),
        compiler_params=pltpu.CompilerParams(
            dimension_semantics=("arbitrary", "arbitrary"), vmem_limit_bytes=VMEM_LIMIT),
        name="mlp",
    )(h2d, w_up, w_down, x2d, gt)


def _rope_tables(tm):
    t = jnp.arange(SEQ)
    row = (t // GRID_W).astype(F32)
    col = (t % GRID_W).astype(F32)

    def tables(rot_dim):
        half = rot_dim // 2
        inv_freq = ROPE_THETA ** (-jnp.arange(0, half, 2, dtype=F32) / half)
        ar = row[:, None] * inv_freq
        ac = col[:, None] * inv_freq
        cos = jnp.concatenate([jnp.cos(ar), jnp.cos(ar), jnp.cos(ac), jnp.cos(ac)], axis=-1)
        sin = jnp.concatenate([-jnp.sin(ar), jnp.sin(ar), -jnp.sin(ac), jnp.sin(ac)], axis=-1)
        return cos, sin

    cos128, sin128 = tables(HEAD_DIM)
    cos64, sin64 = tables(DIFF_HALF)
    lat = (cos128, sin128, jnp.tile(cos64, (1, 2)), jnp.tile(sin64, (1, 2)))
    ones = jnp.ones((tm, LANES), F32)
    zeros = jnp.zeros((tm, LANES), F32)
    return lat, (ones, zeros, ones, zeros)


def _layer_weights(l, w_in, mla_w_uq, mla_w_ukv, g_norm_mix, mla_g_qa, mla_g_kva, mla_g_q,
                   mla_g_k, gqa_g_q, gqa_g_k, diff_g_q, diff_g_k, diff_g_out, na_g_q, na_g_k):
    wi = w_in[l]
    a_end = A_COLS
    kpe = wi[:, MLA_Q_RANK + MLA_KV_RANK:a_end]
    w_in_r = jnp.concatenate(
        [wi[:, :MLA_Q_RANK + MLA_KV_RANK], wi[:, a_end:], kpe, kpe], axis=1).astype(BF16)
    uq = mla_w_uq[l].reshape(MLA_Q_RANK, N_HEADS, MLA_NOPE + MLA_ROPE)
    w_uq_r = jnp.concatenate(
        [uq[:, :, :MLA_NOPE].reshape(MLA_Q_RANK, -1), uq[:, :, MLA_NOPE:].reshape(MLA_Q_RANK, -1)],
        axis=1).astype(BF16)
    ukv = mla_w_ukv[l].reshape(MLA_KV_RANK, N_HEADS, MLA_NOPE + HEAD_DIM)
    w_ukv_r = jnp.concatenate(
        [ukv[:, :, :MLA_NOPE].reshape(MLA_KV_RANK, -1), ukv[:, :, MLA_NOPE:].reshape(MLA_KV_RANK, -1)],
        axis=1).astype(BF16)

    scale_a = (MLA_NOPE + MLA_ROPE) ** -0.5
    scale_b = HEAD_DIM ** -0.5
    scale_c = DIFF_HALF ** -0.5
    scale_d = HEAD_DIM ** -0.5
    two = lambda g: jnp.tile(g, 2)
    rows = [None] * 11
    rows[G_A_QN] = mla_g_q[l][:MLA_NOPE] * scale_a
    rows[G_A_QR] = two(mla_g_q[l][MLA_NOPE:]) * scale_a
    rows[G_A_KN] = mla_g_k[l][:MLA_NOPE]
    rows[G_A_KR] = two(mla_g_k[l][MLA_NOPE:])
    rows[G_B_Q] = gqa_g_q[l] * scale_b
    rows[G_B_K] = gqa_g_k[l]
    rows[G_C_Q] = two(diff_g_q[l]) * scale_c
    rows[G_C_K] = two(diff_g_k[l])
    rows[G_D_Q] = na_g_q[l] * scale_d
    rows[G_D_K] = na_g_k[l]
    rows[G_C_OUT] = diff_g_out[l]
    gains = jnp.stack(rows + [jnp.zeros((LANES,), F32)] * (GAIN_ROWS - len(rows))).astype(F32)
    return {
        "w_in": w_in_r, "w_uq": w_uq_r, "w_ukv": w_ukv_r,
        "g_mix": g_norm_mix[l].reshape(1, D_MODEL),
        "g_qa": mla_g_qa[l].reshape(1, -1), "g_kva": mla_g_kva[l].reshape(1, -1),
        "gains": gains,
    }


def kernel(x, c, ctx, c_ctx, w_mod, b_mod, g_norm_mix, g_norm_mlp, w_in, mla_g_qa, mla_g_kva,
           mla_w_uq, mla_w_ukv, mla_g_q, mla_g_k, gqa_g_q, gqa_g_k, diff_g_q, diff_g_k,
           diff_lq1, diff_lk1, diff_lq2, diff_lk2, diff_g_out, na_g_q, na_g_k, na_rpb,
           w_out, w_up, w_down):
    tm_front = 256
    cc = jnp.concatenate([c, c_ctx[None, :], jnp.zeros((8 - BATCH - 1, D_MODEL), F32)], axis=0)
    mod = _modulation(cc, w_mod, b_mod)
    lat_tabs, ctx_tabs = _rope_tables(tm_front)

    xs = x.reshape(BATCH * SEQ, D_MODEL)
    cs = ctx.reshape(BATCH * CTX_LEN, D_MODEL)
    for l in range(DEPTH):
        need_ctx = l < DEPTH - 1
        lam_init = 0.8 - 0.6 * math.exp(-0.3 * l)
        m6 = mod[l].reshape(8, 6, 1, D_MODEL)
        sh_a, sc_a, gt_a, sh_m, sc_m, gt_m = [m6[:BATCH, k] for k in range(6)]
        csh_a, csc_a, cgt_a, csh_m, csc_m, cgt_m = [m6[BATCH:BATCH + 1, k] for k in range(6)]
        lw = _layer_weights(l, w_in, mla_w_uq, mla_w_ukv, g_norm_mix, mla_g_qa, mla_g_kva,
                            mla_g_q, mla_g_k, gqa_g_q, gqa_g_k, diff_g_q, diff_g_k, diff_g_out,
                            na_g_q, na_g_k)
        w_out_b = w_out[l].astype(BF16)
        w_up_b = w_up[l].astype(BF16)
        w_down_b = w_down[l].astype(BF16)
        g_mlp = g_norm_mlp[l].reshape(1, D_MODEL)
        lam_rows = jnp.stack([diff_lq1[l], diff_lk1[l], diff_lq2[l], diff_lk2[l]]).astype(F32)
        g_out = lw["gains"][G_C_OUT:G_C_OUT + 1]

        fl = _front(xs, sc_a, sh_a, lw, lat_tabs, rows_per_mod=SEQ, tm=tm_front)
        fc = _front(cs, csc_a, csh_a, lw, ctx_tabs, rows_per_mod=BATCH * CTX_LEN, tm=tm_front)
        qa, ka, va, qb, kb, vb, qc, kc, vc, qd, kd, vd = fl
        cqa, cka, cva, cqb, ckb, cvb, cqc, ckc, cvc, cqd, ckd, cvd = fc

        o_a = _attention(qa, ka, va, cka, cva, kind="mla")
        o_b = _attention(qb, kb, vb, ckb, cvb, kind="gqa")
        o_c = _attention(qc, kc, vc, ckc, cvc, kind="diff", extra=(lam_rows, g_out),
                         lam_init=lam_init)
        o_d = _na_attention(qd, kd, vd, ckd, cvd, _na_bias(na_rpb[l]))

        x_mid, h_mlp = _out_proj(xs, (o_a, o_b, o_c, o_d), w_out_b, gt_a, g_mlp, sc_m, sh_m,
                                 rows_per_mod=SEQ)
        xs = _mlp(h_mlp, x_mid, w_up_b, w_down_b, gt_m, rows_per_mod=SEQ)

        if need_ctx:
            oc = _ctx_attention(fc, lam_rows, g_out, lam_init)
            c_mid, ch_mlp = _out_proj(cs, oc, w_out_b, cgt_a, g_mlp, csc_m, csh_m,
                                      rows_per_mod=BATCH * CTX_LEN)
            cs = _mlp(ch_mlp, c_mid, w_up_b, w_down_b, cgt_m, rows_per_mod=BATCH * CTX_LEN)
    return xs.reshape(BATCH, SEQ, D_MODEL)
```

```python
import functools
import math

import numpy as np
import jax
import jax.numpy as jnp
from jax import lax
from jax.experimental import pallas as pl
from jax.experimental.pallas import tpu as pltpu

D_MODEL = 2048
BATCH = 4
SEQ = 4096
DEPTH = 2
GRID_W = 64
CTX_LEN = 256
HEAD_DIM = 128
GROUP_W = D_MODEL // 4
N_HEADS = GROUP_W // HEAD_DIM
D_FF = 4 * D_MODEL
ROPE_THETA = 10000.0
NORM_EPS = 1e-6

MLA_Q_RANK = GROUP_W
MLA_KV_RANK = GROUP_W // 2
MLA_NOPE = 128
MLA_ROPE = 64
GQA_KV_HEADS = N_HEADS // 2
DIFF_HALF = HEAD_DIM // 2
NA_WIN_H = 8
NA_WIN_W = 16
NA_BAND_ROWS = NA_WIN_H + 1
NA_BAND = NA_BAND_ROWS * GRID_W
NA_QBLOCK = 128
NA_PATTERNS = 5

A_COLS = MLA_Q_RANK + MLA_KV_RANK + MLA_ROPE
B_COLS = (N_HEADS + 2 * GQA_KV_HEADS) * HEAD_DIM
C_COLS = 3 * N_HEADS * HEAD_DIM
D_COLS = 3 * N_HEADS * HEAD_DIM

OFF_AQ = 0
OFF_AKV = OFF_AQ + MLA_Q_RANK
OFF_B = OFF_AKV + MLA_KV_RANK
OFF_C = OFF_B + B_COLS
OFF_D = OFF_C + C_COLS
OFF_KPE = OFF_D + D_COLS
IN_COLS_PAD = OFF_KPE + 128

LANES = 128
VMEM_LIMIT = 56 * 1024 * 1024
NEG_BIG = -1e30
LOG2E = math.log2(math.e)
MAX_EXP2_SPAN = 96.0

F32 = jnp.float32
BF16 = jnp.bfloat16

G_A_QN, G_A_QR, G_A_KN, G_A_KR, G_B_Q, G_B_K, G_C_Q, G_C_K, G_D_Q, G_D_K, G_C_OUT = range(11)
GAIN_ROWS = 16


def _lane_iota(shape):
    return lax.broadcasted_iota(jnp.int32, shape, len(shape) - 1)


def _rms(t):
    return t * lax.rsqrt(jnp.mean(t * t, axis=-1, keepdims=True) + NORM_EPS)


def _rms64(t):
    lo = _lane_iota(t.shape) < 64
    sq = t * t
    s_lo = jnp.sum(jnp.where(lo, sq, 0.0), axis=-1, keepdims=True)
    s_hi = jnp.sum(jnp.where(lo, 0.0, sq), axis=-1, keepdims=True)
    ms = jnp.where(lo, s_lo, s_hi) * (1.0 / 64.0)
    return t * lax.rsqrt(ms + NORM_EPS)


def _rope(t, cos, sin_signed, half):
    first = (_lane_iota(t.shape) % (2 * half)) < half
    partner = jnp.where(first, pltpu.roll(t, LANES - half, 1), pltpu.roll(t, half, 1))
    return t * cos + partner * sin_signed


def _dot(a, b):
    return jnp.dot(a, b, preferred_element_type=F32)


def _dot_nt(a, b):
    return lax.dot_general(a, b, (((1,), (1,)), ((), ())), preferred_element_type=F32)


def _mod_kernel(c_ref, w_ref, b_ref, o_ref):
    c = c_ref[...]
    a = (c * jax.nn.sigmoid(c)).astype(BF16)
    o_ref[0] = _dot(a, w_ref[0].astype(BF16)) + b_ref[0]


def _modulation(cc, w_mod, b_mod):
    tn = 1536
    n = 6 * D_MODEL
    return pl.pallas_call(
        _mod_kernel,
        grid=(DEPTH, n // tn),
        in_specs=[
            pl.BlockSpec((8, D_MODEL), lambda l, j: (0, 0)),
            pl.BlockSpec((1, D_MODEL, tn), lambda l, j: (l, 0, j)),
            pl.BlockSpec((1, 1, tn), lambda l, j: (l, 0, j)),
        ],
        out_specs=pl.BlockSpec((1, 8, tn), lambda l, j: (l, 0, j)),
        out_shape=jax.ShapeDtypeStruct((DEPTH, 8, n), F32),
        compiler_params=pltpu.CompilerParams(
            dimension_semantics=("arbitrary", "arbitrary"), vmem_limit_bytes=VMEM_LIMIT),
        name="modulation",
    )(cc, w_mod, b_mod.reshape(DEPTH, 1, n))


def _front_kernel(x_ref, gmix_ref, sc_ref, sh_ref, w_in_ref, w_uq_ref, w_ukv_ref,
                  gqa_ref, gkva_ref, gains_ref, cos128_ref, sin128_ref, cos64_ref, sin64_ref,
                  qa_ref, ka_ref, va_ref, qb_ref, kb_ref, vb_ref,
                  qc_ref, kc_ref, vc_ref, qd_ref, kd_ref, vd_ref):
    x = x_ref[...]
    gm = gmix_ref[...] * (1.0 + sc_ref[0])
    h = (_rms(x) * gm + sh_ref[0]).astype(BF16)

    gains = gains_ref[...]

    def gain(row):
        return gains[row:row + 1, :]

    cos128 = cos128_ref[...]
    sin128 = sin128_ref[...]
    cos64 = cos64_ref[...]
    sin64 = sin64_ref[...]
    lo = _lane_iota((x.shape[0], LANES)) < 64

    def proj(off, width):
        return _dot(h, w_in_ref[:, off:off + width])

    def tile(t, j):
        return t[:, j * LANES:(j + 1) * LANES]

    cq = (_rms(proj(OFF_AQ, MLA_Q_RANK)) * gqa_ref[...]).astype(BF16)
    ckv = (_rms(proj(OFF_AKV, MLA_KV_RANK)) * gkva_ref[...]).astype(BF16)
    kr = _rope(_rms64(proj(OFF_KPE, LANES)) * gain(G_A_KR), cos64, sin64, 16).astype(BF16)
    q = _dot(cq, w_uq_ref[...])
    kv = _dot(ckv, w_ukv_ref[...])
    qr_pairs = [
        _rope(_rms64(tile(q, N_HEADS + j)) * gain(G_A_QR), cos64, sin64, 16)
        for j in range(N_HEADS // 2)
    ]
    for hd in range(N_HEADS):
        qa_ref[hd, :, 0:LANES] = (_rms(tile(q, hd)) * gain(G_A_QN)).astype(BF16)
        pair = qr_pairs[hd // 2]
        keep = lo if hd % 2 == 0 else jnp.logical_not(lo)
        qa_ref[hd, :, LANES:2 * LANES] = jnp.where(keep, pair, 0.0).astype(BF16)
        ka_ref[hd, :, 0:LANES] = (_rms(tile(kv, hd)) * gain(G_A_KN)).astype(BF16)
        ka_ref[hd, :, LANES:2 * LANES] = kr
        va_ref[hd] = tile(kv, N_HEADS + hd).astype(BF16)

    pb = proj(OFF_B, B_COLS)
    for hd in range(N_HEADS):
        qb_ref[hd] = _rope(_rms(tile(pb, hd)) * gain(G_B_Q), cos128, sin128, 32).astype(BF16)
    for hd in range(GQA_KV_HEADS):
        kb_ref[hd] = _rope(_rms(tile(pb, N_HEADS + hd)) * gain(G_B_K),
                           cos128, sin128, 32).astype(BF16)
        vb_ref[hd] = tile(pb, N_HEADS + GQA_KV_HEADS + hd).astype(BF16)

    pc = proj(OFF_C, C_COLS)
    for hd in range(N_HEADS):
        qh = _rope(_rms64(tile(pc, hd)) * gain(G_C_Q), cos64, sin64, 16)
        qc_ref[2 * hd] = jnp.where(lo, qh, 0.0).astype(BF16)
        qc_ref[2 * hd + 1] = jnp.where(lo, 0.0, qh).astype(BF16)
        kc_ref[hd] = _rope(_rms64(tile(pc, N_HEADS + hd)) * gain(G_C_K),
                           cos64, sin64, 16).astype(BF16)
        vc_ref[hd] = tile(pc, 2 * N_HEADS + hd).astype(BF16)

    pd = proj(OFF_D, D_COLS)
    for hd in range(N_HEADS):
        qd_ref[hd] = (_rms(tile(pd, hd)) * gain(G_D_Q)).astype(BF16)
        kd_ref[hd] = (_rms(tile(pd, N_HEADS + hd)) * gain(G_D_K)).astype(BF16)
        vd_ref[hd] = tile(pd, 2 * N_HEADS + hd).astype(BF16)


_FRONT_OUT = (
    (N_HEADS, 2 * LANES), (N_HEADS, 2 * LANES), (N_HEADS, LANES),
    (N_HEADS, LANES), (GQA_KV_HEADS, LANES), (GQA_KV_HEADS, LANES),
    (2 * N_HEADS, LANES), (N_HEADS, LANES), (N_HEADS, LANES),
    (N_HEADS, LANES), (N_HEADS, LANES), (N_HEADS, LANES),
)


def _front(x2d, sc, sh, lw, tabs, *, rows_per_mod, tm=256):
    t = x2d.shape[0]
    n_pos_tiles = tabs[0].shape[0] // tm
    tiles_per_mod = rows_per_mod // tm

    def const(shape):
        return pl.BlockSpec(shape, lambda i: (0,) * len(shape), pipeline_mode=pl.Buffered(1))

    mod_spec = pl.BlockSpec((1, 1, D_MODEL), lambda i: (i // tiles_per_mod, 0, 0))
    tab_spec = pl.BlockSpec((tm, LANES), lambda i: (i % n_pos_tiles, 0))
    in_specs = [
        pl.BlockSpec((tm, D_MODEL), lambda i: (i, 0)),
        const((1, D_MODEL)), mod_spec, mod_spec,
        const((D_MODEL, IN_COLS_PAD)),
        const((MLA_Q_RANK, N_HEADS * (MLA_NOPE + MLA_ROPE))),
        const((MLA_KV_RANK, 2 * N_HEADS * LANES)),
        const((1, MLA_Q_RANK)), const((1, MLA_KV_RANK)), const((GAIN_ROWS, LANES)),
        tab_spec, tab_spec, tab_spec, tab_spec,
    ]
    out_specs = [pl.BlockSpec((nh, tm, w), lambda i: (0, i, 0)) for nh, w in _FRONT_OUT]
    out_shape = [jax.ShapeDtypeStruct((nh, t, w), BF16) for nh, w in _FRONT_OUT]
    return pl.pallas_call(
        _front_kernel,
        grid=(t // tm,),
        in_specs=in_specs,
        out_specs=out_specs,
        out_shape=out_shape,
        compiler_params=pltpu.CompilerParams(
            dimension_semantics=("arbitrary",), vmem_limit_bytes=VMEM_LIMIT),
        name="front",
    )(x2d, lw["g_mix"], sc, sh, lw["w_in"], lw["w_uq"], lw["w_ukv"],
      lw["g_qa"], lw["g_kva"], lw["gains"], *tabs)


def _softmax_step(q, k, v, m, l, acc):
    s = _dot_nt(q, k)
    m_new = jnp.maximum(m, jnp.max(s, axis=-1, keepdims=True))
    alpha = jnp.exp2(m - m_new)
    p = jnp.exp2(s - m_new)
    l_new = alpha * l + jnp.sum(p, axis=-1, keepdims=True)
    acc_new = alpha * acc + _dot(p.astype(BF16), v)
    return m_new, l_new, acc_new


def _fixed_shift_step(q, k, v, shift, l, acc):
    p = jnp.exp2(_dot_nt(q, k) - shift)
    return l + jnp.sum(p, axis=-1, keepdims=True), acc + _dot(p.astype(BF16), v)


def _flash(q, k_ref, v_ref, kc_ref, vc_ref, tk, shift=None):
    rows = q.shape[0]
    dv = v_ref.shape[-1]
    m = jnp.full((rows, 1), NEG_BIG, F32)
    l = jnp.zeros((rows, 1), F32)
    acc = jnp.zeros((rows, dv), F32)
    chunks = []
    if k_ref is not None:
        chunks = [(k_ref[c * tk:(c + 1) * tk, :], v_ref[c * tk:(c + 1) * tk, :])
                  for c in range(k_ref.shape[0] // tk)]
    chunks.append((kc_ref[...], vc_ref[...]))
    for k, v in chunks:
        if shift is None:
            m, l, acc = _softmax_step(q, k, v, m, l, acc)
        else:
            l, acc = _fixed_shift_step(q, k, v, shift, l, acc)
    return acc / l


def _diff_lambda(lam_ref, lam_init):
    lam1 = jnp.exp(jnp.sum(lam_ref[0:1, :] * lam_ref[1:2, :], axis=-1, keepdims=True))
    lam2 = jnp.exp(jnp.sum(lam_ref[2:3, :] * lam_ref[3:4, :], axis=-1, keepdims=True))
    return lam1 - lam2 + lam_init


def _diff_combine(o, tq, lam, g_out, lam_init):
    d = o[:tq] - lam * o[tq:]
    return _rms(d) * (g_out * (1.0 - lam_init))


def _attn_kernel(bound_ref, q_ref, k_ref, v_ref, kc_ref, vc_ref, *rest, stacked, diff,
                 lam_init, tk):
    if diff:
        lam_ref, gout_ref, o_ref = rest
    else:
        (o_ref,) = rest
    tq = q_ref.shape[1]

    def run(shift):
        if stacked:
            q = jnp.concatenate([q_ref[0], q_ref[1]], axis=0)
        else:
            q = q_ref[0]
        o = _flash(q, k_ref.at[0], v_ref.at[0], kc_ref.at[0], vc_ref.at[0], tk, shift)
        if diff:
            lam = _diff_lambda(lam_ref, lam_init)
            o_ref[0] = _diff_combine(o, tq, lam, gout_ref[...], lam_init).astype(o_ref.dtype)
        elif stacked:
            o_ref[0] = o[:tq].astype(o_ref.dtype)
            o_ref[1] = o[tq:].astype(o_ref.dtype)
        else:
            o_ref[0] = o.astype(o_ref.dtype)

    no_underflow = bound_ref[1] <= MAX_EXP2_SPAN

    @pl.when(no_underflow)
    def _():
        run(bound_ref[0])

    @pl.when(jnp.logical_not(no_underflow))
    def _():
        run(None)


def _attention(bound, q, k, v, kc, vc, *, kind, extra=(), lam_init=0.0, tq=512, tk=1024):
    nq = SEQ // tq
    dk = q.shape[-1]
    stacked = kind in ("gqa", "diff")
    n_outer = GQA_KV_HEADS if kind == "gqa" else N_HEADS
    q_heads = 2 if stacked else 1
    o_heads = 2 if kind == "gqa" else 1
    in_specs = [
        pl.BlockSpec(memory_space=pltpu.SMEM),
        pl.BlockSpec((q_heads, tq, dk), lambda b, h, i: (h, b * nq + i, 0)),
        pl.BlockSpec((1, SEQ, dk), lambda b, h, i: (h, b, 0)),
        pl.BlockSpec((1, SEQ, LANES), lambda b, h, i: (h, b, 0)),
        pl.BlockSpec((1, CTX_LEN, dk), lambda b, h, i: (h, b, 0)),
        pl.BlockSpec((1, CTX_LEN, LANES), lambda b, h, i: (h, b, 0)),
    ]
    for e in extra:
        in_specs.append(pl.BlockSpec(e.shape, lambda b, h, i, nd=e.ndim: (0,) * nd))
    return pl.pallas_call(
        functools.partial(_attn_kernel, stacked=stacked, diff=(kind == "diff"),
                          lam_init=lam_init, tk=tk),
        grid=(BATCH, n_outer, nq),
        in_specs=in_specs,
        out_specs=pl.BlockSpec((o_heads, tq, LANES), lambda b, h, i: (h, b * nq + i, 0)),
        out_shape=jax.ShapeDtypeStruct((N_HEADS, BATCH * SEQ, LANES), BF16),
        compiler_params=pltpu.CompilerParams(
            dimension_semantics=("arbitrary", "arbitrary", "arbitrary"),
            vmem_limit_bytes=VMEM_LIMIT),
        name="attn_" + kind,
    )(bound, q, k, v, kc, vc, *extra)


def _na_pattern_of(i):
    n_blocks = SEQ // NA_QBLOCK
    return jnp.where(i == 0, 0, jnp.where(i == 1, 1, jnp.where(
        i == n_blocks - 2, 3, jnp.where(i == n_blocks - 1, 4, 2))))


def _na_geometry(i):
    rows = SEQ // GRID_W
    rows_per_block = NA_QBLOCK // GRID_W
    r0 = i * rows_per_block
    band_start = min(min(max(r0 - NA_WIN_H // 2, 0), rows - NA_WIN_H), rows - NA_BAND_ROWS)
    rel = np.zeros((rows_per_block, NA_BAND_ROWS), np.int64)
    valid = np.zeros((rows_per_block, NA_BAND_ROWS), bool)
    for a in range(rows_per_block):
        q_row = r0 + a
        row_start = min(max(q_row - NA_WIN_H // 2, 0), rows - NA_WIN_H)
        for j in range(NA_BAND_ROWS):
            k_row = band_start + j
            valid[a, j] = row_start <= k_row < row_start + NA_WIN_H
            rel[a, j] = min(max(k_row - q_row + NA_WIN_H - 1, 0), 2 * NA_WIN_H - 2)
    return band_start, rel, valid


_NA_REPRESENTATIVE = (0, 1, 2, SEQ // NA_QBLOCK - 2, SEQ // NA_QBLOCK - 1)


def _na_bias_kernel(rpb_ref, o_ref):
    hd = pl.program_id(0)
    n_r = 2 * NA_WIN_H - 1
    n_c = 2 * NA_WIN_W - 1
    q_col = lax.broadcasted_iota(jnp.int32, (GRID_W, GRID_W), 0)
    k_col = lax.broadcasted_iota(jnp.int32, (GRID_W, GRID_W), 1)
    col_idx = jnp.clip(k_col - q_col + NA_WIN_W - 1, 0, n_c - 1)
    col_start = jnp.clip(q_col - NA_WIN_W // 2, 0, GRID_W - NA_WIN_W)
    col_ok = (k_col >= col_start) & (k_col < col_start + NA_WIN_W)
    tables = []
    for r in range(n_r):
        t = jnp.zeros((GRID_W, GRID_W), F32)
        for c in range(n_c):
            t = jnp.where(col_idx == c, rpb_ref[hd * (n_r * n_c) + r * n_c + c] * LOG2E, t)
        tables.append(jnp.where(col_ok, t, NEG_BIG))
    masked = jnp.full((GRID_W, GRID_W), NEG_BIG, F32)
    for p, i in enumerate(_NA_REPRESENTATIVE):
        _, rel, valid = _na_geometry(i)
        for a in range(rel.shape[0]):
            for j in range(rel.shape[1]):
                o_ref[0, p, a, j] = tables[int(rel[a, j])] if valid[a, j] else masked


def _na_bias(rpb):
    rows_per_block = NA_QBLOCK // GRID_W
    out = pl.pallas_call(
        _na_bias_kernel,
        grid=(N_HEADS,),
        in_specs=[pl.BlockSpec(memory_space=pltpu.SMEM)],
        out_specs=pl.BlockSpec((1, NA_PATTERNS, rows_per_block, NA_BAND_ROWS, GRID_W, GRID_W),
                               lambda h: (h, 0, 0, 0, 0, 0)),
        out_shape=jax.ShapeDtypeStruct(
            (N_HEADS, NA_PATTERNS, rows_per_block, NA_BAND_ROWS, GRID_W, GRID_W), F32),
        name="na_bias",
    )(rpb.reshape(-1))
    return out.transpose(0, 1, 2, 4, 3, 5).reshape(N_HEADS, NA_PATTERNS, NA_QBLOCK, NA_BAND)


def _na_kernel(bound_ref, q_ref, k_ref, v_ref, kc_ref, vc_ref, bias_ref, o_ref):
    rows = SEQ // GRID_W
    rows_per_block = NA_QBLOCK // GRID_W

    def run(shift):
        kc = kc_ref[0]
        vc = vc_ref[0]

        def body(i, carry):
            r0 = i * rows_per_block
            band_start = jnp.minimum(jnp.clip(r0 - NA_WIN_H // 2, 0, rows - NA_WIN_H),
                                     rows - NA_BAND_ROWS)
            k0 = pl.multiple_of(band_start * GRID_W, GRID_W)
            q0 = pl.multiple_of(i * NA_QBLOCK, NA_QBLOCK)
            q = q_ref[0, pl.ds(q0, NA_QBLOCK), :]
            kb = k_ref[0, pl.ds(k0, NA_BAND), :]
            vb = v_ref[0, pl.ds(k0, NA_BAND), :]
            s_band = _dot_nt(q, kb) + bias_ref[0, _na_pattern_of(i)]
            s_ctx = _dot_nt(q, kc)
            if shift is None:
                m = jnp.maximum(jnp.max(s_band, axis=-1, keepdims=True),
                                jnp.max(s_ctx, axis=-1, keepdims=True))
            else:
                m = shift
            p_band = jnp.exp2(s_band - m)
            p_ctx = jnp.exp2(s_ctx - m)
            l = (jnp.sum(p_band, axis=-1, keepdims=True)
                 + jnp.sum(p_ctx, axis=-1, keepdims=True))
            o = _dot(p_band.astype(BF16), vb) + _dot(p_ctx.astype(BF16), vc)
            o_ref[0, pl.ds(q0, NA_QBLOCK), :] = (o / l).astype(o_ref.dtype)
            return carry

        lax.fori_loop(0, SEQ // NA_QBLOCK, body, 0)

    no_underflow = bound_ref[1] <= MAX_EXP2_SPAN

    @pl.when(no_underflow)
    def _():
        run(bound_ref[0])

    @pl.when(jnp.logical_not(no_underflow))
    def _():
        run(None)


def _na_attention(bound, q, k, v, kc, vc, bias):
    return pl.pallas_call(
        _na_kernel,
        grid=(BATCH, N_HEADS),
        in_specs=[
            pl.BlockSpec(memory_space=pltpu.SMEM),
            pl.BlockSpec((1, SEQ, LANES), lambda b, h: (h, b, 0)),
            pl.BlockSpec((1, SEQ, LANES), lambda b, h: (h, b, 0)),
            pl.BlockSpec((1, SEQ, LANES), lambda b, h: (h, b, 0)),
            pl.BlockSpec((1, CTX_LEN, LANES), lambda b, h: (h, b, 0)),
            pl.BlockSpec((1, CTX_LEN, LANES), lambda b, h: (h, b, 0)),
            pl.BlockSpec((1, NA_PATTERNS, NA_QBLOCK, NA_BAND), lambda b, h: (h, 0, 0, 0)),
        ],
        out_specs=pl.BlockSpec((1, SEQ, LANES), lambda b, h: (h, b, 0)),
        out_shape=jax.ShapeDtypeStruct((N_HEADS, BATCH * SEQ, LANES), BF16),
        compiler_params=pltpu.CompilerParams(
            dimension_semantics=("arbitrary", "arbitrary"), vmem_limit_bytes=VMEM_LIMIT),
        name="attn_na",
    )(bound, q, k, v, kc, vc, bias)


def _ctx_attn_kernel(qa_ref, ka_ref, va_ref, qb_ref, kb_ref, vb_ref, qc_ref, kc_ref, vc_ref,
                     qd_ref, kd_ref, vd_ref, lam_ref, gout_ref,
                     oa_ref, ob_ref, oc_ref, od_ref, *, lam_init):
    n = CTX_LEN
    lam = _diff_lambda(lam_ref, lam_init)
    for hd in range(N_HEADS):
        oa_ref[hd] = _flash(qa_ref[hd], None, va_ref.at[hd], ka_ref.at[hd], va_ref.at[hd],
                            n).astype(BF16)
        kvh = hd // (N_HEADS // GQA_KV_HEADS)
        ob_ref[hd] = _flash(qb_ref[hd], None, vb_ref.at[kvh], kb_ref.at[kvh], vb_ref.at[kvh],
                            n).astype(BF16)
        q2 = jnp.concatenate([qc_ref[2 * hd], qc_ref[2 * hd + 1]], axis=0)
        o2 = _flash(q2, None, vc_ref.at[hd], kc_ref.at[hd], vc_ref.at[hd], n)
        oc_ref[hd] = _diff_combine(o2, n, lam, gout_ref[...], lam_init).astype(BF16)
        od_ref[hd] = _flash(qd_ref[hd], None, vd_ref.at[hd], kd_ref.at[hd], vd_ref.at[hd],
                            n).astype(BF16)


def _ctx_attention(fc, lam_rows, g_out, lam_init):
    in_specs = [pl.BlockSpec((a.shape[0], CTX_LEN, a.shape[2]), lambda b: (0, b, 0)) for a in fc]
    in_specs += [pl.BlockSpec(lam_rows.shape, lambda b: (0, 0)),
                 pl.BlockSpec(g_out.shape, lambda b: (0, 0))]
    o_spec = pl.BlockSpec((N_HEADS, CTX_LEN, LANES), lambda b: (0, b, 0))
    o_shape = jax.ShapeDtypeStruct((N_HEADS, BATCH * CTX_LEN, LANES), BF16)
    return pl.pallas_call(
        functools.partial(_ctx_attn_kernel, lam_init=lam_init),
        grid=(BATCH,),
        in_specs=in_specs,
        out_specs=[o_spec] * 4,
        out_shape=[o_shape] * 4,
        compiler_params=pltpu.CompilerParams(
            dimension_semantics=("arbitrary",), vmem_limit_bytes=VMEM_LIMIT),
        name="attn_ctx",
    )(*fc, lam_rows, g_out)


def _out_kernel(x_ref, oa_ref, ob_ref, oc_ref, od_ref, w_ref, gt_ref, g_ref, sc_ref, sh_ref,
                xo_ref, h_ref):
    parts = [r[hd] for r in (oa_ref, ob_ref, oc_ref, od_ref) for hd in range(N_HEADS)]
    o = jnp.concatenate(parts, axis=-1)
    x = x_ref[...] + gt_ref[0] * _dot(o, w_ref[...])
    xo_ref[...] = x
    h_ref[...] = (_rms(x) * (g_ref[...] * (1.0 + sc_ref[0])) + sh_ref[0]).astype(BF16)


def _out_proj(x2d, outs, w_out, gt, g_mlp, sc, sh, *, rows_per_mod, tm=512):
    t = x2d.shape[0]
    tiles_per_mod = rows_per_mod // tm
    mod_spec = pl.BlockSpec((1, 1, D_MODEL), lambda i: (i // tiles_per_mod, 0, 0))
    o_spec = pl.BlockSpec((N_HEADS, tm, LANES), lambda i: (0, i, 0))
    row_spec = pl.BlockSpec((tm, D_MODEL), lambda i: (i, 0))
    return pl.pallas_call(
        _out_kernel,
        grid=(t // tm,),
        in_specs=[row_spec, o_spec, o_spec, o_spec, o_spec,
                  pl.BlockSpec((D_MODEL, D_MODEL), lambda i: (0, 0), pipeline_mode=pl.Buffered(1)),
                  mod_spec, pl.BlockSpec((1, D_MODEL), lambda i: (0, 0)), mod_spec, mod_spec],
        out_specs=[row_spec, row_spec],
        out_shape=[jax.ShapeDtypeStruct((t, D_MODEL), F32),
                   jax.ShapeDtypeStruct((t, D_MODEL), BF16)],
        compiler_params=pltpu.CompilerParams(
            dimension_semantics=("arbitrary",), vmem_limit_bytes=VMEM_LIMIT),
        name="out_proj",
    )(x2d, *outs, w_out, gt, g_mlp, sc, sh)


def _mlp_kernel(h_ref, wu_ref, wd_ref, x_ref, gt_ref, o_ref):
    j = pl.program_id(1)

    @pl.when(j == 0)
    def _():
        o_ref[...] = jnp.zeros_like(o_ref)

    u = jnp.maximum(_dot(h_ref[...], wu_ref[...]), 0.0)
    o_ref[...] += _dot((u * u).astype(BF16), wd_ref[...])

    @pl.when(j == pl.num_programs(1) - 1)
    def _():
        o_ref[...] = x_ref[...] + gt_ref[0] * o_ref[...]


def _mlp(h2d, x2d, w_up, w_down, gt, *, rows_per_mod, tm=1024, tf=512):
    t = x2d.shape[0]
    tiles_per_mod = rows_per_mod // tm
    return pl.pallas_call(
        _mlp_kernel,
        grid=(t // tm, D_FF // tf),
        in_specs=[
            pl.BlockSpec((tm, D_MODEL), lambda i, j: (i, 0)),
            pl.BlockSpec((D_MODEL, tf), lambda i, j: (0, j)),
            pl.BlockSpec((tf, D_MODEL), lambda i, j: (j, 0)),
            pl.BlockSpec((tm, D_MODEL), lambda i, j: (i, 0), pipeline_mode=pl.Buffered(1)),
            pl.BlockSpec((1, 1, D_MODEL), lambda i, j: (i // tiles_per_mod, 0, 0)),
        ],
        out_specs=pl.BlockSpec((tm, D_MODEL), lambda i, j: (i, 0)),
        out_shape=jax.ShapeDtypeStruct((t, D_MODEL), F32),
        compiler_params=pltpu.CompilerParams(
            dimension_semantics=("arbitrary", "arbitrary"), vmem_limit_bytes=VMEM_LIMIT),
        name="mlp",
    )(h2d, w_up, w_down, x2d, gt)


def _rope_tables(tm):
    t = jnp.arange(SEQ)
    row = (t // GRID_W).astype(F32)
    col = (t % GRID_W).astype(F32)

    def tables(rot_dim):
        half = rot_dim // 2
        inv_freq = ROPE_THETA ** (-jnp.arange(0, half, 2, dtype=F32) / half)
        ar = row[:, None] * inv_freq
        ac = col[:, None] * inv_freq
        cos = jnp.concatenate([jnp.cos(ar), jnp.cos(ar), jnp.cos(ac), jnp.cos(ac)], axis=-1)
        sin = jnp.concatenate([-jnp.sin(ar), jnp.sin(ar), -jnp.sin(ac), jnp.sin(ac)], axis=-1)
        return cos, sin

    cos128, sin128 = tables(HEAD_DIM)
    cos64, sin64 = tables(DIFF_HALF)
    lat = (cos128, sin128, jnp.tile(cos64, (1, 2)), jnp.tile(sin64, (1, 2)))
    ones = jnp.ones((tm, LANES), F32)
    zeros = jnp.zeros((tm, LANES), F32)
    return lat, (ones, zeros, ones, zeros)


def _layer_weights(l, w_in, mla_w_uq, mla_w_ukv, g_norm_mix, mla_g_qa, mla_g_kva, mla_g_q,
                   mla_g_k, gqa_g_q, gqa_g_k, diff_g_q, diff_g_k, diff_g_out, na_g_q, na_g_k):
    wi = w_in[l]
    a_end = A_COLS
    kpe = wi[:, MLA_Q_RANK + MLA_KV_RANK:a_end]
    w_in_r = jnp.concatenate(
        [wi[:, :MLA_Q_RANK + MLA_KV_RANK], wi[:, a_end:], kpe, kpe], axis=1).astype(BF16)
    uq = mla_w_uq[l].reshape(MLA_Q_RANK, N_HEADS, MLA_NOPE + MLA_ROPE)
    w_uq_r = jnp.concatenate(
        [uq[:, :, :MLA_NOPE].reshape(MLA_Q_RANK, -1), uq[:, :, MLA_NOPE:].reshape(MLA_Q_RANK, -1)],
        axis=1).astype(BF16)
    ukv = mla_w_ukv[l].reshape(MLA_KV_RANK, N_HEADS, MLA_NOPE + HEAD_DIM)
    w_ukv_r = jnp.concatenate(
        [ukv[:, :, :MLA_NOPE].reshape(MLA_KV_RANK, -1), ukv[:, :, MLA_NOPE:].reshape(MLA_KV_RANK, -1)],
        axis=1).astype(BF16)

    scale_a = (MLA_NOPE + MLA_ROPE) ** -0.5 * LOG2E
    scale_b = HEAD_DIM ** -0.5 * LOG2E
    scale_c = DIFF_HALF ** -0.5 * LOG2E
    scale_d = HEAD_DIM ** -0.5 * LOG2E
    two = lambda g: jnp.tile(g, 2)
    rows = [None] * 11
    rows[G_A_QN] = mla_g_q[l][:MLA_NOPE] * scale_a
    rows[G_A_QR] = two(mla_g_q[l][MLA_NOPE:]) * scale_a
    rows[G_A_KN] = mla_g_k[l][:MLA_NOPE]
    rows[G_A_KR] = two(mla_g_k[l][MLA_NOPE:])
    rows[G_B_Q] = gqa_g_q[l] * scale_b
    rows[G_B_K] = gqa_g_k[l]
    rows[G_C_Q] = two(diff_g_q[l]) * scale_c
    rows[G_C_K] = two(diff_g_k[l])
    rows[G_D_Q] = na_g_q[l] * scale_d
    rows[G_D_K] = na_g_k[l]
    rows[G_C_OUT] = diff_g_out[l]
    gains = jnp.stack(rows + [jnp.zeros((LANES,), F32)] * (GAIN_ROWS - len(rows))).astype(F32)

    amax = lambda r: jnp.max(jnp.abs(rows[r]))
    slack = 1.02
    b_a = slack * (jnp.sqrt(MLA_NOPE * amax(G_A_QN) ** 2 + MLA_ROPE * amax(G_A_QR) ** 2)
                   * jnp.sqrt(MLA_NOPE * amax(G_A_KN) ** 2 + MLA_ROPE * amax(G_A_KR) ** 2))
    b_b = slack * HEAD_DIM * amax(G_B_Q) * amax(G_B_K)
    b_c = slack * DIFF_HALF * amax(G_C_Q) * amax(G_C_K)
    b_d = slack * HEAD_DIM * amax(G_D_Q) * amax(G_D_K)
    span = lambda b: jnp.stack([b, 2.0 * b]).astype(F32)
    return {
        "bound_a": span(b_a), "bound_b": span(b_b), "bound_c": span(b_c), "qk_bound_d": b_d,
        "w_in": w_in_r, "w_uq": w_uq_r, "w_ukv": w_ukv_r,
        "g_mix": g_norm_mix[l].reshape(1, D_MODEL),
        "g_qa": mla_g_qa[l].reshape(1, -1), "g_kva": mla_g_kva[l].reshape(1, -1),
        "gains": gains,
    }


def kernel(x, c, ctx, c_ctx, w_mod, b_mod, g_norm_mix, g_norm_mlp, w_in, mla_g_qa, mla_g_kva,
           mla_w_uq, mla_w_ukv, mla_g_q, mla_g_k, gqa_g_q, gqa_g_k, diff_g_q, diff_g_k,
           diff_lq1, diff_lk1, diff_lq2, diff_lk2, diff_g_out, na_g_q, na_g_k, na_rpb,
           w_out, w_up, w_down):
    tm_front = 256
    cc = jnp.concatenate([c, c_ctx[None, :], jnp.zeros((8 - BATCH - 1, D_MODEL), F32)], axis=0)
    mod = _modulation(cc, w_mod, b_mod)
    lat_tabs, ctx_tabs = _rope_tables(tm_front)

    xs = x.reshape(BATCH * SEQ, D_MODEL)
    cs = ctx.reshape(BATCH * CTX_LEN, D_MODEL)
    for l in range(DEPTH):
        need_ctx = l < DEPTH - 1
        lam_init = 0.8 - 0.6 * math.exp(-0.3 * l)
        m6 = mod[l].reshape(8, 6, 1, D_MODEL)
        sh_a, sc_a, gt_a, sh_m, sc_m, gt_m = [m6[:BATCH, k] for k in range(6)]
        csh_a, csc_a, cgt_a, csh_m, csc_m, cgt_m = [m6[BATCH:BATCH + 1, k] for k in range(6)]
        lw = _layer_weights(l, w_in, mla_w_uq, mla_w_ukv, g_norm_mix, mla_g_qa, mla_g_kva,
                            mla_g_q, mla_g_k, gqa_g_q, gqa_g_k, diff_g_q, diff_g_k, diff_g_out,
                            na_g_q, na_g_k)
        w_out_b = w_out[l].astype(BF16)
        w_up_b = w_up[l].astype(BF16)
        w_down_b = w_down[l].astype(BF16)
        g_mlp = g_norm_mlp[l].reshape(1, D_MODEL)
        lam_rows = jnp.stack([diff_lq1[l], diff_lk1[l], diff_lq2[l], diff_lk2[l]]).astype(F32)
        g_out = lw["gains"][G_C_OUT:G_C_OUT + 1]

        fl = _front(xs, sc_a, sh_a, lw, lat_tabs, rows_per_mod=SEQ, tm=tm_front)
        fc = _front(cs, csc_a, csh_a, lw, ctx_tabs, rows_per_mod=BATCH * CTX_LEN, tm=tm_front)
        qa, ka, va, qb, kb, vb, qc, kc, vc, qd, kd, vd = fl
        cqa, cka, cva, cqb, ckb, cvb, cqc, ckc, cvc, cqd, ckd, cvd = fc

        o_a = _attention(lw["bound_a"], qa, ka, va, cka, cva, kind="mla", tq=1024)
        o_b = _attention(lw["bound_b"], qb, kb, vb, ckb, cvb, kind="gqa", tq=512)
        o_c = _attention(lw["bound_c"], qc, kc, vc, ckc, cvc, kind="diff", tq=512,
                         extra=(lam_rows, g_out), lam_init=lam_init)
        bias_hi = jnp.maximum(jnp.max(na_rpb[l]), 0.0) * LOG2E
        bias_lo = jnp.minimum(jnp.min(na_rpb[l]), 0.0) * LOG2E
        bound_d = jnp.stack([lw["qk_bound_d"] + bias_hi,
                             2.0 * lw["qk_bound_d"] + bias_hi - bias_lo]).astype(F32)
        o_d = _na_attention(bound_d, qd, kd, vd, ckd, cvd, _na_bias(na_rpb[l]))

        x_mid, h_mlp = _out_proj(xs, (o_a, o_b, o_c, o_d), w_out_b, gt_a, g_mlp, sc_m, sh_m,
                                 rows_per_mod=SEQ)
        xs = _mlp(h_mlp, x_mid, w_up_b, w_down_b, gt_m, rows_per_mod=SEQ)

        if need_ctx:
            oc = _ctx_attention(fc, lam_rows, g_out, lam_init)
            c_mid, ch_mlp = _out_proj(cs, oc, w_out_b, cgt_a, g_mlp, csc_m, csh_m,
                                      rows_per_mod=BATCH * CTX_LEN)
            cs = _mlp(ch_mlp, c_mid, w_up_b, w_down_b, cgt_m, rows_per_mod=BATCH * CTX_LEN)
    return xs.reshape(BATCH, SEQ, D_MODEL)
```

```python
import functools
import math

import numpy as np
import jax
import jax.numpy as jnp
from jax import lax
from jax.experimental import pallas as pl
from jax.experimental.pallas import tpu as pltpu

D_MODEL = 2048
BATCH = 4
SEQ = 4096
DEPTH = 2
GRID_W = 64
CTX_LEN = 256
HEAD_DIM = 128
GROUP_W = D_MODEL // 4
N_HEADS = GROUP_W // HEAD_DIM
D_FF = 4 * D_MODEL
ROPE_THETA = 10000.0
NORM_EPS = 1e-6

MLA_Q_RANK = GROUP_W
MLA_KV_RANK = GROUP_W // 2
MLA_NOPE = 128
MLA_ROPE = 64
GQA_KV_HEADS = N_HEADS // 2
DIFF_HALF = HEAD_DIM // 2
NA_WIN_H = 8
NA_WIN_W = 16
NA_BAND_ROWS = NA_WIN_H + 1
NA_BAND = NA_BAND_ROWS * GRID_W
NA_QBLOCK = 128
NA_PATTERNS = 5

A_COLS = MLA_Q_RANK + MLA_KV_RANK + MLA_ROPE
B_COLS = (N_HEADS + 2 * GQA_KV_HEADS) * HEAD_DIM
C_COLS = 3 * N_HEADS * HEAD_DIM
D_COLS = 3 * N_HEADS * HEAD_DIM

OFF_AQ = 0
OFF_AKV = OFF_AQ + MLA_Q_RANK
OFF_B = OFF_AKV + MLA_KV_RANK
OFF_C = OFF_B + B_COLS
OFF_D = OFF_C + C_COLS
OFF_KPE = OFF_D + D_COLS
IN_COLS_PAD = OFF_KPE + 128

LANES = 128
VMEM_LIMIT = 56 * 1024 * 1024
NEG_BIG = -1e30
LOG2E = math.log2(math.e)
MAX_EXP2_SPAN = 96.0
ONES_ROWS = 16

F32 = jnp.float32
BF16 = jnp.bfloat16

G_A_QN, G_A_QR, G_A_KN, G_A_KR, G_B_Q, G_B_K, G_C_Q, G_C_K, G_D_Q, G_D_K, G_C_OUT = range(11)
GAIN_ROWS = 16


def _lane_iota(shape):
    return lax.broadcasted_iota(jnp.int32, shape, len(shape) - 1)


def _rms(t):
    return t * lax.rsqrt(jnp.mean(t * t, axis=-1, keepdims=True) + NORM_EPS)


def _rms64(t):
    lo = _lane_iota(t.shape) < 64
    sq = t * t
    s_lo = jnp.sum(jnp.where(lo, sq, 0.0), axis=-1, keepdims=True)
    s_hi = jnp.sum(jnp.where(lo, 0.0, sq), axis=-1, keepdims=True)
    ms = jnp.where(lo, s_lo, s_hi) * (1.0 / 64.0)
    return t * lax.rsqrt(ms + NORM_EPS)


def _rope(t, cos, sin_signed, half):
    first = (_lane_iota(t.shape) % (2 * half)) < half
    partner = jnp.where(first, pltpu.roll(t, LANES - half, 1), pltpu.roll(t, half, 1))
    return t * cos + partner * sin_signed


def _dot(a, b):
    return jnp.dot(a, b, preferred_element_type=F32)


def _dot_nt(a, b):
    return lax.dot_general(a, b, (((1,), (1,)), ((), ())), preferred_element_type=F32)


def _mod_kernel(c_ref, w_ref, b_ref, o_ref):
    c = c_ref[...]
    a = (c * jax.nn.sigmoid(c)).astype(BF16)
    o_ref[0] = _dot(a, w_ref[0].astype(BF16)) + b_ref[0]


def _modulation(cc, w_mod, b_mod):
    tn = 1536
    n = 6 * D_MODEL
    return pl.pallas_call(
        _mod_kernel,
        grid=(DEPTH, n // tn),
        in_specs=[
            pl.BlockSpec((8, D_MODEL), lambda l, j: (0, 0)),
            pl.BlockSpec((1, D_MODEL, tn), lambda l, j: (l, 0, j)),
            pl.BlockSpec((1, 1, tn), lambda l, j: (l, 0, j)),
        ],
        out_specs=pl.BlockSpec((1, 8, tn), lambda l, j: (l, 0, j)),
        out_shape=jax.ShapeDtypeStruct((DEPTH, 8, n), F32),
        compiler_params=pltpu.CompilerParams(
            dimension_semantics=("arbitrary", "arbitrary"), vmem_limit_bytes=VMEM_LIMIT),
        name="modulation",
    )(cc, w_mod, b_mod.reshape(DEPTH, 1, n))


def _front_kernel(x_ref, gmix_ref, sc_ref, sh_ref, w_in_ref, w_uq_ref, w_ukv_ref,
                  gqa_ref, gkva_ref, gains_ref, cos128_ref, sin128_ref, cos64_ref, sin64_ref,
                  qa_ref, ka_ref, va_ref, qb_ref, kb_ref, vb_ref,
                  qc_ref, kc_ref, vc_ref, qd_ref, kd_ref, vd_ref):
    x = x_ref[...]
    gm = gmix_ref[...] * (1.0 + sc_ref[0])
    h = (_rms(x) * gm + sh_ref[0]).astype(BF16)

    gains = gains_ref[...]

    def gain(row):
        return gains[row:row + 1, :]

    cos128 = cos128_ref[...]
    sin128 = sin128_ref[...]
    cos64 = cos64_ref[...]
    sin64 = sin64_ref[...]
    lo = _lane_iota((x.shape[0], LANES)) < 64

    def proj(off, width):
        return _dot(h, w_in_ref[:, off:off + width])

    def tile(t, j):
        return t[:, j * LANES:(j + 1) * LANES]

    cq = (_rms(proj(OFF_AQ, MLA_Q_RANK)) * gqa_ref[...]).astype(BF16)
    ckv = (_rms(proj(OFF_AKV, MLA_KV_RANK)) * gkva_ref[...]).astype(BF16)
    kr = _rope(_rms64(proj(OFF_KPE, LANES)) * gain(G_A_KR), cos64, sin64, 16).astype(BF16)
    q = _dot(cq, w_uq_ref[...])
    kv = _dot(ckv, w_ukv_ref[...])
    qr_pairs = [
        _rope(_rms64(tile(q, N_HEADS + j)) * gain(G_A_QR), cos64, sin64, 16)
        for j in range(N_HEADS // 2)
    ]
    for hd in range(N_HEADS):
        qa_ref[hd, :, 0:LANES] = (_rms(tile(q, hd)) * gain(G_A_QN)).astype(BF16)
        pair = qr_pairs[hd // 2]
        keep = lo if hd % 2 == 0 else jnp.logical_not(lo)
        qa_ref[hd, :, LANES:2 * LANES] = jnp.where(keep, pair, 0.0).astype(BF16)
        ka_ref[hd, :, 0:LANES] = (_rms(tile(kv, hd)) * gain(G_A_KN)).astype(BF16)
        ka_ref[hd, :, LANES:2 * LANES] = kr
        va_ref[hd] = tile(kv, N_HEADS + hd).T.astype(BF16)

    pb = proj(OFF_B, B_COLS)
    for hd in range(N_HEADS):
        qb_ref[hd] = _rope(_rms(tile(pb, hd)) * gain(G_B_Q), cos128, sin128, 32).astype(BF16)
    for hd in range(GQA_KV_HEADS):
        kb_ref[hd] = _rope(_rms(tile(pb, N_HEADS + hd)) * gain(G_B_K),
                           cos128, sin128, 32).astype(BF16)
        vb_ref[hd] = tile(pb, N_HEADS + GQA_KV_HEADS + hd).T.astype(BF16)

    pc = proj(OFF_C, C_COLS)
    for hd in range(N_HEADS):
        qh = _rope(_rms64(tile(pc, hd)) * gain(G_C_Q), cos64, sin64, 16)
        qc_ref[2 * hd] = jnp.where(lo, qh, 0.0).astype(BF16)
        qc_ref[2 * hd + 1] = jnp.where(lo, 0.0, qh).astype(BF16)
        kc_ref[hd] = _rope(_rms64(tile(pc, N_HEADS + hd)) * gain(G_C_K),
                           cos64, sin64, 16).astype(BF16)
        vc_ref[hd] = tile(pc, 2 * N_HEADS + hd).T.astype(BF16)

    pd = proj(OFF_D, D_COLS)
    for hd in range(N_HEADS):
        qd_ref[hd] = (_rms(tile(pd, hd)) * gain(G_D_Q)).astype(BF16)
        kd_ref[hd] = (_rms(tile(pd, N_HEADS + hd)) * gain(G_D_K)).astype(BF16)
        vd_ref[hd] = tile(pd, 2 * N_HEADS + hd).astype(BF16)


_FRONT_OUT = (
    (N_HEADS, 2 * LANES, False), (N_HEADS, 2 * LANES, False), (N_HEADS, LANES, True),
    (N_HEADS, LANES, False), (GQA_KV_HEADS, LANES, False), (GQA_KV_HEADS, LANES, True),
    (2 * N_HEADS, LANES, False), (N_HEADS, LANES, False), (N_HEADS, LANES, True),
    (N_HEADS, LANES, False), (N_HEADS, LANES, False), (N_HEADS, LANES, False),
)


def _front(x2d, sc, sh, lw, tabs, *, rows_per_mod, tm=256):
    t = x2d.shape[0]
    n_pos_tiles = tabs[0].shape[0] // tm
    tiles_per_mod = rows_per_mod // tm

    def const(shape):
        return pl.BlockSpec(shape, lambda i: (0,) * len(shape), pipeline_mode=pl.Buffered(1))

    mod_spec = pl.BlockSpec((1, 1, D_MODEL), lambda i: (i // tiles_per_mod, 0, 0))
    tab_spec = pl.BlockSpec((tm, LANES), lambda i: (i % n_pos_tiles, 0))
    in_specs = [
        pl.BlockSpec((tm, D_MODEL), lambda i: (i, 0)),
        const((1, D_MODEL)), mod_spec, mod_spec,
        const((D_MODEL, IN_COLS_PAD)),
        const((MLA_Q_RANK, N_HEADS * (MLA_NOPE + MLA_ROPE))),
        const((MLA_KV_RANK, 2 * N_HEADS * LANES)),
        const((1, MLA_Q_RANK)), const((1, MLA_KV_RANK)), const((GAIN_ROWS, LANES)),
        tab_spec, tab_spec, tab_spec, tab_spec,
    ]
    out_specs = [pl.BlockSpec((nh, w, tm), lambda i: (0, 0, i)) if tr
                 else pl.BlockSpec((nh, tm, w), lambda i: (0, i, 0)) for nh, w, tr in _FRONT_OUT]
    out_shape = [jax.ShapeDtypeStruct((nh, w, t) if tr else (nh, t, w), BF16)
                 for nh, w, tr in _FRONT_OUT]
    return pl.pallas_call(
        _front_kernel,
        grid=(t // tm,),
        in_specs=in_specs,
        out_specs=out_specs,
        out_shape=out_shape,
        compiler_params=pltpu.CompilerParams(
            dimension_semantics=("arbitrary",), vmem_limit_bytes=VMEM_LIMIT),
        name="front",
    )(x2d, lw["g_mix"], sc, sh, lw["w_in"], lw["w_uq"], lw["w_ukv"],
      lw["g_qa"], lw["g_kva"], lw["gains"], *tabs)


def _flash_t(q, k_ref, vt_ref, kc_ref, vct_ref, tk, shift=None):
    cols = q.shape[0]
    dv = vct_ref.shape[0]
    chunks = []
    if k_ref is not None:
        chunks = [(k_ref[c * tk:(c + 1) * tk, :], vt_ref[:, c * tk:(c + 1) * tk])
                  for c in range(k_ref.shape[0] // tk)]
    chunks.append((kc_ref[...], vct_ref[...]))
    if shift is None:
        m = jnp.full((1, cols), NEG_BIG, F32)
        l = jnp.zeros((1, cols), F32)
        acc = jnp.zeros((dv, cols), F32)
        for k, vt in chunks:
            st = _dot_nt(k, q)
            m_new = jnp.maximum(m, jnp.max(st, axis=0, keepdims=True))
            alpha = jnp.exp2(m - m_new)
            p = jnp.exp2(st - m_new)
            l = alpha * l + jnp.sum(p, axis=0, keepdims=True)
            acc = alpha * acc + _dot(vt, p.astype(BF16))
            m = m_new
        return acc / l
    acc = jnp.zeros((dv + ONES_ROWS, cols), F32)
    for k, vt in chunks:
        p = jnp.exp2(_dot_nt(k, q) - shift).astype(BF16)
        vt_aug = jnp.concatenate([vt, jnp.ones((ONES_ROWS, vt.shape[1]), BF16)], axis=0)
        acc = acc + _dot(vt_aug, p)
    return acc[:dv] / acc[dv:dv + 1]


def _diff_lambda(lam_ref, lam_init):
    lam1 = jnp.exp(jnp.sum(lam_ref[0:1, :] * lam_ref[1:2, :], axis=-1, keepdims=True))
    lam2 = jnp.exp(jnp.sum(lam_ref[2:3, :] * lam_ref[3:4, :], axis=-1, keepdims=True))
    return lam1 - lam2 + lam_init


def _diff_combine(o, tq, lam, g_out, lam_init):
    d = o[:tq] - lam * o[tq:]
    return _rms(d) * (g_out * (1.0 - lam_init))


def _attn_kernel(bound_ref, q_ref, k_ref, v_ref, kc_ref, vc_ref, *rest, stacked, diff,
                 lam_init, tk):
    if diff:
        lam_ref, gout_ref, o_ref = rest
    else:
        (o_ref,) = rest
    tq = q_ref.shape[1]

    def run(shift):
        if stacked:
            q = jnp.concatenate([q_ref[0], q_ref[1]], axis=0)
        else:
            q = q_ref[0]
        o = _flash_t(q, k_ref.at[0], v_ref.at[0], kc_ref.at[0], vc_ref.at[0], tk, shift).T
        if diff:
            lam = _diff_lambda(lam_ref, lam_init)
            o_ref[0] = _diff_combine(o, tq, lam, gout_ref[...], lam_init).astype(o_ref.dtype)
        elif stacked:
            o_ref[0] = o[:tq].astype(o_ref.dtype)
            o_ref[1] = o[tq:].astype(o_ref.dtype)
        else:
            o_ref[0] = o.astype(o_ref.dtype)

    no_underflow = bound_ref[1] <= MAX_EXP2_SPAN

    @pl.when(no_underflow)
    def _():
        run(bound_ref[0])

    @pl.when(jnp.logical_not(no_underflow))
    def _():
        run(None)


def _attention(bound, q, k, v, kc, vc, *, kind, extra=(), lam_init=0.0, tq=512, tk=1024):
    nq = SEQ // tq
    dk = q.shape[-1]
    stacked = kind in ("gqa", "diff")
    n_outer = GQA_KV_HEADS if kind == "gqa" else N_HEADS
    q_heads = 2 if stacked else 1
    o_heads = 2 if kind == "gqa" else 1
    in_specs = [
        pl.BlockSpec(memory_space=pltpu.SMEM),
        pl.BlockSpec((q_heads, tq, dk), lambda b, h, i: (h, b * nq + i, 0)),
        pl.BlockSpec((1, SEQ, dk), lambda b, h, i: (h, b, 0)),
        pl.BlockSpec((1, LANES, SEQ), lambda b, h, i: (h, 0, b)),
        pl.BlockSpec((1, CTX_LEN, dk), lambda b, h, i: (h, b, 0)),
        pl.BlockSpec((1, LANES, CTX_LEN), lambda b, h, i: (h, 0, b)),
    ]
    for e in extra:
        in_specs.append(pl.BlockSpec(e.shape, lambda b, h, i, nd=e.ndim: (0,) * nd))
    return pl.pallas_call(
        functools.partial(_attn_kernel, stacked=stacked, diff=(kind == "diff"),
                          lam_init=lam_init, tk=tk),
        grid=(BATCH, n_outer, nq),
        in_specs=in_specs,
        out_specs=pl.BlockSpec((o_heads, tq, LANES), lambda b, h, i: (h, b * nq + i, 0)),
        out_shape=jax.ShapeDtypeStruct((N_HEADS, BATCH * SEQ, LANES), BF16),
        compiler_params=pltpu.CompilerParams(
            dimension_semantics=("arbitrary", "arbitrary", "arbitrary"),
            vmem_limit_bytes=VMEM_LIMIT),
        name="attn_" + kind,
    )(bound, q, k, v, kc, vc, *extra)


def _na_pattern_of(i):
    n_blocks = SEQ // NA_QBLOCK
    return jnp.where(i == 0, 0, jnp.where(i == 1, 1, jnp.where(
        i == n_blocks - 2, 3, jnp.where(i == n_blocks - 1, 4, 2))))


def _na_geometry(i):
    rows = SEQ // GRID_W
    rows_per_block = NA_QBLOCK // GRID_W
    r0 = i * rows_per_block
    band_start = min(min(max(r0 - NA_WIN_H // 2, 0), rows - NA_WIN_H), rows - NA_BAND_ROWS)
    rel = np.zeros((rows_per_block, NA_BAND_ROWS), np.int64)
    valid = np.zeros((rows_per_block, NA_BAND_ROWS), bool)
    for a in range(rows_per_block):
        q_row = r0 + a
        row_start = min(max(q_row - NA_WIN_H // 2, 0), rows - NA_WIN_H)
        for j in range(NA_BAND_ROWS):
            k_row = band_start + j
            valid[a, j] = row_start <= k_row < row_start + NA_WIN_H
            rel[a, j] = min(max(k_row - q_row + NA_WIN_H - 1, 0), 2 * NA_WIN_H - 2)
    return band_start, rel, valid


_NA_REPRESENTATIVE = (0, 1, 2, SEQ // NA_QBLOCK - 2, SEQ // NA_QBLOCK - 1)


def _na_bias_kernel(rpb_ref, o_ref):
    hd = pl.program_id(0)
    n_r = 2 * NA_WIN_H - 1
    n_c = 2 * NA_WIN_W - 1
    q_col = lax.broadcasted_iota(jnp.int32, (GRID_W, GRID_W), 0)
    k_col = lax.broadcasted_iota(jnp.int32, (GRID_W, GRID_W), 1)
    col_idx = jnp.clip(k_col - q_col + NA_WIN_W - 1, 0, n_c - 1)
    col_start = jnp.clip(q_col - NA_WIN_W // 2, 0, GRID_W - NA_WIN_W)
    col_ok = (k_col >= col_start) & (k_col < col_start + NA_WIN_W)
    tables = []
    for r in range(n_r):
        t = jnp.zeros((GRID_W, GRID_W), F32)
        for c in range(n_c):
            t = jnp.where(col_idx == c, rpb_ref[hd * (n_r * n_c) + r * n_c + c] * LOG2E, t)
        tables.append(jnp.where(col_ok, t, NEG_BIG))
    masked = jnp.full((GRID_W, GRID_W), NEG_BIG, F32)
    for p, i in enumerate(_NA_REPRESENTATIVE):
        _, rel, valid = _na_geometry(i)
        for a in range(rel.shape[0]):
            for j in range(rel.shape[1]):
                o_ref[0, p, a, j] = tables[int(rel[a, j])] if valid[a, j] else masked


def _na_bias(rpb):
    rows_per_block = NA_QBLOCK // GRID_W
    out = pl.pallas_call(
        _na_bias_kernel,
        grid=(N_HEADS,),
        in_specs=[pl.BlockSpec(memory_space=pltpu.SMEM)],
        out_specs=pl.BlockSpec((1, NA_PATTERNS, rows_per_block, NA_BAND_ROWS, GRID_W, GRID_W),
                               lambda h: (h, 0, 0, 0, 0, 0)),
        out_shape=jax.ShapeDtypeStruct(
            (N_HEADS, NA_PATTERNS, rows_per_block, NA_BAND_ROWS, GRID_W, GRID_W), F32),
        name="na_bias",
    )(rpb.reshape(-1))
    return out.transpose(0, 1, 2, 4, 3, 5).reshape(N_HEADS, NA_PATTERNS, NA_QBLOCK, NA_BAND)


def _na_kernel(bound_ref, q_ref, k_ref, v_ref, kc_ref, vc_ref, bias_ref, o_ref):
    rows = SEQ // GRID_W
    rows_per_block = NA_QBLOCK // GRID_W

    def run(shift):
        kc = kc_ref[0]
        vc = vc_ref[0]

        def body(i, carry):
            r0 = i * rows_per_block
            band_start = jnp.minimum(jnp.clip(r0 - NA_WIN_H // 2, 0, rows - NA_WIN_H),
                                     rows - NA_BAND_ROWS)
            k0 = pl.multiple_of(band_start * GRID_W, GRID_W)
            q0 = pl.multiple_of(i * NA_QBLOCK, NA_QBLOCK)
            q = q_ref[0, pl.ds(q0, NA_QBLOCK), :]
            kb = k_ref[0, pl.ds(k0, NA_BAND), :]
            vb = v_ref[0, pl.ds(k0, NA_BAND), :]
            s_band = _dot_nt(q, kb) + bias_ref[0, _na_pattern_of(i)]
            s_ctx = _dot_nt(q, kc)
            if shift is None:
                m = jnp.maximum(jnp.max(s_band, axis=-1, keepdims=True),
                                jnp.max(s_ctx, axis=-1, keepdims=True))
            else:
                m = shift
            p_band = jnp.exp2(s_band - m)
            p_ctx = jnp.exp2(s_ctx - m)
            l = (jnp.sum(p_band, axis=-1, keepdims=True)
                 + jnp.sum(p_ctx, axis=-1, keepdims=True))
            o = _dot(p_band.astype(BF16), vb) + _dot(p_ctx.astype(BF16), vc)
            o_ref[0, pl.ds(q0, NA_QBLOCK), :] = (o / l).astype(o_ref.dtype)
            return carry

        lax.fori_loop(0, SEQ // NA_QBLOCK, body, 0)

    no_underflow = bound_ref[1] <= MAX_EXP2_SPAN

    @pl.when(no_underflow)
    def _():
        run(bound_ref[0])

    @pl.when(jnp.logical_not(no_underflow))
    def _():
        run(None)


def _na_attention(bound, q, k, v, kc, vc, bias):
    return pl.pallas_call(
        _na_kernel,
        grid=(BATCH, N_HEADS),
        in_specs=[
            pl.BlockSpec(memory_space=pltpu.SMEM),
            pl.BlockSpec((1, SEQ, LANES), lambda b, h: (h, b, 0)),
            pl.BlockSpec((1, SEQ, LANES), lambda b, h: (h, b, 0)),
            pl.BlockSpec((1, SEQ, LANES), lambda b, h: (h, b, 0)),
            pl.BlockSpec((1, CTX_LEN, LANES), lambda b, h: (h, b, 0)),
            pl.BlockSpec((1, CTX_LEN, LANES), lambda b, h: (h, b, 0)),
            pl.BlockSpec((1, NA_PATTERNS, NA_QBLOCK, NA_BAND), lambda b, h: (h, 0, 0, 0)),
        ],
        out_specs=pl.BlockSpec((1, SEQ, LANES), lambda b, h: (h, b, 0)),
        out_shape=jax.ShapeDtypeStruct((N_HEADS, BATCH * SEQ, LANES), BF16),
        compiler_params=pltpu.CompilerParams(
            dimension_semantics=("arbitrary", "arbitrary"), vmem_limit_bytes=VMEM_LIMIT),
        name="attn_na",
    )(bound, q, k, v, kc, vc, bias)


def _ctx_attn_kernel(qa_ref, ka_ref, va_ref, qb_ref, kb_ref, vb_ref, qc_ref, kc_ref, vc_ref,
                     qd_ref, kd_ref, vd_ref, lam_ref, gout_ref,
                     oa_ref, ob_ref, oc_ref, od_ref, *, lam_init):
    n = CTX_LEN
    lam = _diff_lambda(lam_ref, lam_init)

    def attend(q, k_ref, vt_ref):
        return _flash_t(q, None, None, k_ref, vt_ref, n).T

    for hd in range(N_HEADS):
        oa_ref[hd] = attend(qa_ref[hd], ka_ref.at[hd], va_ref.at[hd]).astype(BF16)
        kvh = hd // (N_HEADS // GQA_KV_HEADS)
        ob_ref[hd] = attend(qb_ref[hd], kb_ref.at[kvh], vb_ref.at[kvh]).astype(BF16)
        q2 = jnp.concatenate([qc_ref[2 * hd], qc_ref[2 * hd + 1]], axis=0)
        o2 = attend(q2, kc_ref.at[hd], vc_ref.at[hd])
        oc_ref[hd] = _diff_combine(o2, n, lam, gout_ref[...], lam_init).astype(BF16)
        vdt = vd_ref[hd].astype(F32).T.astype(BF16)
        od_ref[hd] = attend(qd_ref[hd], kd_ref.at[hd], vdt).astype(BF16)


def _ctx_attention(fc, lam_rows, g_out, lam_init):
    in_specs = [pl.BlockSpec((nh, w, CTX_LEN), lambda b: (0, 0, b)) if tr
                else pl.BlockSpec((nh, CTX_LEN, w), lambda b: (0, b, 0))
                for nh, w, tr in _FRONT_OUT]
    in_specs += [pl.BlockSpec(lam_rows.shape, lambda b: (0, 0)),
                 pl.BlockSpec(g_out.shape, lambda b: (0, 0))]
    o_spec = pl.BlockSpec((N_HEADS, CTX_LEN, LANES), lambda b: (0, b, 0))
    o_shape = jax.ShapeDtypeStruct((N_HEADS, BATCH * CTX_LEN, LANES), BF16)
    return pl.pallas_call(
        functools.partial(_ctx_attn_kernel, lam_init=lam_init),
        grid=(BATCH,),
        in_specs=in_specs,
        out_specs=[o_spec] * 4,
        out_shape=[o_shape] * 4,
        compiler_params=pltpu.CompilerParams(
            dimension_semantics=("arbitrary",), vmem_limit_bytes=VMEM_LIMIT),
        name="attn_ctx",
    )(*fc, lam_rows, g_out)


def _out_kernel(x_ref, oa_ref, ob_ref, oc_ref, od_ref, w_ref, gt_ref, g_ref, sc_ref, sh_ref,
                xo_ref, h_ref):
    parts = [r[hd] for r in (oa_ref, ob_ref, oc_ref, od_ref) for hd in range(N_HEADS)]
    o = jnp.concatenate(parts, axis=-1)
    x = x_ref[...] + gt_ref[0] * _dot(o, w_ref[...])
    xo_ref[...] = x
    h_ref[...] = (_rms(x) * (g_ref[...] * (1.0 + sc_ref[0])) + sh_ref[0]).astype(BF16)


def _out_proj(x2d, outs, w_out, gt, g_mlp, sc, sh, *, rows_per_mod, tm=512):
    t = x2d.shape[0]
    tiles_per_mod = rows_per_mod // tm
    mod_spec = pl.BlockSpec((1, 1, D_MODEL), lambda i: (i // tiles_per_mod, 0, 0))
    o_spec = pl.BlockSpec((N_HEADS, tm, LANES), lambda i: (0, i, 0))
    row_spec = pl.BlockSpec((tm, D_MODEL), lambda i: (i, 0))
    return pl.pallas_call(
        _out_kernel,
        grid=(t // tm,),
        in_specs=[row_spec, o_spec, o_spec, o_spec, o_spec,
                  pl.BlockSpec((D_MODEL, D_MODEL), lambda i: (0, 0), pipeline_mode=pl.Buffered(1)),
                  mod_spec, pl.BlockSpec((1, D_MODEL), lambda i: (0, 0)), mod_spec, mod_spec],
        out_specs=[row_spec, row_spec],
        out_shape=[jax.ShapeDtypeStruct((t, D_MODEL), F32),
                   jax.ShapeDtypeStruct((t, D_MODEL), BF16)],
        compiler_params=pltpu.CompilerParams(
            dimension_semantics=("arbitrary",), vmem_limit_bytes=VMEM_LIMIT),
        name="out_proj",
    )(x2d, *outs, w_out, gt, g_mlp, sc, sh)


def _mlp_kernel(h_ref, wu_ref, wd_ref, x_ref, gt_ref, o_ref, xs_ref):
    j = pl.program_id(1)
    slab = x_ref.shape[0]

    @pl.when(j == 0)
    def _():
        o_ref[...] = jnp.zeros_like(o_ref)

    xs_ref[pl.ds(pl.multiple_of(j * slab, slab), slab), :] = x_ref[...]
    u = jnp.maximum(_dot(h_ref[...], wu_ref[...]), 0.0)
    o_ref[...] += _dot((u * u).astype(BF16), wd_ref[...])

    @pl.when(j == pl.num_programs(1) - 1)
    def _():
        o_ref[...] = xs_ref[...] + gt_ref[0] * o_ref[...]


def _mlp(h2d, x2d, w_up, w_down, gt, *, rows_per_mod, tm=1024, tf=512):
    t = x2d.shape[0]
    tiles_per_mod = rows_per_mod // tm
    nj = D_FF // tf
    slab = tm // nj
    return pl.pallas_call(
        _mlp_kernel,
        grid=(t // tm, nj),
        in_specs=[
            pl.BlockSpec((tm, D_MODEL), lambda i, j: (i, 0)),
            pl.BlockSpec((D_MODEL, tf), lambda i, j: (0, j)),
            pl.BlockSpec((tf, D_MODEL), lambda i, j: (j, 0)),
            pl.BlockSpec((slab, D_MODEL), lambda i, j: (i * nj + j, 0)),
            pl.BlockSpec((1, 1, D_MODEL), lambda i, j: (i // tiles_per_mod, 0, 0)),
        ],
        out_specs=pl.BlockSpec((tm, D_MODEL), lambda i, j: (i, 0)),
        out_shape=jax.ShapeDtypeStruct((t, D_MODEL), F32),
        scratch_shapes=[pltpu.VMEM((tm, D_MODEL), F32)],
        compiler_params=pltpu.CompilerParams(
            dimension_semantics=("arbitrary", "arbitrary"), vmem_limit_bytes=VMEM_LIMIT),
        name="mlp",
    )(h2d, w_up, w_down, x2d, gt)


def _rope_tables(tm):
    t = jnp.arange(SEQ)
    row = (t // GRID_W).astype(F32)
    col = (t % GRID_W).astype(F32)

    def tables(rot_dim):
        half = rot_dim // 2
        inv_freq = ROPE_THETA ** (-jnp.arange(0, half, 2, dtype=F32) / half)
        ar = row[:, None] * inv_freq
        ac = col[:, None] * inv_freq
        cos = jnp.concatenate([jnp.cos(ar), jnp.cos(ar), jnp.cos(ac), jnp.cos(ac)], axis=-1)
        sin = jnp.concatenate([-jnp.sin(ar), jnp.sin(ar), -jnp.sin(ac), jnp.sin(ac)], axis=-1)
        return cos, sin

    cos128, sin128 = tables(HEAD_DIM)
    cos64, sin64 = tables(DIFF_HALF)
    lat = (cos128, sin128, jnp.tile(cos64, (1, 2)), jnp.tile(sin64, (1, 2)))
    ones = jnp.ones((tm, LANES), F32)
    zeros = jnp.zeros((tm, LANES), F32)
    return lat, (ones, zeros, ones, zeros)


def _layer_weights(l, w_in, mla_w_uq, mla_w_ukv, g_norm_mix, mla_g_qa, mla_g_kva, mla_g_q,
                   mla_g_k, gqa_g_q, gqa_g_k, diff_g_q, diff_g_k, diff_g_out, na_g_q, na_g_k):
    wi = w_in[l]
    a_end = A_COLS
    kpe = wi[:, MLA_Q_RANK + MLA_KV_RANK:a_end]
    w_in_r = jnp.concatenate(
        [wi[:, :MLA_Q_RANK + MLA_KV_RANK], wi[:, a_end:], kpe, kpe], axis=1).astype(BF16)
    uq = mla_w_uq[l].reshape(MLA_Q_RANK, N_HEADS, MLA_NOPE + MLA_ROPE)
    w_uq_r = jnp.concatenate(
        [uq[:, :, :MLA_NOPE].reshape(MLA_Q_RANK, -1), uq[:, :, MLA_NOPE:].reshape(MLA_Q_RANK, -1)],
        axis=1).astype(BF16)
    ukv = mla_w_ukv[l].reshape(MLA_KV_RANK, N_HEADS, MLA_NOPE + HEAD_DIM)
    w_ukv_r = jnp.concatenate(
        [ukv[:, :, :MLA_NOPE].reshape(MLA_KV_RANK, -1), ukv[:, :, MLA_NOPE:].reshape(MLA_KV_RANK, -1)],
        axis=1).astype(BF16)

    scale_a = (MLA_NOPE + MLA_ROPE) ** -0.5 * LOG2E
    scale_b = HEAD_DIM ** -0.5 * LOG2E
    scale_c = DIFF_HALF ** -0.5 * LOG2E
    scale_d = HEAD_DIM ** -0.5 * LOG2E
    two = lambda g: jnp.tile(g, 2)
    rows = [None] * 11
    rows[G_A_QN] = mla_g_q[l][:MLA_NOPE] * scale_a
    rows[G_A_QR] = two(mla_g_q[l][MLA_NOPE:]) * scale_a
    rows[G_A_KN] = mla_g_k[l][:MLA_NOPE]
    rows[G_A_KR] = two(mla_g_k[l][MLA_NOPE:])
    rows[G_B_Q] = gqa_g_q[l] * scale_b
    rows[G_B_K] = gqa_g_k[l]
    rows[G_C_Q] = two(diff_g_q[l]) * scale_c
    rows[G_C_K] = two(diff_g_k[l])
    rows[G_D_Q] = na_g_q[l] * scale_d
    rows[G_D_K] = na_g_k[l]
    rows[G_C_OUT] = diff_g_out[l]
    gains = jnp.stack(rows + [jnp.zeros((LANES,), F32)] * (GAIN_ROWS - len(rows))).astype(F32)

    amax = lambda r: jnp.max(jnp.abs(rows[r]))
    slack = 1.02
    b_a = slack * (jnp.sqrt(MLA_NOPE * amax(G_A_QN) ** 2 + MLA_ROPE * amax(G_A_QR) ** 2)
                   * jnp.sqrt(MLA_NOPE * amax(G_A_KN) ** 2 + MLA_ROPE * amax(G_A_KR) ** 2))
    b_b = slack * HEAD_DIM * amax(G_B_Q) * amax(G_B_K)
    b_c = slack * DIFF_HALF * amax(G_C_Q) * amax(G_C_K)
    b_d = slack * HEAD_DIM * amax(G_D_Q) * amax(G_D_K)
    span = lambda b: jnp.stack([b, 2.0 * b]).astype(F32)
    return {
        "bound_a": span(b_a), "bound_b": span(b_b), "bound_c": span(b_c), "qk_bound_d": b_d,
        "w_in": w_in_r, "w_uq": w_uq_r, "w_ukv": w_ukv_r,
        "g_mix": g_norm_mix[l].reshape(1, D_MODEL),
        "g_qa": mla_g_qa[l].reshape(1, -1), "g_kva": mla_g_kva[l].reshape(1, -1),
        "gains": gains,
    }


def kernel(x, c, ctx, c_ctx, w_mod, b_mod, g_norm_mix, g_norm_mlp, w_in, mla_g_qa, mla_g_kva,
           mla_w_uq, mla_w_ukv, mla_g_q, mla_g_k, gqa_g_q, gqa_g_k, diff_g_q, diff_g_k,
           diff_lq1, diff_lk1, diff_lq2, diff_lk2, diff_g_out, na_g_q, na_g_k, na_rpb,
           w_out, w_up, w_down):
    tm_front = 256
    cc = jnp.concatenate([c, c_ctx[None, :], jnp.zeros((8 - BATCH - 1, D_MODEL), F32)], axis=0)
    mod = _modulation(cc, w_mod, b_mod)
    lat_tabs, ctx_tabs = _rope_tables(tm_front)

    xs = x.reshape(BATCH * SEQ, D_MODEL)
    cs = ctx.reshape(BATCH * CTX_LEN, D_MODEL)
    for l in range(DEPTH):
        need_ctx = l < DEPTH - 1
        lam_init = 0.8 - 0.6 * math.exp(-0.3 * l)
        m6 = mod[l].reshape(8, 6, 1, D_MODEL)
        sh_a, sc_a, gt_a, sh_m, sc_m, gt_m = [m6[:BATCH, k] for k in range(6)]
        csh_a, csc_a, cgt_a, csh_m, csc_m, cgt_m = [m6[BATCH:BATCH + 1, k] for k in range(6)]
        lw = _layer_weights(l, w_in, mla_w_uq, mla_w_ukv, g_norm_mix, mla_g_qa, mla_g_kva,
                            mla_g_q, mla_g_k, gqa_g_q, gqa_g_k, diff_g_q, diff_g_k, diff_g_out,
                            na_g_q, na_g_k)
        w_out_b = w_out[l].astype(BF16)
        w_up_b = w_up[l].astype(BF16)
        w_down_b = w_down[l].astype(BF16)
        g_mlp = g_norm_mlp[l].reshape(1, D_MODEL)
        lam_rows = jnp.stack([diff_lq1[l], diff_lk1[l], diff_lq2[l], diff_lk2[l]]).astype(F32)
        g_out = lw["gains"][G_C_OUT:G_C_OUT + 1]

        fl = _front(xs, sc_a, sh_a, lw, lat_tabs, rows_per_mod=SEQ, tm=tm_front)
        fc = _front(cs, csc_a, csh_a, lw, ctx_tabs, rows_per_mod=BATCH * CTX_LEN, tm=tm_front)
        qa, ka, va, qb, kb, vb, qc, kc, vc, qd, kd, vd = fl
        cqa, cka, cva, cqb, ckb, cvb, cqc, ckc, cvc, cqd, ckd, cvd = fc

        o_a = _attention(lw["bound_a"], qa, ka, va, cka, cva, kind="mla", tq=1024)
        o_b = _attention(lw["bound_b"], qb, kb, vb, ckb, cvb, kind="gqa", tq=512)
        o_c = _attention(lw["bound_c"], qc, kc, vc, ckc, cvc, kind="diff", tq=512,
                         extra=(lam_rows, g_out), lam_init=lam_init)
        bias_hi = jnp.maximum(jnp.max(na_rpb[l]), 0.0) * LOG2E
        bias_lo = jnp.minimum(jnp.min(na_rpb[l]), 0.0) * LOG2E
        bound_d = jnp.stack([lw["qk_bound_d"] + bias_hi,
                             2.0 * lw["qk_bound_d"] + bias_hi - bias_lo]).astype(F32)
        o_d = _na_attention(bound_d, qd, kd, vd, ckd, cvd, _na_bias(na_rpb[l]))

        x_mid, h_mlp = _out_proj(xs, (o_a, o_b, o_c, o_d), w_out_b, gt_a, g_mlp, sc_m, sh_m,
                                 rows_per_mod=SEQ)
        xs = _mlp(h_mlp, x_mid, w_up_b, w_down_b, gt_m, rows_per_mod=SEQ)

        if need_ctx:
            oc = _ctx_attention(fc, lam_rows, g_out, lam_init)
            c_mid, ch_mlp = _out_proj(cs, oc, w_out_b, cgt_a, g_mlp, csc_m, csh_m,
                                      rows_per_mod=BATCH * CTX_LEN)
            cs = _mlp(ch_mlp, c_mid, w_up_b, w_down_b, cgt_m, rows_per_mod=BATCH * CTX_LEN)
    return xs.reshape(BATCH, SEQ, D_MODEL)
```

```python
import functools
import math

import numpy as np
import jax
import jax.numpy as jnp
from jax import lax
from jax.experimental import pallas as pl
from jax.experimental.pallas import tpu as pltpu

D_MODEL = 2048
BATCH = 4
SEQ = 4096
DEPTH = 2
GRID_W = 64
CTX_LEN = 256
HEAD_DIM = 128
GROUP_W = D_MODEL // 4
N_HEADS = GROUP_W // HEAD_DIM
D_FF = 4 * D_MODEL
ROPE_THETA = 10000.0
NORM_EPS = 1e-6

MLA_Q_RANK = GROUP_W
MLA_KV_RANK = GROUP_W // 2
MLA_NOPE = 128
MLA_ROPE = 64
GQA_KV_HEADS = N_HEADS // 2
DIFF_HALF = HEAD_DIM // 2
NA_WIN_H = 8
NA_WIN_W = 16
NA_QROWS = 8
NA_QBLOCK = NA_QROWS * GRID_W
NA_BLOCKS = SEQ // NA_QBLOCK
NA_BAND_ROWS = NA_WIN_H + NA_QROWS
NA_BAND = NA_BAND_ROWS * GRID_W
NA_PATTERNS = 3

A_COLS = MLA_Q_RANK + MLA_KV_RANK + MLA_ROPE
B_COLS = (N_HEADS + 2 * GQA_KV_HEADS) * HEAD_DIM
C_COLS = 3 * N_HEADS * HEAD_DIM
D_COLS = 3 * N_HEADS * HEAD_DIM

LANES = 128
OFF_AQ = 0
OFF_AKV = OFF_AQ + MLA_Q_RANK
OFF_KPE = OFF_AKV + MLA_KV_RANK
WA_COLS = OFF_KPE + LANES
OFF_B = 0
OFF_C = OFF_B + B_COLS
OFF_D = OFF_C + C_COLS
WBCD_COLS = OFF_D + D_COLS

VMEM_LIMIT = 56 * 1024 * 1024
NEG_BIG = -1e30
LOG2E = math.log2(math.e)
MAX_EXP2_SPAN = 96.0
ONES_ROWS = 16

F32 = jnp.float32
BF16 = jnp.bfloat16

G_A_QN, G_A_QR, G_A_KN, G_A_KR, G_B_Q, G_B_K, G_C_Q, G_C_K, G_D_Q, G_D_K, G_C_OUT = range(11)
GAIN_ROWS = 16


def _lane_iota(shape):
    return lax.broadcasted_iota(jnp.int32, shape, len(shape) - 1)


def _rms(t):
    return t * lax.rsqrt(jnp.mean(t * t, axis=-1, keepdims=True) + NORM_EPS)


def _rms64(t):
    lo = _lane_iota(t.shape) < 64
    sq = t * t
    s_lo = jnp.sum(jnp.where(lo, sq, 0.0), axis=-1, keepdims=True)
    s_hi = jnp.sum(jnp.where(lo, 0.0, sq), axis=-1, keepdims=True)
    ms = jnp.where(lo, s_lo, s_hi) * (1.0 / 64.0)
    return t * lax.rsqrt(ms + NORM_EPS)


def _rope(t, cos, sin_signed, half):
    first = (_lane_iota(t.shape) % (2 * half)) < half
    partner = jnp.where(first, pltpu.roll(t, LANES - half, 1), pltpu.roll(t, half, 1))
    return t * cos + partner * sin_signed


def _dot(a, b):
    return jnp.dot(a, b, preferred_element_type=F32)


def _dot_nt(a, b):
    return lax.dot_general(a, b, (((1,), (1,)), ((), ())), preferred_element_type=F32)


def _mod_kernel(c_ref, w_ref, b_ref, o_ref):
    c = c_ref[...]
    a = (c * jax.nn.sigmoid(c)).astype(BF16)
    o_ref[0] = _dot(a, w_ref[0].astype(BF16)) + b_ref[0]


def _modulation(cc, w_mod, b_mod):
    tn = 1536
    n = 6 * D_MODEL
    return pl.pallas_call(
        _mod_kernel,
        grid=(DEPTH, n // tn),
        in_specs=[
            pl.BlockSpec((8, D_MODEL), lambda l, j: (0, 0)),
            pl.BlockSpec((1, D_MODEL, tn), lambda l, j: (l, 0, j)),
            pl.BlockSpec((1, 1, tn), lambda l, j: (l, 0, j)),
        ],
        out_specs=pl.BlockSpec((1, 8, tn), lambda l, j: (l, 0, j)),
        out_shape=jax.ShapeDtypeStruct((DEPTH, 8, n), F32),
        compiler_params=pltpu.CompilerParams(
            dimension_semantics=("arbitrary", "arbitrary"), vmem_limit_bytes=VMEM_LIMIT),
        name="modulation",
    )(cc, w_mod, b_mod.reshape(DEPTH, 1, n))


def _front_kernel(x_ref, gmix_ref, sc_ref, sh_ref, wa_ref, wbcd_ref, w_uq_ref, w_ukv_ref,
                  gqa_ref, gkva_ref, gains_ref, cos128_ref, sin128_ref, cos64_ref, sin64_ref,
                  qa_ref, ka_ref, va_ref, qb_ref, kb_ref, vb_ref,
                  qc_ref, kc_ref, vc_ref, qd_ref, kd_ref, vd_ref):
    x = x_ref[...]
    gm = gmix_ref[0] * (1.0 + sc_ref[0])
    h = (_rms(x) * gm + sh_ref[0]).astype(BF16)

    gains = gains_ref[0]

    def gain(row):
        return gains[row:row + 1, :]

    cos128 = cos128_ref[...]
    sin128 = sin128_ref[...]
    cos64 = cos64_ref[...]
    sin64 = sin64_ref[...]
    lo = _lane_iota((x.shape[0], LANES)) < 64

    def proj(w_ref, off, width):
        return _dot(h, w_ref[0, :, off:off + width])

    def tile(t, j):
        return t[:, j * LANES:(j + 1) * LANES]

    def store_t(ref, hd, t):
        ref[hd] = t.T.astype(BF16)

    cq = (_rms(proj(wa_ref, OFF_AQ, MLA_Q_RANK)) * gqa_ref[0]).astype(BF16)
    ckv = (_rms(proj(wa_ref, OFF_AKV, MLA_KV_RANK)) * gkva_ref[0]).astype(BF16)
    kr = _rope(_rms64(proj(wa_ref, OFF_KPE, LANES)) * gain(G_A_KR), cos64, sin64, 16).astype(BF16)
    q = _dot(cq, w_uq_ref[0])
    kv = _dot(ckv, w_ukv_ref[0])
    qr_pairs = [
        _rope(_rms64(tile(q, N_HEADS + j)) * gain(G_A_QR), cos64, sin64, 16)
        for j in range(N_HEADS // 2)
    ]
    for hd in range(N_HEADS):
        qa_ref[hd, :, 0:LANES] = (_rms(tile(q, hd)) * gain(G_A_QN)).astype(BF16)
        pair = qr_pairs[hd // 2]
        keep = lo if hd % 2 == 0 else jnp.logical_not(lo)
        qa_ref[hd, :, LANES:2 * LANES] = jnp.where(keep, pair, 0.0).astype(BF16)
        ka_ref[hd, :, 0:LANES] = (_rms(tile(kv, hd)) * gain(G_A_KN)).astype(BF16)
        ka_ref[hd, :, LANES:2 * LANES] = kr
        store_t(va_ref, hd, tile(kv, N_HEADS + hd))

    pb = proj(wbcd_ref, OFF_B, B_COLS)
    for hd in range(N_HEADS):
        qb_ref[hd] = _rope(_rms(tile(pb, hd)) * gain(G_B_Q), cos128, sin128, 32).astype(BF16)
    for hd in range(GQA_KV_HEADS):
        kb_ref[hd] = _rope(_rms(tile(pb, N_HEADS + hd)) * gain(G_B_K),
                           cos128, sin128, 32).astype(BF16)
        store_t(vb_ref, hd, tile(pb, N_HEADS + GQA_KV_HEADS + hd))

    pc = proj(wbcd_ref, OFF_C, C_COLS)
    for hd in range(N_HEADS):
        qh = _rope(_rms64(tile(pc, hd)) * gain(G_C_Q), cos64, sin64, 16)
        qc_ref[2 * hd] = jnp.where(lo, qh, 0.0).astype(BF16)
        qc_ref[2 * hd + 1] = jnp.where(lo, 0.0, qh).astype(BF16)
        kc_ref[hd] = _rope(_rms64(tile(pc, N_HEADS + hd)) * gain(G_C_K),
                           cos64, sin64, 16).astype(BF16)
        store_t(vc_ref, hd, tile(pc, 2 * N_HEADS + hd))

    pd = proj(wbcd_ref, OFF_D, D_COLS)
    for hd in range(N_HEADS):
        qd_ref[hd] = (_rms(tile(pd, hd)) * gain(G_D_Q)).astype(BF16)
        kd_ref[hd] = (_rms(tile(pd, N_HEADS + hd)) * gain(G_D_K)).astype(BF16)
        store_t(vd_ref, hd, tile(pd, 2 * N_HEADS + hd))


_FRONT_OUT = (
    (N_HEADS, 2 * LANES, False), (N_HEADS, 2 * LANES, False), (N_HEADS, LANES, True),
    (N_HEADS, LANES, False), (GQA_KV_HEADS, LANES, False), (GQA_KV_HEADS, LANES, True),
    (2 * N_HEADS, LANES, False), (N_HEADS, LANES, False), (N_HEADS, LANES, True),
    (N_HEADS, LANES, False), (N_HEADS, LANES, False), (N_HEADS, LANES, True),
)


def _front(x2d, sc, sh, pw, layer, tabs, *, rows_per_mod, tm=256):
    t = x2d.shape[0]
    n_pos_tiles = tabs[0].shape[0] // tm
    tiles_per_mod = rows_per_mod // tm

    def stacked(a):
        return pl.BlockSpec((1,) + a.shape[1:], lambda i: (layer,) + (0,) * (a.ndim - 1),
                            pipeline_mode=pl.Buffered(1))

    mod_spec = pl.BlockSpec((1, 1, D_MODEL), lambda i: (i // tiles_per_mod, 0, 0))
    tab_spec = pl.BlockSpec((tm, LANES), lambda i: (i % n_pos_tiles, 0))
    params = [pw["g_mix"], None, None, pw["w_a"], pw["w_bcd"], pw["w_uq"], pw["w_ukv"],
              pw["g_qa"], pw["g_kva"], pw["gains"]]
    in_specs = [pl.BlockSpec((tm, D_MODEL), lambda i: (i, 0))]
    in_specs += [mod_spec if p is None else stacked(p) for p in params]
    in_specs += [tab_spec] * 4
    params[1], params[2] = sc, sh
    out_specs = [pl.BlockSpec((nh, w, tm), lambda i: (0, 0, i)) if tr
                 else pl.BlockSpec((nh, tm, w), lambda i: (0, i, 0)) for nh, w, tr in _FRONT_OUT]
    out_shape = [jax.ShapeDtypeStruct((nh, w, t) if tr else (nh, t, w), BF16)
                 for nh, w, tr in _FRONT_OUT]
    return pl.pallas_call(
        _front_kernel,
        grid=(t // tm,),
        in_specs=in_specs,
        out_specs=out_specs,
        out_shape=out_shape,
        compiler_params=pltpu.CompilerParams(
            dimension_semantics=("arbitrary",), vmem_limit_bytes=VMEM_LIMIT),
        name="front",
    )(x2d, *params, *tabs)


def _attend_t(q, chunks, shift=None):
    cols = q.shape[0]
    dv = chunks[0][1].shape[0]

    def scores(k, bias):
        st = _dot_nt(k, q)
        return st if bias is None else st + bias

    if shift is None:
        m = jnp.full((1, cols), NEG_BIG, F32)
        l = jnp.zeros((1, cols), F32)
        acc = jnp.zeros((dv, cols), F32)
        for k, vt, bias in chunks:
            st = scores(k, bias)
            m_new = jnp.maximum(m, jnp.max(st, axis=0, keepdims=True))
            alpha = jnp.exp2(m - m_new)
            p = jnp.exp2(st - m_new)
            l = alpha * l + jnp.sum(p, axis=0, keepdims=True)
            acc = alpha * acc + _dot(vt, p.astype(BF16))
            m = m_new
        return acc / l
    acc = jnp.zeros((dv + ONES_ROWS, cols), F32)
    for k, vt, bias in chunks:
        p = jnp.exp2(scores(k, bias) - shift).astype(BF16)
        vt_aug = jnp.concatenate([vt, jnp.ones((ONES_ROWS, vt.shape[1]), BF16)], axis=0)
        acc = acc + _dot(vt_aug, p)
    return acc[:dv] / acc[dv:dv + 1]


def _by_underflow_guard(bound_ref, run):
    no_underflow = bound_ref[1] <= MAX_EXP2_SPAN

    @pl.when(no_underflow)
    def _():
        run(bound_ref[0])

    @pl.when(jnp.logical_not(no_underflow))
    def _():
        run(None)


def _diff_lambda(lam_ref, lam_init):
    lam1 = jnp.exp(jnp.sum(lam_ref[0:1, :] * lam_ref[1:2, :], axis=-1, keepdims=True))
    lam2 = jnp.exp(jnp.sum(lam_ref[2:3, :] * lam_ref[3:4, :], axis=-1, keepdims=True))
    return lam1 - lam2 + lam_init


def _diff_combine(o, tq, lam, g_out, lam_init):
    d = o[:tq] - lam * o[tq:]
    return _rms(d) * (g_out * (1.0 - lam_init))


def _attn_kernel(bound_ref, q_ref, k_ref, v_ref, kc_ref, vc_ref, *rest, stacked, diff,
                 lam_init, tk):
    if diff:
        lam_ref, gout_ref, o_ref = rest
    else:
        (o_ref,) = rest
    tq = q_ref.shape[1]

    def run(shift):
        if stacked:
            q = jnp.concatenate([q_ref[0], q_ref[1]], axis=0)
        else:
            q = q_ref[0]
        chunks = [(k_ref[0, c * tk:(c + 1) * tk, :], v_ref[0, :, c * tk:(c + 1) * tk], None)
                  for c in range(SEQ // tk)]
        chunks.append((kc_ref[0], vc_ref[0], None))
        o = _attend_t(q, chunks, shift).T
        if diff:
            lam = _diff_lambda(lam_ref, lam_init)
            o_ref[0] = _diff_combine(o, tq, lam, gout_ref[...], lam_init).astype(o_ref.dtype)
        elif stacked:
            o_ref[0] = o[:tq].astype(o_ref.dtype)
            o_ref[1] = o[tq:].astype(o_ref.dtype)
        else:
            o_ref[0] = o.astype(o_ref.dtype)

    _by_underflow_guard(bound_ref, run)


def _attention(bound, q, k, v, kc, vc, *, kind, extra=(), lam_init=0.0, tq=512, tk=1024):
    nq = SEQ // tq
    dk = q.shape[-1]
    stacked = kind in ("gqa", "diff")
    n_outer = GQA_KV_HEADS if kind == "gqa" else N_HEADS
    q_heads = 2 if stacked else 1
    o_heads = 2 if kind == "gqa" else 1
    in_specs = [
        pl.BlockSpec(memory_space=pltpu.SMEM),
        pl.BlockSpec((q_heads, tq, dk), lambda b, h, i: (h, b * nq + i, 0)),
        pl.BlockSpec((1, SEQ, dk), lambda b, h, i: (h, b, 0)),
        pl.BlockSpec((1, LANES, SEQ), lambda b, h, i: (h, 0, b)),
        pl.BlockSpec((1, CTX_LEN, dk), lambda b, h, i: (h, b, 0)),
        pl.BlockSpec((1, LANES, CTX_LEN), lambda b, h, i: (h, 0, b)),
    ]
    for e in extra:
        in_specs.append(pl.BlockSpec(e.shape, lambda b, h, i, nd=e.ndim: (0,) * nd))
    return pl.pallas_call(
        functools.partial(_attn_kernel, stacked=stacked, diff=(kind == "diff"),
                          lam_init=lam_init, tk=tk),
        grid=(BATCH, n_outer, nq),
        in_specs=in_specs,
        out_specs=pl.BlockSpec((o_heads, tq, LANES), lambda b, h, i: (h, b * nq + i, 0)),
        out_shape=jax.ShapeDtypeStruct((N_HEADS, BATCH * SEQ, LANES), BF16),
        compiler_params=pltpu.CompilerParams(
            dimension_semantics=("arbitrary", "arbitrary", "arbitrary"),
            vmem_limit_bytes=VMEM_LIMIT),
        name="attn_" + kind,
    )(bound, q, k, v, kc, vc, *extra)


def _na_geometry(i):
    rows = SEQ // GRID_W
    r0 = i * NA_QROWS
    band_start = min(max(r0 - NA_WIN_H // 2, 0), rows - NA_BAND_ROWS)
    rel = np.zeros((NA_QROWS, NA_BAND_ROWS), np.int64)
    valid = np.zeros((NA_QROWS, NA_BAND_ROWS), bool)
    for a in range(NA_QROWS):
        q_row = r0 + a
        row_start = min(max(q_row - NA_WIN_H // 2, 0), rows - NA_WIN_H)
        assert band_start <= row_start and row_start + NA_WIN_H <= band_start + NA_BAND_ROWS
        for j in range(NA_BAND_ROWS):
            k_row = band_start + j
            valid[a, j] = row_start <= k_row < row_start + NA_WIN_H
            rel[a, j] = min(max(k_row - q_row + NA_WIN_H - 1, 0), 2 * NA_WIN_H - 2)
    return band_start, rel, valid


def _na_pattern_of(i):
    return 0 if i == 0 else (NA_PATTERNS - 1 if i == NA_BLOCKS - 1 else 1)


_NA_REPRESENTATIVE = (0, 1, NA_BLOCKS - 1)
for _i in range(NA_BLOCKS):
    _g, _r = _na_geometry(_i), _na_geometry(_NA_REPRESENTATIVE[_na_pattern_of(_i)])
    assert (_g[1] == _r[1]).all() and (_g[2] == _r[2]).all()


def _na_bias_kernel(rpb_ref, o_ref):
    hd = pl.program_id(0)
    n_r = 2 * NA_WIN_H - 1
    n_c = 2 * NA_WIN_W - 1
    k_col = lax.broadcasted_iota(jnp.int32, (GRID_W, LANES), 0)
    lane = lax.broadcasted_iota(jnp.int32, (GRID_W, LANES), 1)
    q_col = lane % GRID_W
    left = lane < GRID_W
    col_idx = jnp.clip(k_col - q_col + NA_WIN_W - 1, 0, n_c - 1)
    col_start = jnp.clip(q_col - NA_WIN_W // 2, 0, GRID_W - NA_WIN_W)
    col_ok = (k_col >= col_start) & (k_col < col_start + NA_WIN_W)
    tables = []
    for r in range(n_r):
        t = jnp.zeros((GRID_W, LANES), F32)
        for c in range(n_c):
            t = jnp.where(col_idx == c, rpb_ref[hd * (n_r * n_c) + r * n_c + c] * LOG2E, t)
        tables.append(jnp.where(col_ok, t, NEG_BIG))
    masked = jnp.full((GRID_W, LANES), NEG_BIG, F32)
    for p, i in enumerate(_NA_REPRESENTATIVE):
        _, rel, valid = _na_geometry(i)
        for j in range(NA_BAND_ROWS):
            for a in range(0, NA_QROWS, 2):
                t0 = tables[int(rel[a, j])] if valid[a, j] else masked
                t1 = tables[int(rel[a + 1, j])] if valid[a + 1, j] else masked
                blk = t0 if t0 is t1 else jnp.where(left, t0, t1)
                o_ref[0, p, j * GRID_W:(j + 1) * GRID_W, a * GRID_W:(a + 2) * GRID_W] = blk


def _na_bias(rpb, layer):
    n = N_HEADS * (2 * NA_WIN_H - 1) * (2 * NA_WIN_W - 1)
    return pl.pallas_call(
        _na_bias_kernel,
        grid=(N_HEADS,),
        in_specs=[pl.BlockSpec(memory_space=pltpu.SMEM)],
        out_specs=pl.BlockSpec((1, NA_PATTERNS, NA_BAND, NA_QBLOCK), lambda h: (h, 0, 0, 0)),
        out_shape=jax.ShapeDtypeStruct((N_HEADS, NA_PATTERNS, NA_BAND, NA_QBLOCK), F32),
        compiler_params=pltpu.CompilerParams(
            dimension_semantics=("arbitrary",), vmem_limit_bytes=VMEM_LIMIT),
        name="na_bias",
    )(rpb.reshape(DEPTH, n)[layer])


def _na_kernel(bound_ref, q_ref, k_ref, vt_ref, kc_ref, vct_ref, bias_ref, o_ref):
    def run(shift):
        kc = kc_ref[0]
        vct = vct_ref[0]
        for i in range(NA_BLOCKS):
            k0 = _na_geometry(i)[0] * GRID_W
            rows = slice(i * NA_QBLOCK, (i + 1) * NA_QBLOCK)
            chunks = [(k_ref[0, k0:k0 + NA_BAND, :], vt_ref[0, :, k0:k0 + NA_BAND],
                       bias_ref[0, _na_pattern_of(i)]),
                      (kc, vct, None)]
            o_ref[0, rows, :] = _attend_t(q_ref[0, rows, :], chunks, shift).T.astype(o_ref.dtype)

    _by_underflow_guard(bound_ref, run)


def _na_attention(bound, q, k, vt, kc, vct, bias):
    return pl.pallas_call(
        _na_kernel,
        grid=(BATCH, N_HEADS),
        in_specs=[
            pl.BlockSpec(memory_space=pltpu.SMEM),
            pl.BlockSpec((1, SEQ, LANES), lambda b, h: (h, b, 0)),
            pl.BlockSpec((1, SEQ, LANES), lambda b, h: (h, b, 0)),
            pl.BlockSpec((1, LANES, SEQ), lambda b, h: (h, 0, b)),
            pl.BlockSpec((1, CTX_LEN, LANES), lambda b, h: (h, b, 0)),
            pl.BlockSpec((1, LANES, CTX_LEN), lambda b, h: (h, 0, b)),
            pl.BlockSpec((1, NA_PATTERNS, NA_BAND, NA_QBLOCK), lambda b, h: (h, 0, 0, 0)),
        ],
        out_specs=pl.BlockSpec((1, SEQ, LANES), lambda b, h: (h, b, 0)),
        out_shape=jax.ShapeDtypeStruct((N_HEADS, BATCH * SEQ, LANES), BF16),
        compiler_params=pltpu.CompilerParams(
            dimension_semantics=("arbitrary", "arbitrary"), vmem_limit_bytes=VMEM_LIMIT),
        name="attn_na",
    )(bound, q, k, vt, kc, vct, bias)


def _ctx_attn_kernel(qa_ref, ka_ref, va_ref, qb_ref, kb_ref, vb_ref, qc_ref, kc_ref, vc_ref,
                     qd_ref, kd_ref, vd_ref, lam_ref, gout_ref,
                     oa_ref, ob_ref, oc_ref, od_ref, *, lam_init):
    n = CTX_LEN
    lam = _diff_lambda(lam_ref, lam_init)

    def attend(q, k_ref, vt_ref, hd):
        return _attend_t(q, [(k_ref[hd], vt_ref[hd], None)]).T

    for hd in range(N_HEADS):
        oa_ref[hd] = attend(qa_ref[hd], ka_ref, va_ref, hd).astype(BF16)
        kvh = hd // (N_HEADS // GQA_KV_HEADS)
        ob_ref[hd] = attend(qb_ref[hd], kb_ref, vb_ref, kvh).astype(BF16)
        q2 = jnp.concatenate([qc_ref[2 * hd], qc_ref[2 * hd + 1]], axis=0)
        o2 = attend(q2, kc_ref, vc_ref, hd)
        oc_ref[hd] = _diff_combine(o2, n, lam, gout_ref[...], lam_init).astype(BF16)
        od_ref[hd] = attend(qd_ref[hd], kd_ref, vd_ref, hd).astype(BF16)


def _ctx_attention(fc, lam_rows, g_out, lam_init):
    in_specs = [pl.BlockSpec((nh, w, CTX_LEN), lambda b: (0, 0, b)) if tr
                else pl.BlockSpec((nh, CTX_LEN, w), lambda b: (0, b, 0))
                for nh, w, tr in _FRONT_OUT]
    in_specs += [pl.BlockSpec(lam_rows.shape, lambda b: (0, 0)),
                 pl.BlockSpec(g_out.shape, lambda b: (0, 0))]
    o_spec = pl.BlockSpec((N_HEADS, CTX_LEN, LANES), lambda b: (0, b, 0))
    o_shape = jax.ShapeDtypeStruct((N_HEADS, BATCH * CTX_LEN, LANES), BF16)
    return pl.pallas_call(
        functools.partial(_ctx_attn_kernel, lam_init=lam_init),
        grid=(BATCH,),
        in_specs=in_specs,
        out_specs=[o_spec] * 4,
        out_shape=[o_shape] * 4,
        compiler_params=pltpu.CompilerParams(
            dimension_semantics=("arbitrary",), vmem_limit_bytes=VMEM_LIMIT),
        name="attn_ctx",
    )(*fc, lam_rows, g_out)


def _out_kernel(x_ref, oa_ref, ob_ref, oc_ref, od_ref, w_ref, gt_ref, g_ref, sc_ref, sh_ref,
                xo_ref, h_ref):
    parts = [r[hd] for r in (oa_ref, ob_ref, oc_ref, od_ref) for hd in range(N_HEADS)]
    o = jnp.concatenate(parts, axis=-1)
    x = x_ref[...] + gt_ref[0] * _dot(o, w_ref[0])
    xo_ref[...] = x
    h_ref[...] = (_rms(x) * (g_ref[0] * (1.0 + sc_ref[0])) + sh_ref[0]).astype(BF16)


def _out_proj(x2d, outs, w_out, layer, gt, g_mlp, sc, sh, *, rows_per_mod, tm=512):
    t = x2d.shape[0]
    tiles_per_mod = rows_per_mod // tm
    mod_spec = pl.BlockSpec((1, 1, D_MODEL), lambda i: (i // tiles_per_mod, 0, 0))
    o_spec = pl.BlockSpec((N_HEADS, tm, LANES), lambda i: (0, i, 0))
    row_spec = pl.BlockSpec((tm, D_MODEL), lambda i: (i, 0))
    return pl.pallas_call(
        _out_kernel,
        grid=(t // tm,),
        in_specs=[row_spec, o_spec, o_spec, o_spec, o_spec,
                  pl.BlockSpec((1, D_MODEL, D_MODEL), lambda i: (layer, 0, 0),
                               pipeline_mode=pl.Buffered(1)),
                  mod_spec, pl.BlockSpec((1, 1, D_MODEL), lambda i: (layer, 0, 0)),
                  mod_spec, mod_spec],
        out_specs=[row_spec, row_spec],
        out_shape=[jax.ShapeDtypeStruct((t, D_MODEL), F32),
                   jax.ShapeDtypeStruct((t, D_MODEL), BF16)],
        compiler_params=pltpu.CompilerParams(
            dimension_semantics=("arbitrary",), vmem_limit_bytes=VMEM_LIMIT),
        name="out_proj",
    )(x2d, *outs, w_out, gt, g_mlp, sc, sh)


def _mlp_kernel(h_ref, wu_ref, wd_ref, x_ref, gt_ref, o_ref, xs_ref):
    j = pl.program_id(1)
    slab = x_ref.shape[0]

    @pl.when(j == 0)
    def _():
        o_ref[...] = jnp.zeros_like(o_ref)

    xs_ref[pl.ds(pl.multiple_of(j * slab, slab), slab), :] = x_ref[...]
    u = jnp.maximum(_dot(h_ref[...], wu_ref[0]), 0.0)
    o_ref[...] += _dot((u * u).astype(BF16), wd_ref[0])

    @pl.when(j == pl.num_programs(1) - 1)
    def _():
        o_ref[...] = xs_ref[...] + gt_ref[0] * o_ref[...]


def _mlp(h2d, x2d, w_up, w_down, layer, gt, *, rows_per_mod, tm=1024, tf=512):
    t = x2d.shape[0]
    tiles_per_mod = rows_per_mod // tm
    nj = D_FF // tf
    slab = tm // nj
    return pl.pallas_call(
        _mlp_kernel,
        grid=(t // tm, nj),
        in_specs=[
            pl.BlockSpec((tm, D_MODEL), lambda i, j: (i, 0)),
            pl.BlockSpec((1, D_MODEL, tf), lambda i, j: (layer, 0, j)),
            pl.BlockSpec((1, tf, D_MODEL), lambda i, j: (layer, j, 0)),
            pl.BlockSpec((slab, D_MODEL), lambda i, j: (i * nj + j, 0)),
            pl.BlockSpec((1, 1, D_MODEL), lambda i, j: (i // tiles_per_mod, 0, 0)),
        ],
        out_specs=pl.BlockSpec((tm, D_MODEL), lambda i, j: (i, 0)),
        out_shape=jax.ShapeDtypeStruct((t, D_MODEL), F32),
        scratch_shapes=[pltpu.VMEM((tm, D_MODEL), F32)],
        compiler_params=pltpu.CompilerParams(
            dimension_semantics=("arbitrary", "arbitrary"), vmem_limit_bytes=VMEM_LIMIT),
        name="mlp",
    )(h2d, w_up, w_down, x2d, gt)


def _rope_tables(tm):
    t = jnp.arange(SEQ)
    row = (t // GRID_W).astype(F32)
    col = (t % GRID_W).astype(F32)

    def tables(rot_dim):
        half = rot_dim // 2
        inv_freq = ROPE_THETA ** (-jnp.arange(0, half, 2, dtype=F32) / half)
        ar = row[:, None] * inv_freq
        ac = col[:, None] * inv_freq
        cos = jnp.concatenate([jnp.cos(ar), jnp.cos(ar), jnp.cos(ac), jnp.cos(ac)], axis=-1)
        sin = jnp.concatenate([-jnp.sin(ar), jnp.sin(ar), -jnp.sin(ac), jnp.sin(ac)], axis=-1)
        return cos, sin

    cos128, sin128 = tables(HEAD_DIM)
    cos64, sin64 = tables(DIFF_HALF)
    lat = (cos128, sin128, jnp.tile(cos64, (1, 2)), jnp.tile(sin64, (1, 2)))
    ones = jnp.ones((tm, LANES), F32)
    zeros = jnp.zeros((tm, LANES), F32)
    return lat, (ones, zeros, ones, zeros)


def _prepare_params(w_in, mla_w_uq, mla_w_ukv, g_norm_mix, mla_g_qa, mla_g_kva, mla_g_q, mla_g_k,
                    gqa_g_q, gqa_g_k, diff_g_q, diff_g_k, diff_g_out, na_g_q, na_g_k, na_rpb):
    kpe = w_in[:, :, OFF_KPE:A_COLS]
    w_a = jnp.concatenate([w_in[:, :, :OFF_KPE], kpe, kpe], axis=2).astype(BF16)
    w_bcd = w_in[:, :, A_COLS:].astype(BF16)
    uq = mla_w_uq.reshape(DEPTH, MLA_Q_RANK, N_HEADS, MLA_NOPE + MLA_ROPE)
    w_uq = jnp.concatenate(
        [uq[..., :MLA_NOPE].reshape(DEPTH, MLA_Q_RANK, -1),
         uq[..., MLA_NOPE:].reshape(DEPTH, MLA_Q_RANK, -1)], axis=2).astype(BF16)
    ukv = mla_w_ukv.reshape(DEPTH, MLA_KV_RANK, N_HEADS, MLA_NOPE + HEAD_DIM)
    w_ukv = jnp.concatenate(
        [ukv[..., :MLA_NOPE].reshape(DEPTH, MLA_KV_RANK, -1),
         ukv[..., MLA_NOPE:].reshape(DEPTH, MLA_KV_RANK, -1)], axis=2).astype(BF16)

    scale_a = (MLA_NOPE + MLA_ROPE) ** -0.5 * LOG2E
    scale_b = HEAD_DIM ** -0.5 * LOG2E
    scale_c = DIFF_HALF ** -0.5 * LOG2E
    scale_d = HEAD_DIM ** -0.5 * LOG2E
    two = lambda g: jnp.tile(g, (1, 2))
    rows = [None] * 11
    rows[G_A_QN] = mla_g_q[:, :MLA_NOPE] * scale_a
    rows[G_A_QR] = two(mla_g_q[:, MLA_NOPE:]) * scale_a
    rows[G_A_KN] = mla_g_k[:, :MLA_NOPE]
    rows[G_A_KR] = two(mla_g_k[:, MLA_NOPE:])
    rows[G_B_Q] = gqa_g_q * scale_b
    rows[G_B_K] = gqa_g_k
    rows[G_C_Q] = two(diff_g_q) * scale_c
    rows[G_C_K] = two(diff_g_k)
    rows[G_D_Q] = na_g_q * scale_d
    rows[G_D_K] = na_g_k
    rows[G_C_OUT] = diff_g_out
    pad = [jnp.zeros((DEPTH, LANES), F32)] * (GAIN_ROWS - len(rows))
    gains = jnp.stack(rows + pad, axis=1).astype(F32)

    amax = lambda r: jnp.max(jnp.abs(rows[r]), axis=1)
    slack = 1.02
    b_a = slack * (jnp.sqrt(MLA_NOPE * amax(G_A_QN) ** 2 + MLA_ROPE * amax(G_A_QR) ** 2)
                   * jnp.sqrt(MLA_NOPE * amax(G_A_KN) ** 2 + MLA_ROPE * amax(G_A_KR) ** 2))
    b_b = slack * HEAD_DIM * amax(G_B_Q) * amax(G_B_K)
    b_c = slack * DIFF_HALF * amax(G_C_Q) * amax(G_C_K)
    b_d = slack * HEAD_DIM * amax(G_D_Q) * amax(G_D_K)
    bias_hi = jnp.maximum(jnp.max(na_rpb, axis=(1, 2, 3)), 0.0) * LOG2E
    bias_lo = jnp.minimum(jnp.min(na_rpb, axis=(1, 2, 3)), 0.0) * LOG2E
    span = lambda b: jnp.stack([b, 2.0 * b], axis=1).astype(F32)
    return {
        "bound_a": span(b_a), "bound_b": span(b_b), "bound_c": span(b_c),
        "bound_d": jnp.stack([b_d + bias_hi, 2.0 * b_d + bias_hi - bias_lo], axis=1).astype(F32),
        "w_a": w_a, "w_bcd": w_bcd, "w_uq": w_uq, "w_ukv": w_ukv,
        "g_mix": g_norm_mix.reshape(DEPTH, 1, D_MODEL),
        "g_qa": mla_g_qa.reshape(DEPTH, 1, -1), "g_kva": mla_g_kva.reshape(DEPTH, 1, -1),
        "gains": gains,
    }


def kernel(x, c, ctx, c_ctx, w_mod, b_mod, g_norm_mix, g_norm_mlp, w_in, mla_g_qa, mla_g_kva,
           mla_w_uq, mla_w_ukv, mla_g_q, mla_g_k, gqa_g_q, gqa_g_k, diff_g_q, diff_g_k,
           diff_lq1, diff_lk1, diff_lq2, diff_lk2, diff_g_out, na_g_q, na_g_k, na_rpb,
           w_out, w_up, w_down):
    tm_front = 256
    cc = jnp.concatenate([c, c_ctx[None, :], jnp.zeros((8 - BATCH - 1, D_MODEL), F32)], axis=0)
    mod = _modulation(cc, w_mod, b_mod)
    lat_tabs, ctx_tabs = _rope_tables(tm_front)
    pw = _prepare_params(w_in, mla_w_uq, mla_w_ukv, g_norm_mix, mla_g_qa, mla_g_kva, mla_g_q,
                         mla_g_k, gqa_g_q, gqa_g_k, diff_g_q, diff_g_k, diff_g_out, na_g_q,
                         na_g_k, na_rpb)
    w_out_b = w_out.astype(BF16)
    w_up_b = w_up.astype(BF16)
    w_down_b = w_down.astype(BF16)
    g_mlp = g_norm_mlp.reshape(DEPTH, 1, D_MODEL)

    xs = x.reshape(BATCH * SEQ, D_MODEL)
    cs = ctx.reshape(BATCH * CTX_LEN, D_MODEL)
    for l in range(DEPTH):
        need_ctx = l < DEPTH - 1
        lam_init = 0.8 - 0.6 * math.exp(-0.3 * l)
        m6 = mod[l].reshape(8, 6, 1, D_MODEL)
        sh_a, sc_a, gt_a, sh_m, sc_m, gt_m = [m6[:BATCH, k] for k in range(6)]
        csh_a, csc_a, cgt_a, csh_m, csc_m, cgt_m = [m6[BATCH:BATCH + 1, k] for k in range(6)]
        lam_rows = jnp.stack([diff_lq1[l], diff_lk1[l], diff_lq2[l], diff_lk2[l]]).astype(F32)
        g_out = pw["gains"][l, G_C_OUT:G_C_OUT + 1]

        fl = _front(xs, sc_a, sh_a, pw, l, lat_tabs, rows_per_mod=SEQ, tm=tm_front)
        fc = _front(cs, csc_a, csh_a, pw, l, ctx_tabs, rows_per_mod=BATCH * CTX_LEN, tm=tm_front)
        qa, ka, va, qb, kb, vb, qc, kc, vc, qd, kd, vd = fl
        cqa, cka, cva, cqb, ckb, cvb, cqc, ckc, cvc, cqd, ckd, cvd = fc

        o_a = _attention(pw["bound_a"][l], qa, ka, va, cka, cva, kind="mla", tq=1024)
        o_b = _attention(pw["bound_b"][l], qb, kb, vb, ckb, cvb, kind="gqa", tq=512)
        o_c = _attention(pw["bound_c"][l], qc, kc, vc, ckc, cvc, kind="diff", tq=512,
                         extra=(lam_rows, g_out), lam_init=lam_init)
        o_d = _na_attention(pw["bound_d"][l], qd, kd, vd, ckd, cvd, _na_bias(na_rpb, l))

        x_mid, h_mlp = _out_proj(xs, (o_a, o_b, o_c, o_d), w_out_b, l, gt_a, g_mlp, sc_m, sh_m,
                                 rows_per_mod=SEQ)
        xs = _mlp(h_mlp, x_mid, w_up_b, w_down_b, l, gt_m, rows_per_mod=SEQ)

        if need_ctx:
            oc = _ctx_attention(fc, lam_rows, g_out, lam_init)
            c_mid, ch_mlp = _out_proj(cs, oc, w_out_b, l, cgt_a, g_mlp, csc_m, csh_m,
                                      rows_per_mod=BATCH * CTX_LEN)
            cs = _mlp(ch_mlp, c_mid, w_up_b, w_down_b, l, cgt_m, rows_per_mod=BATCH * CTX_LEN)
    return xs.reshape(BATCH, SEQ, D_MODEL)
```

```python
import functools
import math

import numpy as np
import jax
import jax.numpy as jnp
from jax import lax
from jax.experimental import pallas as pl
from jax.experimental.pallas import tpu as pltpu

D_MODEL = 2048
BATCH = 4
SEQ = 4096
DEPTH = 2
GRID_W = 64
CTX_LEN = 256
HEAD_DIM = 128
GROUP_W = D_MODEL // 4
N_HEADS = GROUP_W // HEAD_DIM
D_FF = 4 * D_MODEL
ROPE_THETA = 10000.0
NORM_EPS = 1e-6

MLA_Q_RANK = GROUP_W
MLA_KV_RANK = GROUP_W // 2
MLA_NOPE = 128
MLA_ROPE = 64
GQA_KV_HEADS = N_HEADS // 2
DIFF_HALF = HEAD_DIM // 2
NA_WIN_H = 8
NA_WIN_W = 16
NA_QROWS = 8
NA_QBLOCK = NA_QROWS * GRID_W
NA_BLOCKS = SEQ // NA_QBLOCK
NA_BAND_ROWS = NA_WIN_H + NA_QROWS
NA_BAND = NA_BAND_ROWS * GRID_W
NA_PATTERNS = 3

A_COLS = MLA_Q_RANK + MLA_KV_RANK + MLA_ROPE
B_COLS = (N_HEADS + 2 * GQA_KV_HEADS) * HEAD_DIM
C_COLS = 3 * N_HEADS * HEAD_DIM
D_COLS = 3 * N_HEADS * HEAD_DIM

LANES = 128
OFF_AQ = 0
OFF_AKV = OFF_AQ + MLA_Q_RANK
OFF_KPE = OFF_AKV + MLA_KV_RANK
WA_COLS = OFF_KPE + LANES
OFF_B = 0
OFF_C = OFF_B + B_COLS
OFF_D = OFF_C + C_COLS
WBCD_COLS = OFF_D + D_COLS

VMEM_LIMIT = 60 * 1024 * 1024
NEG_BIG = -1e30
LOG2E = math.log2(math.e)
MAX_EXP2_SPAN = 96.0
ONES_ROWS = 16

F32 = jnp.float32
BF16 = jnp.bfloat16

G_A_QN, G_A_QR, G_A_KN, G_A_KR, G_B_Q, G_B_K, G_C_Q, G_C_K, G_D_Q, G_D_K, G_C_OUT = range(11)
GAIN_ROWS = 16


def _lane_iota(shape):
    return lax.broadcasted_iota(jnp.int32, shape, len(shape) - 1)


def _rms(t):
    return t * lax.rsqrt(jnp.mean(t * t, axis=-1, keepdims=True) + NORM_EPS)


def _rms64(t):
    lo = _lane_iota(t.shape) < 64
    sq = t * t
    s_lo = jnp.sum(jnp.where(lo, sq, 0.0), axis=-1, keepdims=True)
    s_hi = jnp.sum(jnp.where(lo, 0.0, sq), axis=-1, keepdims=True)
    ms = jnp.where(lo, s_lo, s_hi) * (1.0 / 64.0)
    return t * lax.rsqrt(ms + NORM_EPS)


def _rope(t, cos, sin_signed, half):
    first = (_lane_iota(t.shape) % (2 * half)) < half
    partner = jnp.where(first, pltpu.roll(t, LANES - half, 1), pltpu.roll(t, half, 1))
    return t * cos + partner * sin_signed


def _dot(a, b):
    return jnp.dot(a, b, preferred_element_type=F32)


def _dot_nt(a, b):
    return lax.dot_general(a, b, (((1,), (1,)), ((), ())), preferred_element_type=F32)


def _mod_kernel(c_ref, w_ref, b_ref, o_ref):
    c = c_ref[...]
    a = (c * jax.nn.sigmoid(c)).astype(BF16)
    o_ref[0] = _dot(a, w_ref[0].astype(BF16)) + b_ref[0]


def _modulation(cc, w_mod, b_mod):
    tn = 1536
    n = 6 * D_MODEL
    return pl.pallas_call(
        _mod_kernel,
        grid=(DEPTH, n // tn),
        in_specs=[
            pl.BlockSpec((8, D_MODEL), lambda l, j: (0, 0)),
            pl.BlockSpec((1, D_MODEL, tn), lambda l, j: (l, 0, j)),
            pl.BlockSpec((1, 1, tn), lambda l, j: (l, 0, j)),
        ],
        out_specs=pl.BlockSpec((1, 8, tn), lambda l, j: (l, 0, j)),
        out_shape=jax.ShapeDtypeStruct((DEPTH, 8, n), F32),
        compiler_params=pltpu.CompilerParams(
            dimension_semantics=("arbitrary", "arbitrary"), vmem_limit_bytes=VMEM_LIMIT),
        name="modulation",
    )(cc, w_mod, b_mod.reshape(DEPTH, 1, n))


def _front_kernel(x_ref, gmix_ref, sc_ref, sh_ref, wa_ref, wbcd_ref, w_uq_ref, w_ukv_ref,
                  gqa_ref, gkva_ref, gains_ref, cos128_ref, sin128_ref, cos64_ref, sin64_ref,
                  qa_ref, ka_ref, va_ref, qb_ref, kb_ref, vb_ref,
                  qc_ref, kc_ref, vc_ref, qd_ref, kd_ref, vd_ref):
    x = x_ref[...]
    gm = gmix_ref[0] * (1.0 + sc_ref[0])
    h = (_rms(x) * gm + sh_ref[0]).astype(BF16)

    gains = gains_ref[0]

    def gain(row):
        return gains[row:row + 1, :]

    cos128 = cos128_ref[...]
    sin128 = sin128_ref[...]
    cos64 = cos64_ref[...]
    sin64 = sin64_ref[...]
    lo = _lane_iota((x.shape[0], LANES)) < 64

    def proj(w_ref, off, width):
        return _dot(h, w_ref[0, :, off:off + width])

    def tile(t, j):
        return t[:, j * LANES:(j + 1) * LANES]

    def store_t(ref, hd, t):
        ref[hd] = t.T.astype(BF16)

    cq = (_rms(proj(wa_ref, OFF_AQ, MLA_Q_RANK)) * gqa_ref[0]).astype(BF16)
    ckv = (_rms(proj(wa_ref, OFF_AKV, MLA_KV_RANK)) * gkva_ref[0]).astype(BF16)
    kr = _rope(_rms64(proj(wa_ref, OFF_KPE, LANES)) * gain(G_A_KR), cos64, sin64, 16).astype(BF16)
    q = _dot(cq, w_uq_ref[0])
    kv = _dot(ckv, w_ukv_ref[0])
    qr_pairs = [
        _rope(_rms64(tile(q, N_HEADS + j)) * gain(G_A_QR), cos64, sin64, 16)
        for j in range(N_HEADS // 2)
    ]
    for hd in range(N_HEADS):
        qa_ref[hd, :, 0:LANES] = (_rms(tile(q, hd)) * gain(G_A_QN)).astype(BF16)
        pair = qr_pairs[hd // 2]
        keep = lo if hd % 2 == 0 else jnp.logical_not(lo)
        qa_ref[hd, :, LANES:2 * LANES] = jnp.where(keep, pair, 0.0).astype(BF16)
        ka_ref[hd, :, 0:LANES] = (_rms(tile(kv, hd)) * gain(G_A_KN)).astype(BF16)
        ka_ref[hd, :, LANES:2 * LANES] = kr
        store_t(va_ref, hd, tile(kv, N_HEADS + hd))

    pb = proj(wbcd_ref, OFF_B, B_COLS)
    for hd in range(N_HEADS):
        qb_ref[hd] = _rope(_rms(tile(pb, hd)) * gain(G_B_Q), cos128, sin128, 32).astype(BF16)
    for hd in range(GQA_KV_HEADS):
        kb_ref[hd] = _rope(_rms(tile(pb, N_HEADS + hd)) * gain(G_B_K),
                           cos128, sin128, 32).astype(BF16)
        store_t(vb_ref, hd, tile(pb, N_HEADS + GQA_KV_HEADS + hd))

    pc = proj(wbcd_ref, OFF_C, C_COLS)
    for hd in range(N_HEADS):
        qh = _rope(_rms64(tile(pc, hd)) * gain(G_C_Q), cos64, sin64, 16)
        qc_ref[2 * hd] = jnp.where(lo, qh, 0.0).astype(BF16)
        qc_ref[2 * hd + 1] = jnp.where(lo, 0.0, qh).astype(BF16)
        kc_ref[hd] = _rope(_rms64(tile(pc, N_HEADS + hd)) * gain(G_C_K),
                           cos64, sin64, 16).astype(BF16)
        store_t(vc_ref, hd, tile(pc, 2 * N_HEADS + hd))

    pd = proj(wbcd_ref, OFF_D, D_COLS)
    for hd in range(N_HEADS):
        qd_ref[hd] = (_rms(tile(pd, hd)) * gain(G_D_Q)).astype(BF16)
        kd_ref[hd] = (_rms(tile(pd, N_HEADS + hd)) * gain(G_D_K)).astype(BF16)
        store_t(vd_ref, hd, tile(pd, 2 * N_HEADS + hd))


_FRONT_OUT = (
    (N_HEADS, 2 * LANES, False), (N_HEADS, 2 * LANES, False), (N_HEADS, LANES, True),
    (N_HEADS, LANES, False), (GQA_KV_HEADS, LANES, False), (GQA_KV_HEADS, LANES, True),
    (2 * N_HEADS, LANES, False), (N_HEADS, LANES, False), (N_HEADS, LANES, True),
    (N_HEADS, LANES, False), (N_HEADS, LANES, False), (N_HEADS, LANES, True),
)


def _front(x2d, sc, sh, pw, layer, tabs, *, rows_per_mod, tm=256):
    t = x2d.shape[0]
    n_pos_tiles = tabs[0].shape[0] // tm
    tiles_per_mod = rows_per_mod // tm

    def stacked(a):
        return pl.BlockSpec((1,) + a.shape[1:], lambda i: (layer,) + (0,) * (a.ndim - 1),
                            pipeline_mode=pl.Buffered(1))

    mod_spec = pl.BlockSpec((1, 1, D_MODEL), lambda i: (i // tiles_per_mod, 0, 0))
    tab_spec = pl.BlockSpec((tm, LANES), lambda i: (i % n_pos_tiles, 0))
    params = [pw["g_mix"], None, None, pw["w_a"], pw["w_bcd"], pw["w_uq"], pw["w_ukv"],
              pw["g_qa"], pw["g_kva"], pw["gains"]]
    in_specs = [pl.BlockSpec((tm, D_MODEL), lambda i: (i, 0))]
    in_specs += [mod_spec if p is None else stacked(p) for p in params]
    in_specs += [tab_spec] * 4
    params[1], params[2] = sc, sh
    out_specs = [pl.BlockSpec((nh, w, tm), lambda i: (0, 0, i)) if tr
                 else pl.BlockSpec((nh, tm, w), lambda i: (0, i, 0)) for nh, w, tr in _FRONT_OUT]
    out_shape = [jax.ShapeDtypeStruct((nh, w, t) if tr else (nh, t, w), BF16)
                 for nh, w, tr in _FRONT_OUT]
    return pl.pallas_call(
        _front_kernel,
        grid=(t // tm,),
        in_specs=in_specs,
        out_specs=out_specs,
        out_shape=out_shape,
        compiler_params=pltpu.CompilerParams(
            dimension_semantics=("arbitrary",), vmem_limit_bytes=VMEM_LIMIT),
        name="front",
    )(x2d, *params, *tabs)


def _attend_t(q, chunks, shift=None):
    cols = q.shape[0]
    dv = chunks[0][1].shape[0]

    def scores(k, bias):
        st = _dot_nt(k, q)
        return st if bias is None else st + bias

    if shift is None:
        m = jnp.full((1, cols), NEG_BIG, F32)
        l = jnp.zeros((1, cols), F32)
        acc = jnp.zeros((dv, cols), F32)
        for k, vt, bias in chunks:
            st = scores(k, bias)
            m_new = jnp.maximum(m, jnp.max(st, axis=0, keepdims=True))
            alpha = jnp.exp2(m - m_new)
            p = jnp.exp2(st - m_new)
            l = alpha * l + jnp.sum(p, axis=0, keepdims=True)
            acc = alpha * acc + _dot(vt, p.astype(BF16))
            m = m_new
        return acc / l
    acc = jnp.zeros((dv + ONES_ROWS, cols), F32)
    for k, vt, bias in chunks:
        p = jnp.exp2(scores(k, bias) - shift).astype(BF16)
        vt_aug = jnp.concatenate([vt, jnp.ones((ONES_ROWS, vt.shape[1]), BF16)], axis=0)
        acc = acc + _dot(vt_aug, p)
    return acc[:dv] / acc[dv:dv + 1]


def _by_underflow_guard(bound_ref, run):
    no_underflow = bound_ref[1] <= MAX_EXP2_SPAN

    @pl.when(no_underflow)
    def _():
        run(bound_ref[0])

    @pl.when(jnp.logical_not(no_underflow))
    def _():
        run(None)


def _diff_lambda(lam_ref, lam_init):
    lam1 = jnp.exp(jnp.sum(lam_ref[0:1, :] * lam_ref[1:2, :], axis=-1, keepdims=True))
    lam2 = jnp.exp(jnp.sum(lam_ref[2:3, :] * lam_ref[3:4, :], axis=-1, keepdims=True))
    return lam1 - lam2 + lam_init


def _diff_combine(o, tq, lam, g_out, lam_init):
    d = o[:tq] - lam * o[tq:]
    return _rms(d) * (g_out * (1.0 - lam_init))


def _attn_kernel(bound_ref, q_ref, k_ref, v_ref, kc_ref, vc_ref, *rest, stacked, diff,
                 lam_init, tk, n_cast):
    if diff:
        lam_ref, gout_ref = rest[:2]
        rest = rest[2:]
    cast_in, o_ref, cast_out = rest[:n_cast], rest[n_cast], rest[n_cast + 1:]
    tq = q_ref.shape[1]
    for src, dst in zip(cast_in, cast_out):
        dst[...] = src[...].astype(dst.dtype)

    def run(shift):
        if stacked:
            q = jnp.concatenate([q_ref[0], q_ref[1]], axis=0)
        else:
            q = q_ref[0]
        chunks = [(k_ref[0, c * tk:(c + 1) * tk, :], v_ref[0, :, c * tk:(c + 1) * tk], None)
                  for c in range(SEQ // tk)]
        chunks.append((kc_ref[0], vc_ref[0], None))
        o = _attend_t(q, chunks, shift).T
        if diff:
            lam = _diff_lambda(lam_ref, lam_init)
            o_ref[0] = _diff_combine(o, tq, lam, gout_ref[...], lam_init).astype(o_ref.dtype)
        elif stacked:
            o_ref[0] = o[:tq].astype(o_ref.dtype)
            o_ref[1] = o[tq:].astype(o_ref.dtype)
        else:
            o_ref[0] = o.astype(o_ref.dtype)

    _by_underflow_guard(bound_ref, run)


def _attention(bound, q, k, v, kc, vc, *, kind, extra=(), cast=(), lam_init=0.0, tq=512,
               tk=1024):
    nq = SEQ // tq
    dk = q.shape[-1]
    stacked = kind in ("gqa", "diff")
    n_outer = GQA_KV_HEADS if kind == "gqa" else N_HEADS
    q_heads = 2 if stacked else 1
    o_heads = 2 if kind == "gqa" else 1
    n_steps = BATCH * n_outer * nq
    cast2d = [w.reshape(-1, w.shape[-1]) for w in cast]
    cast_specs = [pl.BlockSpec((w.shape[0] // n_steps, w.shape[1]),
                               lambda b, h, i: ((b * n_outer + h) * nq + i, 0)) for w in cast2d]
    in_specs = [
        pl.BlockSpec(memory_space=pltpu.SMEM),
        pl.BlockSpec((q_heads, tq, dk), lambda b, h, i: (h, b * nq + i, 0)),
        pl.BlockSpec((1, SEQ, dk), lambda b, h, i: (h, b, 0)),
        pl.BlockSpec((1, LANES, SEQ), lambda b, h, i: (h, 0, b)),
        pl.BlockSpec((1, CTX_LEN, dk), lambda b, h, i: (h, b, 0)),
        pl.BlockSpec((1, LANES, CTX_LEN), lambda b, h, i: (h, 0, b)),
    ]
    for e in extra:
        in_specs.append(pl.BlockSpec(e.shape, lambda b, h, i, nd=e.ndim: (0,) * nd))
    outs = pl.pallas_call(
        functools.partial(_attn_kernel, stacked=stacked, diff=(kind == "diff"),
                          lam_init=lam_init, tk=tk, n_cast=len(cast)),
        grid=(BATCH, n_outer, nq),
        in_specs=in_specs + cast_specs,
        out_specs=[pl.BlockSpec((o_heads, tq, LANES), lambda b, h, i: (h, b * nq + i, 0))]
        + cast_specs,
        out_shape=[jax.ShapeDtypeStruct((N_HEADS, BATCH * SEQ, LANES), BF16)]
        + [jax.ShapeDtypeStruct(w.shape, BF16) for w in cast2d],
        compiler_params=pltpu.CompilerParams(
            dimension_semantics=("arbitrary", "arbitrary", "arbitrary"),
            vmem_limit_bytes=VMEM_LIMIT),
        name="attn_" + kind,
    )(bound, q, k, v, kc, vc, *extra, *cast2d)
    if not cast:
        return outs[0]
    return [outs[0]] + [o.reshape(w.shape) for o, w in zip(outs[1:], cast)]


def _na_geometry(i):
    rows = SEQ // GRID_W
    r0 = i * NA_QROWS
    band_start = min(max(r0 - NA_WIN_H // 2, 0), rows - NA_BAND_ROWS)
    rel = np.zeros((NA_QROWS, NA_BAND_ROWS), np.int64)
    valid = np.zeros((NA_QROWS, NA_BAND_ROWS), bool)
    for a in range(NA_QROWS):
        q_row = r0 + a
        row_start = min(max(q_row - NA_WIN_H // 2, 0), rows - NA_WIN_H)
        assert band_start <= row_start and row_start + NA_WIN_H <= band_start + NA_BAND_ROWS
        for j in range(NA_BAND_ROWS):
            k_row = band_start + j
            valid[a, j] = row_start <= k_row < row_start + NA_WIN_H
            rel[a, j] = min(max(k_row - q_row + NA_WIN_H - 1, 0), 2 * NA_WIN_H - 2)
    return band_start, rel, valid


def _na_pattern_of(i):
    return 0 if i == 0 else (NA_PATTERNS - 1 if i == NA_BLOCKS - 1 else 1)


_NA_REPRESENTATIVE = (0, 1, NA_BLOCKS - 1)
for _i in range(NA_BLOCKS):
    _g, _r = _na_geometry(_i), _na_geometry(_NA_REPRESENTATIVE[_na_pattern_of(_i)])
    assert (_g[1] == _r[1]).all() and (_g[2] == _r[2]).all()


def _na_bias_kernel(rpb_ref, o_ref):
    hd = pl.program_id(0)
    n_r = 2 * NA_WIN_H - 1
    n_c = 2 * NA_WIN_W - 1
    k_col = lax.broadcasted_iota(jnp.int32, (GRID_W, LANES), 0)
    lane = lax.broadcasted_iota(jnp.int32, (GRID_W, LANES), 1)
    q_col = lane % GRID_W
    left = lane < GRID_W
    col_idx = jnp.clip(k_col - q_col + NA_WIN_W - 1, 0, n_c - 1)
    col_start = jnp.clip(q_col - NA_WIN_W // 2, 0, GRID_W - NA_WIN_W)
    col_ok = (k_col >= col_start) & (k_col < col_start + NA_WIN_W)
    tables = []
    for r in range(n_r):
        t = jnp.zeros((GRID_W, LANES), F32)
        for c in range(n_c):
            t = jnp.where(col_idx == c, rpb_ref[hd * (n_r * n_c) + r * n_c + c] * LOG2E, t)
        tables.append(jnp.where(col_ok, t, NEG_BIG))
    masked = jnp.full((GRID_W, LANES), NEG_BIG, F32)
    for p, i in enumerate(_NA_REPRESENTATIVE):
        _, rel, valid = _na_geometry(i)
        for j in range(NA_BAND_ROWS):
            for a in range(0, NA_QROWS, 2):
                t0 = tables[int(rel[a, j])] if valid[a, j] else masked
                t1 = tables[int(rel[a + 1, j])] if valid[a + 1, j] else masked
                blk = t0 if t0 is t1 else jnp.where(left, t0, t1)
                o_ref[0, p, j * GRID_W:(j + 1) * GRID_W, a * GRID_W:(a + 2) * GRID_W] = blk


def _na_bias(rpb, layer):
    n = N_HEADS * (2 * NA_WIN_H - 1) * (2 * NA_WIN_W - 1)
    return pl.pallas_call(
        _na_bias_kernel,
        grid=(N_HEADS,),
        in_specs=[pl.BlockSpec(memory_space=pltpu.SMEM)],
        out_specs=pl.BlockSpec((1, NA_PATTERNS, NA_BAND, NA_QBLOCK), lambda h: (h, 0, 0, 0)),
        out_shape=jax.ShapeDtypeStruct((N_HEADS, NA_PATTERNS, NA_BAND, NA_QBLOCK), F32),
        compiler_params=pltpu.CompilerParams(
            dimension_semantics=("arbitrary",), vmem_limit_bytes=VMEM_LIMIT),
        name="na_bias",
    )(rpb.reshape(DEPTH, n)[layer])


def _na_kernel(bound_ref, q_ref, k_ref, vt_ref, kc_ref, vct_ref, bias_ref, o_ref):
    def run(shift):
        kc = kc_ref[0]
        vct = vct_ref[0]
        for i in range(NA_BLOCKS):
            k0 = _na_geometry(i)[0] * GRID_W
            rows = slice(i * NA_QBLOCK, (i + 1) * NA_QBLOCK)
            chunks = [(k_ref[0, k0:k0 + NA_BAND, :], vt_ref[0, :, k0:k0 + NA_BAND],
                       bias_ref[0, _na_pattern_of(i)]),
                      (kc, vct, None)]
            o_ref[0, rows, :] = _attend_t(q_ref[0, rows, :], chunks, shift).T.astype(o_ref.dtype)

    _by_underflow_guard(bound_ref, run)


def _na_attention(bound, q, k, vt, kc, vct, bias):
    return pl.pallas_call(
        _na_kernel,
        grid=(BATCH, N_HEADS),
        in_specs=[
            pl.BlockSpec(memory_space=pltpu.SMEM),
            pl.BlockSpec((1, SEQ, LANES), lambda b, h: (h, b, 0)),
            pl.BlockSpec((1, SEQ, LANES), lambda b, h: (h, b, 0)),
            pl.BlockSpec((1, LANES, SEQ), lambda b, h: (h, 0, b)),
            pl.BlockSpec((1, CTX_LEN, LANES), lambda b, h: (h, b, 0)),
            pl.BlockSpec((1, LANES, CTX_LEN), lambda b, h: (h, 0, b)),
            pl.BlockSpec((1, NA_PATTERNS, NA_BAND, NA_QBLOCK), lambda b, h: (h, 0, 0, 0)),
        ],
        out_specs=pl.BlockSpec((1, SEQ, LANES), lambda b, h: (h, b, 0)),
        out_shape=jax.ShapeDtypeStruct((N_HEADS, BATCH * SEQ, LANES), BF16),
        compiler_params=pltpu.CompilerParams(
            dimension_semantics=("arbitrary", "arbitrary"), vmem_limit_bytes=VMEM_LIMIT),
        name="attn_na",
    )(bound, q, k, vt, kc, vct, bias)


def _ctx_attn_kernel(qa_ref, ka_ref, va_ref, qb_ref, kb_ref, vb_ref, qc_ref, kc_ref, vc_ref,
                     qd_ref, kd_ref, vd_ref, lam_ref, gout_ref,
                     oa_ref, ob_ref, oc_ref, od_ref, *, lam_init):
    n = CTX_LEN
    lam = _diff_lambda(lam_ref, lam_init)

    def attend(q, k_ref, vt_ref, hd):
        return _attend_t(q, [(k_ref[hd], vt_ref[hd], None)]).T

    for hd in range(N_HEADS):
        oa_ref[hd] = attend(qa_ref[hd], ka_ref, va_ref, hd).astype(BF16)
        kvh = hd // (N_HEADS // GQA_KV_HEADS)
        ob_ref[hd] = attend(qb_ref[hd], kb_ref, vb_ref, kvh).astype(BF16)
        q2 = jnp.concatenate([qc_ref[2 * hd], qc_ref[2 * hd + 1]], axis=0)
        o2 = attend(q2, kc_ref, vc_ref, hd)
        oc_ref[hd] = _diff_combine(o2, n, lam, gout_ref[...], lam_init).astype(BF16)
        od_ref[hd] = attend(qd_ref[hd], kd_ref, vd_ref, hd).astype(BF16)


def _ctx_attention(fc, lam_rows, g_out, lam_init):
    in_specs = [pl.BlockSpec((nh, w, CTX_LEN), lambda b: (0, 0, b)) if tr
                else pl.BlockSpec((nh, CTX_LEN, w), lambda b: (0, b, 0))
                for nh, w, tr in _FRONT_OUT]
    in_specs += [pl.BlockSpec(lam_rows.shape, lambda b: (0, 0)),
                 pl.BlockSpec(g_out.shape, lambda b: (0, 0))]
    o_spec = pl.BlockSpec((N_HEADS, CTX_LEN, LANES), lambda b: (0, b, 0))
    o_shape = jax.ShapeDtypeStruct((N_HEADS, BATCH * CTX_LEN, LANES), BF16)
    return pl.pallas_call(
        functools.partial(_ctx_attn_kernel, lam_init=lam_init),
        grid=(BATCH,),
        in_specs=in_specs,
        out_specs=[o_spec] * 4,
        out_shape=[o_shape] * 4,
        compiler_params=pltpu.CompilerParams(
            dimension_semantics=("arbitrary",), vmem_limit_bytes=VMEM_LIMIT),
        name="attn_ctx",
    )(*fc, lam_rows, g_out)


def _out_kernel(x_ref, oa_ref, ob_ref, oc_ref, od_ref, w_ref, gt_ref, g_ref, sc_ref, sh_ref,
                xo_ref, h_ref):
    parts = [r[hd] for r in (oa_ref, ob_ref, oc_ref, od_ref) for hd in range(N_HEADS)]
    o = jnp.concatenate(parts, axis=-1)
    x = x_ref[...] + gt_ref[0] * _dot(o, w_ref[0])
    xo_ref[...] = x
    h_ref[...] = (_rms(x) * (g_ref[0] * (1.0 + sc_ref[0])) + sh_ref[0]).astype(BF16)


def _out_proj(x2d, outs, w_out, layer, gt, g_mlp, sc, sh, *, rows_per_mod, tm=512):
    t = x2d.shape[0]
    tiles_per_mod = rows_per_mod // tm
    mod_spec = pl.BlockSpec((1, 1, D_MODEL), lambda i: (i // tiles_per_mod, 0, 0))
    o_spec = pl.BlockSpec((N_HEADS, tm, LANES), lambda i: (0, i, 0))
    row_spec = pl.BlockSpec((tm, D_MODEL), lambda i: (i, 0))
    return pl.pallas_call(
        _out_kernel,
        grid=(t // tm,),
        in_specs=[row_spec, o_spec, o_spec, o_spec, o_spec,
                  pl.BlockSpec((1, D_MODEL, D_MODEL), lambda i: (layer, 0, 0),
                               pipeline_mode=pl.Buffered(1)),
                  mod_spec, pl.BlockSpec((1, 1, D_MODEL), lambda i: (layer, 0, 0)),
                  mod_spec, mod_spec],
        out_specs=[row_spec, row_spec],
        out_shape=[jax.ShapeDtypeStruct((t, D_MODEL), F32),
                   jax.ShapeDtypeStruct((t, D_MODEL), BF16)],
        compiler_params=pltpu.CompilerParams(
            dimension_semantics=("arbitrary",), vmem_limit_bytes=VMEM_LIMIT),
        name="out_proj",
    )(x2d, *outs, w_out, gt, g_mlp, sc, sh)


def _mlp_kernel(h_ref, wu_ref, wd_ref, x_ref, gt_ref, o_ref, xs_ref):
    j = pl.program_id(1)
    last = pl.num_programs(1) - 1
    slab = x_ref.shape[0]
    xs_ref[pl.ds(pl.multiple_of(j * slab, slab), slab), :] = x_ref[...]

    def step(is_first, is_last):
        u = jnp.maximum(_dot(h_ref[...], wu_ref[0]), 0.0)
        y = _dot((u * u).astype(BF16), wd_ref[0])
        acc = y if is_first else o_ref[...] + y
        o_ref[...] = xs_ref[...] + gt_ref[0] * acc if is_last else acc

    pl.when(j == 0)(lambda: step(True, False))
    pl.when(jnp.logical_and(j > 0, j < last))(lambda: step(False, False))
    pl.when(j == last)(lambda: step(False, True))


def _mlp(h2d, x2d, w_up, w_down, layer, gt, *, rows_per_mod, tm=1024, tf=1024):
    t = x2d.shape[0]
    tiles_per_mod = rows_per_mod // tm
    nj = D_FF // tf
    slab = tm // nj
    return pl.pallas_call(
        _mlp_kernel,
        grid=(t // tm, nj),
        in_specs=[
            pl.BlockSpec((tm, D_MODEL), lambda i, j: (i, 0)),
            pl.BlockSpec((1, D_MODEL, tf), lambda i, j: (layer, 0, j)),
            pl.BlockSpec((1, tf, D_MODEL), lambda i, j: (layer, j, 0)),
            pl.BlockSpec((slab, D_MODEL), lambda i, j: (i * nj + j, 0)),
            pl.BlockSpec((1, 1, D_MODEL), lambda i, j: (i // tiles_per_mod, 0, 0)),
        ],
        out_specs=pl.BlockSpec((tm, D_MODEL), lambda i, j: (i, 0)),
        out_shape=jax.ShapeDtypeStruct((t, D_MODEL), F32),
        scratch_shapes=[pltpu.VMEM((tm, D_MODEL), F32)],
        compiler_params=pltpu.CompilerParams(
            dimension_semantics=("arbitrary", "arbitrary"), vmem_limit_bytes=VMEM_LIMIT),
        name="mlp",
    )(h2d, w_up, w_down, x2d, gt)


def _rope_tables(tm):
    t = jnp.arange(SEQ)
    row = (t // GRID_W).astype(F32)
    col = (t % GRID_W).astype(F32)

    def tables(rot_dim):
        half = rot_dim // 2
        inv_freq = ROPE_THETA ** (-jnp.arange(0, half, 2, dtype=F32) / half)
        ar = row[:, None] * inv_freq
        ac = col[:, None] * inv_freq
        cos = jnp.concatenate([jnp.cos(ar), jnp.cos(ar), jnp.cos(ac), jnp.cos(ac)], axis=-1)
        sin = jnp.concatenate([-jnp.sin(ar), jnp.sin(ar), -jnp.sin(ac), jnp.sin(ac)], axis=-1)
        return cos, sin

    cos128, sin128 = tables(HEAD_DIM)
    cos64, sin64 = tables(DIFF_HALF)
    lat = (cos128, sin128, jnp.tile(cos64, (1, 2)), jnp.tile(sin64, (1, 2)))
    ones = jnp.ones((tm, LANES), F32)
    zeros = jnp.zeros((tm, LANES), F32)
    return lat, (ones, zeros, ones, zeros)


def _prepare_params(w_in, mla_w_uq, mla_w_ukv, g_norm_mix, mla_g_qa, mla_g_kva, mla_g_q, mla_g_k,
                    gqa_g_q, gqa_g_k, diff_g_q, diff_g_k, diff_g_out, na_g_q, na_g_k, na_rpb):
    kpe = w_in[:, :, OFF_KPE:A_COLS]
    w_a = jnp.concatenate([w_in[:, :, :OFF_KPE], kpe, kpe], axis=2).astype(BF16)
    w_bcd = w_in[:, :, A_COLS:].astype(BF16)
    uq = mla_w_uq.reshape(DEPTH, MLA_Q_RANK, N_HEADS, MLA_NOPE + MLA_ROPE)
    w_uq = jnp.concatenate(
        [uq[..., :MLA_NOPE].reshape(DEPTH, MLA_Q_RANK, -1),
         uq[..., MLA_NOPE:].reshape(DEPTH, MLA_Q_RANK, -1)], axis=2).astype(BF16)
    ukv = mla_w_ukv.reshape(DEPTH, MLA_KV_RANK, N_HEADS, MLA_NOPE + HEAD_DIM)
    w_ukv = jnp.concatenate(
        [ukv[..., :MLA_NOPE].reshape(DEPTH, MLA_KV_RANK, -1),
         ukv[..., MLA_NOPE:].reshape(DEPTH, MLA_KV_RANK, -1)], axis=2).astype(BF16)

    scale_a = (MLA_NOPE + MLA_ROPE) ** -0.5 * LOG2E
    scale_b = HEAD_DIM ** -0.5 * LOG2E
    scale_c = DIFF_HALF ** -0.5 * LOG2E
    scale_d = HEAD_DIM ** -0.5 * LOG2E
    two = lambda g: jnp.tile(g, (1, 2))
    rows = [None] * 11
    rows[G_A_QN] = mla_g_q[:, :MLA_NOPE] * scale_a
    rows[G_A_QR] = two(mla_g_q[:, MLA_NOPE:]) * scale_a
    rows[G_A_KN] = mla_g_k[:, :MLA_NOPE]
    rows[G_A_KR] = two(mla_g_k[:, MLA_NOPE:])
    rows[G_B_Q] = gqa_g_q * scale_b
    rows[G_B_K] = gqa_g_k
    rows[G_C_Q] = two(diff_g_q) * scale_c
    rows[G_C_K] = two(diff_g_k)
    rows[G_D_Q] = na_g_q * scale_d
    rows[G_D_K] = na_g_k
    rows[G_C_OUT] = diff_g_out
    pad = [jnp.zeros((DEPTH, LANES), F32)] * (GAIN_ROWS - len(rows))
    gains = jnp.stack(rows + pad, axis=1).astype(F32)

    amax = lambda r: jnp.max(jnp.abs(rows[r]), axis=1)
    slack = 1.02
    b_a = slack * (jnp.sqrt(MLA_NOPE * amax(G_A_QN) ** 2 + MLA_ROPE * amax(G_A_QR) ** 2)
                   * jnp.sqrt(MLA_NOPE * amax(G_A_KN) ** 2 + MLA_ROPE * amax(G_A_KR) ** 2))
    b_b = slack * HEAD_DIM * amax(G_B_Q) * amax(G_B_K)
    b_c = slack * DIFF_HALF * amax(G_C_Q) * amax(G_C_K)
    b_d = slack * HEAD_DIM * amax(G_D_Q) * amax(G_D_K)
    bias_hi = jnp.maximum(jnp.max(na_rpb, axis=(1, 2, 3)), 0.0) * LOG2E
    bias_lo = jnp.minimum(jnp.min(na_rpb, axis=(1, 2, 3)), 0.0) * LOG2E
    span = lambda b: jnp.stack([b, 2.0 * b], axis=1).astype(F32)
    return {
        "bound_a": span(b_a), "bound_b": span(b_b), "bound_c": span(b_c),
        "bound_d": jnp.stack([b_d + bias_hi, 2.0 * b_d + bias_hi - bias_lo], axis=1).astype(F32),
        "w_a": w_a, "w_bcd": w_bcd, "w_uq": w_uq, "w_ukv": w_ukv,
        "g_mix": g_norm_mix.reshape(DEPTH, 1, D_MODEL),
        "g_qa": mla_g_qa.reshape(DEPTH, 1, -1), "g_kva": mla_g_kva.reshape(DEPTH, 1, -1),
        "gains": gains,
    }


def kernel(x, c, ctx, c_ctx, w_mod, b_mod, g_norm_mix, g_norm_mlp, w_in, mla_g_qa, mla_g_kva,
           mla_w_uq, mla_w_ukv, mla_g_q, mla_g_k, gqa_g_q, gqa_g_k, diff_g_q, diff_g_k,
           diff_lq1, diff_lk1, diff_lq2, diff_lk2, diff_g_out, na_g_q, na_g_k, na_rpb,
           w_out, w_up, w_down):
    tm_front = 512
    cc = jnp.concatenate([c, c_ctx[None, :], jnp.zeros((8 - BATCH - 1, D_MODEL), F32)], axis=0)
    mod = _modulation(cc, w_mod, b_mod)
    lat_tabs, ctx_tabs = _rope_tables(tm_front)
    pw = _prepare_params(w_in, mla_w_uq, mla_w_ukv, g_norm_mix, mla_g_qa, mla_g_kva, mla_g_q,
                         mla_g_k, gqa_g_q, gqa_g_k, diff_g_q, diff_g_k, diff_g_out, na_g_q,
                         na_g_k, na_rpb)
    g_mlp = g_norm_mlp.reshape(DEPTH, 1, D_MODEL)

    xs = x.reshape(BATCH * SEQ, D_MODEL)
    cs = ctx.reshape(BATCH * CTX_LEN, D_MODEL)
    for l in range(DEPTH):
        need_ctx = l < DEPTH - 1
        lam_init = 0.8 - 0.6 * math.exp(-0.3 * l)
        m6 = mod[l].reshape(8, 6, 1, D_MODEL)
        sh_a, sc_a, gt_a, sh_m, sc_m, gt_m = [m6[:BATCH, k] for k in range(6)]
        csh_a, csc_a, cgt_a, csh_m, csc_m, cgt_m = [m6[BATCH:BATCH + 1, k] for k in range(6)]
        lam_rows = jnp.stack([diff_lq1[l], diff_lk1[l], diff_lq2[l], diff_lk2[l]]).astype(F32)
        g_out = pw["gains"][l, G_C_OUT:G_C_OUT + 1]

        fl = _front(xs, sc_a, sh_a, pw, l, lat_tabs, rows_per_mod=SEQ, tm=tm_front)
        fc = _front(cs, csc_a, csh_a, pw, l, ctx_tabs, rows_per_mod=BATCH * CTX_LEN, tm=tm_front)
        qa, ka, va, qb, kb, vb, qc, kc, vc, qd, kd, vd = fl
        cqa, cka, cva, cqb, ckb, cvb, cqc, ckc, cvc, cqd, ckd, cvd = fc

        o_a = _attention(pw["bound_a"][l], qa, ka, va, cka, cva, kind="mla", tq=1024)
        o_b = _attention(pw["bound_b"][l], qb, kb, vb, ckb, cvb, kind="gqa", tq=512)
        o_c = _attention(pw["bound_c"][l], qc, kc, vc, ckc, cvc, kind="diff", tq=512,
                         extra=(lam_rows, g_out), lam_init=lam_init,
                         cast=(w_out, w_up, w_down) if l == 0 else ())
        if l == 0:
            o_c, w_out_b, w_up_b, w_down_b = o_c
        o_d = _na_attention(pw["bound_d"][l], qd, kd, vd, ckd, cvd, _na_bias(na_rpb, l))

        x_mid, h_mlp = _out_proj(xs, (o_a, o_b, o_c, o_d), w_out_b, l, gt_a, g_mlp, sc_m, sh_m,
                                 rows_per_mod=SEQ)
        xs = _mlp(h_mlp, x_mid, w_up_b, w_down_b, l, gt_m, rows_per_mod=SEQ)

        if need_ctx:
            oc = _ctx_attention(fc, lam_rows, g_out, lam_init)
            c_mid, ch_mlp = _out_proj(cs, oc, w_out_b, l, cgt_a, g_mlp, csc_m, csh_m,
                                      rows_per_mod=BATCH * CTX_LEN)
            cs = _mlp(ch_mlp, c_mid, w_up_b, w_down_b, l, cgt_m, rows_per_mod=BATCH * CTX_LEN)
    return xs.reshape(BATCH, SEQ, D_MODEL)
```

```python
import functools
import math

import numpy as np
import jax
import jax.numpy as jnp
from jax import lax
from jax.experimental import pallas as pl
from jax.experimental.pallas import tpu as pltpu

D_MODEL = 2048
BATCH = 4
SEQ = 4096
DEPTH = 2
GRID_W = 64
CTX_LEN = 256
HEAD_DIM = 128
GROUP_W = D_MODEL // 4
N_HEADS = GROUP_W // HEAD_DIM
D_FF = 4 * D_MODEL
ROPE_THETA = 10000.0
NORM_EPS = 1e-6

MLA_Q_RANK = GROUP_W
MLA_KV_RANK = GROUP_W // 2
MLA_NOPE = 128
MLA_ROPE = 64
GQA_KV_HEADS = N_HEADS // 2
DIFF_HALF = HEAD_DIM // 2
NA_WIN_H = 8
NA_WIN_W = 16
NA_QROWS = 8
NA_QBLOCK = NA_QROWS * GRID_W
NA_BLOCKS = SEQ // NA_QBLOCK
NA_BAND_ROWS = NA_WIN_H + NA_QROWS
NA_BAND = NA_BAND_ROWS * GRID_W
NA_PATTERNS = 3

A_COLS = MLA_Q_RANK + MLA_KV_RANK + MLA_ROPE
B_COLS = (N_HEADS + 2 * GQA_KV_HEADS) * HEAD_DIM
C_COLS = 3 * N_HEADS * HEAD_DIM
D_COLS = 3 * N_HEADS * HEAD_DIM

LANES = 128
OFF_AQ = 0
OFF_AKV = OFF_AQ + MLA_Q_RANK
OFF_KPE = OFF_AKV + MLA_KV_RANK
WA_COLS = OFF_KPE + LANES
OFF_B = 0
OFF_C = OFF_B + B_COLS
OFF_D = OFF_C + C_COLS
WBCD_COLS = OFF_D + D_COLS

VMEM_LIMIT = 60 * 1024 * 1024
NEG_BIG = -1e30
LOG2E = math.log2(math.e)
MAX_EXP2_SPAN = 96.0
ONES_ROWS = 16

F32 = jnp.float32
BF16 = jnp.bfloat16

G_A_QN, G_A_QR, G_A_KN, G_A_KR, G_B_Q, G_B_K, G_C_Q, G_C_K, G_D_Q, G_D_K, G_C_OUT = range(11)
GAIN_ROWS = 16


def _lane_iota(shape):
    return lax.broadcasted_iota(jnp.int32, shape, len(shape) - 1)


def _rms(t):
    return t * lax.rsqrt(jnp.mean(t * t, axis=-1, keepdims=True) + NORM_EPS)


def _rms64(t):
    lo = _lane_iota(t.shape) < 64
    sq = t * t
    s_lo = jnp.sum(jnp.where(lo, sq, 0.0), axis=-1, keepdims=True)
    s_hi = jnp.sum(jnp.where(lo, 0.0, sq), axis=-1, keepdims=True)
    ms = jnp.where(lo, s_lo, s_hi) * (1.0 / 64.0)
    return t * lax.rsqrt(ms + NORM_EPS)


def _rope(t, cos, sin_signed, half):
    first = (_lane_iota(t.shape) % (2 * half)) < half
    partner = jnp.where(first, pltpu.roll(t, LANES - half, 1), pltpu.roll(t, half, 1))
    return t * cos + partner * sin_signed


def _dot(a, b):
    return jnp.dot(a, b, preferred_element_type=F32)


def _dot_nt(a, b):
    return lax.dot_general(a, b, (((1,), (1,)), ((), ())), preferred_element_type=F32)


def _mod_kernel(c_ref, w_ref, b_ref, o_ref):
    c = c_ref[...]
    a = (c * jax.nn.sigmoid(c)).astype(BF16)
    o_ref[0] = _dot(a, w_ref[0].astype(BF16)) + b_ref[0]


def _modulation(cc, w_mod, b_mod):
    tn = 1536
    n = 6 * D_MODEL
    return pl.pallas_call(
        _mod_kernel,
        grid=(DEPTH, n // tn),
        in_specs=[
            pl.BlockSpec((8, D_MODEL), lambda l, j: (0, 0)),
            pl.BlockSpec((1, D_MODEL, tn), lambda l, j: (l, 0, j)),
            pl.BlockSpec((1, 1, tn), lambda l, j: (l, 0, j)),
        ],
        out_specs=pl.BlockSpec((1, 8, tn), lambda l, j: (l, 0, j)),
        out_shape=jax.ShapeDtypeStruct((DEPTH, 8, n), F32),
        compiler_params=pltpu.CompilerParams(
            dimension_semantics=("arbitrary", "arbitrary"), vmem_limit_bytes=VMEM_LIMIT),
        name="modulation",
    )(cc, w_mod, b_mod.reshape(DEPTH, 1, n))


def _w_in_prep_body(src_ref, wa_ref, wbcd_ref):
    s = src_ref[0]
    wa_ref[0, :, 0:OFF_KPE] = s[:, 0:OFF_KPE].astype(BF16)
    kpe = s[:, OFF_KPE:A_COLS].astype(BF16)
    wa_ref[0, :, OFF_KPE:OFF_KPE + MLA_ROPE] = kpe
    wa_ref[0, :, OFF_KPE + MLA_ROPE:WA_COLS] = kpe
    wbcd_ref[0] = s[:, A_COLS:].astype(BF16)


def _w_in_prep_specs(layer, rows, step_of):
    idx = lambda *g: (layer, step_of(*g), 0)
    in_spec = pl.BlockSpec((1, rows, A_COLS + WBCD_COLS), idx)
    out_specs = [pl.BlockSpec((1, rows, WA_COLS), lambda *g: (0, step_of(*g), 0)),
                 pl.BlockSpec((1, rows, WBCD_COLS), lambda *g: (0, step_of(*g), 0))]
    out_shapes = [jax.ShapeDtypeStruct((1, D_MODEL, WA_COLS), BF16),
                  jax.ShapeDtypeStruct((1, D_MODEL, WBCD_COLS), BF16)]
    return in_spec, out_specs, out_shapes


def _w_in_prep(w_in, layer, rows=128):
    in_spec, out_specs, out_shapes = _w_in_prep_specs(layer, rows, lambda i: i)
    return pl.pallas_call(
        _w_in_prep_body,
        grid=(D_MODEL // rows,),
        in_specs=[in_spec],
        out_specs=out_specs,
        out_shape=out_shapes,
        compiler_params=pltpu.CompilerParams(
            dimension_semantics=("arbitrary",), vmem_limit_bytes=VMEM_LIMIT),
        name="w_in_prep",
    )(w_in)


def _front_kernel(x_ref, gmix_ref, sc_ref, sh_ref, wa_ref, wbcd_ref, w_uq_ref, w_ukv_ref,
                  gqa_ref, gkva_ref, gains_ref, cos128_ref, sin128_ref, cos64_ref, sin64_ref,
                  qa_ref, ka_ref, va_ref, qb_ref, kb_ref, vb_ref,
                  qc_ref, kc_ref, vc_ref, qd_ref, kd_ref, vd_ref):
    x = x_ref[...]
    gm = gmix_ref[0] * (1.0 + sc_ref[0])
    h = (_rms(x) * gm + sh_ref[0]).astype(BF16)

    gains = gains_ref[0]

    def gain(row):
        return gains[row:row + 1, :]

    cos128 = cos128_ref[...]
    sin128 = sin128_ref[...]
    cos64 = cos64_ref[...]
    sin64 = sin64_ref[...]
    lo = _lane_iota((x.shape[0], LANES)) < 64

    def proj(w_ref, off, width):
        return _dot(h, w_ref[0, :, off:off + width])

    def tile(t, j):
        return t[:, j * LANES:(j + 1) * LANES]

    def store_t(ref, hd, t):
        ref[hd] = t.T.astype(BF16)

    cq = (_rms(proj(wa_ref, OFF_AQ, MLA_Q_RANK)) * gqa_ref[0]).astype(BF16)
    ckv = (_rms(proj(wa_ref, OFF_AKV, MLA_KV_RANK)) * gkva_ref[0]).astype(BF16)
    kr = _rope(_rms64(proj(wa_ref, OFF_KPE, LANES)) * gain(G_A_KR), cos64, sin64, 16).astype(BF16)
    q = _dot(cq, w_uq_ref[0])
    kv = _dot(ckv, w_ukv_ref[0])
    qr_pairs = [
        _rope(_rms64(tile(q, N_HEADS + j)) * gain(G_A_QR), cos64, sin64, 16)
        for j in range(N_HEADS // 2)
    ]
    for hd in range(N_HEADS):
        qa_ref[hd, :, 0:LANES] = (_rms(tile(q, hd)) * gain(G_A_QN)).astype(BF16)
        pair = qr_pairs[hd // 2]
        keep = lo if hd % 2 == 0 else jnp.logical_not(lo)
        qa_ref[hd, :, LANES:2 * LANES] = jnp.where(keep, pair, 0.0).astype(BF16)
        ka_ref[hd, :, 0:LANES] = (_rms(tile(kv, hd)) * gain(G_A_KN)).astype(BF16)
        ka_ref[hd, :, LANES:2 * LANES] = kr
        store_t(va_ref, hd, tile(kv, N_HEADS + hd))

    pb = proj(wbcd_ref, OFF_B, B_COLS)
    for hd in range(N_HEADS):
        qb_ref[hd] = _rope(_rms(tile(pb, hd)) * gain(G_B_Q), cos128, sin128, 32).astype(BF16)
    for hd in range(GQA_KV_HEADS):
        kb_ref[hd] = _rope(_rms(tile(pb, N_HEADS + hd)) * gain(G_B_K),
                           cos128, sin128, 32).astype(BF16)
        store_t(vb_ref, hd, tile(pb, N_HEADS + GQA_KV_HEADS + hd))

    pc = proj(wbcd_ref, OFF_C, C_COLS)
    for hd in range(N_HEADS):
        qh = _rope(_rms64(tile(pc, hd)) * gain(G_C_Q), cos64, sin64, 16)
        qc_ref[2 * hd] = jnp.where(lo, qh, 0.0).astype(BF16)
        qc_ref[2 * hd + 1] = jnp.where(lo, 0.0, qh).astype(BF16)
        kc_ref[hd] = _rope(_rms64(tile(pc, N_HEADS + hd)) * gain(G_C_K),
                           cos64, sin64, 16).astype(BF16)
        store_t(vc_ref, hd, tile(pc, 2 * N_HEADS + hd))

    pd = proj(wbcd_ref, OFF_D, D_COLS)
    for hd in range(N_HEADS):
        qd_ref[hd] = (_rms(tile(pd, hd)) * gain(G_D_Q)).astype(BF16)
        kd_ref[hd] = (_rms(tile(pd, N_HEADS + hd)) * gain(G_D_K)).astype(BF16)
        store_t(vd_ref, hd, tile(pd, 2 * N_HEADS + hd))


_FRONT_OUT = (
    (N_HEADS, 2 * LANES, False), (N_HEADS, 2 * LANES, False), (N_HEADS, LANES, True),
    (N_HEADS, LANES, False), (GQA_KV_HEADS, LANES, False), (GQA_KV_HEADS, LANES, True),
    (2 * N_HEADS, LANES, False), (N_HEADS, LANES, False), (N_HEADS, LANES, True),
    (N_HEADS, LANES, False), (N_HEADS, LANES, False), (N_HEADS, LANES, True),
)


def _front(x2d, sc, sh, pw, layer, w_in_parts, tabs, *, rows_per_mod, tm=256):
    t = x2d.shape[0]
    n_pos_tiles = tabs[0].shape[0] // tm
    tiles_per_mod = rows_per_mod // tm

    def resident(a):
        lead = layer if a.shape[0] == DEPTH else 0
        return pl.BlockSpec((1,) + a.shape[1:], lambda i: (lead,) + (0,) * (a.ndim - 1),
                            pipeline_mode=pl.Buffered(1))

    mod_spec = pl.BlockSpec((1, 1, D_MODEL), lambda i: (i // tiles_per_mod, 0, 0))
    tab_spec = pl.BlockSpec((tm, LANES), lambda i: (i % n_pos_tiles, 0))
    params = [pw["g_mix"], None, None, *w_in_parts, pw["w_uq"], pw["w_ukv"],
              pw["g_qa"], pw["g_kva"], pw["gains"]]
    in_specs = [pl.BlockSpec((tm, D_MODEL), lambda i: (i, 0))]
    in_specs += [mod_spec if p is None else resident(p) for p in params]
    in_specs += [tab_spec] * 4
    params[1], params[2] = sc, sh
    out_specs = [pl.BlockSpec((nh, w, tm), lambda i: (0, 0, i)) if tr
                 else pl.BlockSpec((nh, tm, w), lambda i: (0, i, 0)) for nh, w, tr in _FRONT_OUT]
    out_shape = [jax.ShapeDtypeStruct((nh, w, t) if tr else (nh, t, w), BF16)
                 for nh, w, tr in _FRONT_OUT]
    return pl.pallas_call(
        _front_kernel,
        grid=(t // tm,),
        in_specs=in_specs,
        out_specs=out_specs,
        out_shape=out_shape,
        compiler_params=pltpu.CompilerParams(
            dimension_semantics=("arbitrary",), vmem_limit_bytes=VMEM_LIMIT),
        name="front",
    )(x2d, *params, *tabs)


def _attend_t(q, chunks, shift=None):
    cols = q.shape[0]
    dv = chunks[0][1].shape[0]

    def scores(k, bias):
        st = _dot_nt(k, q)
        return st if bias is None else st + bias

    if shift is None:
        m = jnp.full((1, cols), NEG_BIG, F32)
        l = jnp.zeros((1, cols), F32)
        acc = jnp.zeros((dv, cols), F32)
        for k, vt, bias in chunks:
            st = scores(k, bias)
            m_new = jnp.maximum(m, jnp.max(st, axis=0, keepdims=True))
            alpha = jnp.exp2(m - m_new)
            p = jnp.exp2(st - m_new)
            l = alpha * l + jnp.sum(p, axis=0, keepdims=True)
            acc = alpha * acc + _dot(vt, p.astype(BF16))
            m = m_new
        return acc / l
    acc = jnp.zeros((dv + ONES_ROWS, cols), F32)
    for k, vt, bias in chunks:
        p = jnp.exp2(scores(k, bias) - shift).astype(BF16)
        vt_aug = jnp.concatenate([vt, jnp.ones((ONES_ROWS, vt.shape[1]), BF16)], axis=0)
        acc = acc + _dot(vt_aug, p)
    return acc[:dv] / acc[dv:dv + 1]


def _by_underflow_guard(bound_ref, run):
    no_underflow = bound_ref[1] <= MAX_EXP2_SPAN

    @pl.when(no_underflow)
    def _():
        run(bound_ref[0])

    @pl.when(jnp.logical_not(no_underflow))
    def _():
        run(None)


def _diff_lambda(lam_ref, lam_init):
    lam1 = jnp.exp(jnp.sum(lam_ref[0:1, :] * lam_ref[1:2, :], axis=-1, keepdims=True))
    lam2 = jnp.exp(jnp.sum(lam_ref[2:3, :] * lam_ref[3:4, :], axis=-1, keepdims=True))
    return lam1 - lam2 + lam_init


def _diff_combine(o, tq, lam, g_out, lam_init):
    d = o[:tq] - lam * o[tq:]
    return _rms(d) * (g_out * (1.0 - lam_init))


def _attn_kernel(bound_ref, q_ref, k_ref, v_ref, kc_ref, vc_ref, *rest, stacked, diff,
                 lam_init, tk, n_cast, prep_w_in):
    if diff:
        lam_ref, gout_ref = rest[:2]
        rest = rest[2:]
    n_in = n_cast + (1 if prep_w_in else 0)
    ride_in, o_ref, ride_out = rest[:n_in], rest[n_in], rest[n_in + 1:]
    tq = q_ref.shape[1]
    for src, dst in zip(ride_in[:n_cast], ride_out[:n_cast]):
        dst[...] = src[...].astype(dst.dtype)
    if prep_w_in:
        _w_in_prep_body(ride_in[n_cast], *ride_out[n_cast:])

    def run(shift):
        if stacked:
            q = jnp.concatenate([q_ref[0], q_ref[1]], axis=0)
        else:
            q = q_ref[0]
        chunks = [(k_ref[0, c * tk:(c + 1) * tk, :], v_ref[0, :, c * tk:(c + 1) * tk], None)
                  for c in range(SEQ // tk)]
        chunks.append((kc_ref[0], vc_ref[0], None))
        o = _attend_t(q, chunks, shift).T
        if diff:
            lam = _diff_lambda(lam_ref, lam_init)
            o_ref[0] = _diff_combine(o, tq, lam, gout_ref[...], lam_init).astype(o_ref.dtype)
        elif stacked:
            o_ref[0] = o[:tq].astype(o_ref.dtype)
            o_ref[1] = o[tq:].astype(o_ref.dtype)
        else:
            o_ref[0] = o.astype(o_ref.dtype)

    _by_underflow_guard(bound_ref, run)


def _attention(bound, q, k, v, kc, vc, *, kind, extra=(), cast=(), prep_w_in=None,
               lam_init=0.0, tq=512, tk=1024):
    nq = SEQ // tq
    dk = q.shape[-1]
    stacked = kind in ("gqa", "diff")
    n_outer = GQA_KV_HEADS if kind == "gqa" else N_HEADS
    q_heads = 2 if stacked else 1
    o_heads = 2 if kind == "gqa" else 1
    n_steps = BATCH * n_outer * nq
    step_of = lambda b, h, i: (b * n_outer + h) * nq + i
    cast2d = [w.reshape(-1, w.shape[-1]) for w in cast]
    ride_in_specs = [pl.BlockSpec((w.shape[0] // n_steps, w.shape[1]),
                                  lambda b, h, i: (step_of(b, h, i), 0)) for w in cast2d]
    ride_out_specs = list(ride_in_specs)
    ride_out_shapes = [jax.ShapeDtypeStruct(w.shape, BF16) for w in cast2d]
    ride_args = list(cast2d)
    if prep_w_in is not None:
        in_spec, out_specs, out_shapes = _w_in_prep_specs(prep_w_in[1], D_MODEL // n_steps, step_of)
        ride_in_specs.append(in_spec)
        ride_out_specs += out_specs
        ride_out_shapes += out_shapes
        ride_args.append(prep_w_in[0])
    in_specs = [
        pl.BlockSpec(memory_space=pltpu.SMEM),
        pl.BlockSpec((q_heads, tq, dk), lambda b, h, i: (h, b * nq + i, 0)),
        pl.BlockSpec((1, SEQ, dk), lambda b, h, i: (h, b, 0)),
        pl.BlockSpec((1, LANES, SEQ), lambda b, h, i: (h, 0, b)),
        pl.BlockSpec((1, CTX_LEN, dk), lambda b, h, i: (h, b, 0)),
        pl.BlockSpec((1, LANES, CTX_LEN), lambda b, h, i: (h, 0, b)),
    ]
    for e in extra:
        in_specs.append(pl.BlockSpec(e.shape, lambda b, h, i, nd=e.ndim: (0,) * nd))
    outs = pl.pallas_call(
        functools.partial(_attn_kernel, stacked=stacked, diff=(kind == "diff"),
                          lam_init=lam_init, tk=tk, n_cast=len(cast),
                          prep_w_in=prep_w_in is not None),
        grid=(BATCH, n_outer, nq),
        in_specs=in_specs + ride_in_specs,
        out_specs=[pl.BlockSpec((o_heads, tq, LANES), lambda b, h, i: (h, b * nq + i, 0))]
        + ride_out_specs,
        out_shape=[jax.ShapeDtypeStruct((N_HEADS, BATCH * SEQ, LANES), BF16)] + ride_out_shapes,
        compiler_params=pltpu.CompilerParams(
            dimension_semantics=("arbitrary", "arbitrary", "arbitrary"),
            vmem_limit_bytes=VMEM_LIMIT),
        name="attn_" + kind,
    )(bound, q, k, v, kc, vc, *extra, *ride_args)
    if len(outs) == 1:
        return outs[0]
    n = 1 + len(cast)
    return [outs[0]] + [o.reshape(w.shape) for o, w in zip(outs[1:n], cast)] + list(outs[n:])


def _na_geometry(i):
    rows = SEQ // GRID_W
    r0 = i * NA_QROWS
    band_start = min(max(r0 - NA_WIN_H // 2, 0), rows - NA_BAND_ROWS)
    rel = np.zeros((NA_QROWS, NA_BAND_ROWS), np.int64)
    valid = np.zeros((NA_QROWS, NA_BAND_ROWS), bool)
    for a in range(NA_QROWS):
        q_row = r0 + a
        row_start = min(max(q_row - NA_WIN_H // 2, 0), rows - NA_WIN_H)
        assert band_start <= row_start and row_start + NA_WIN_H <= band_start + NA_BAND_ROWS
        for j in range(NA_BAND_ROWS):
            k_row = band_start + j
            valid[a, j] = row_start <= k_row < row_start + NA_WIN_H
            rel[a, j] = min(max(k_row - q_row + NA_WIN_H - 1, 0), 2 * NA_WIN_H - 2)
    return band_start, rel, valid


def _na_pattern_of(i):
    return 0 if i == 0 else (NA_PATTERNS - 1 if i == NA_BLOCKS - 1 else 1)


_NA_REPRESENTATIVE = (0, 1, NA_BLOCKS - 1)
for _i in range(NA_BLOCKS):
    _g, _r = _na_geometry(_i), _na_geometry(_NA_REPRESENTATIVE[_na_pattern_of(_i)])
    assert (_g[1] == _r[1]).all() and (_g[2] == _r[2]).all()


def _na_bias_kernel(rpb_ref, o_ref):
    hd = pl.program_id(0)
    n_r = 2 * NA_WIN_H - 1
    n_c = 2 * NA_WIN_W - 1
    k_col = lax.broadcasted_iota(jnp.int32, (GRID_W, LANES), 0)
    lane = lax.broadcasted_iota(jnp.int32, (GRID_W, LANES), 1)
    q_col = lane % GRID_W
    left = lane < GRID_W
    col_idx = jnp.clip(k_col - q_col + NA_WIN_W - 1, 0, n_c - 1)
    col_start = jnp.clip(q_col - NA_WIN_W // 2, 0, GRID_W - NA_WIN_W)
    col_ok = (k_col >= col_start) & (k_col < col_start + NA_WIN_W)
    tables = []
    for r in range(n_r):
        t = jnp.zeros((GRID_W, LANES), F32)
        for c in range(n_c):
            t = jnp.where(col_idx == c, rpb_ref[hd * (n_r * n_c) + r * n_c + c] * LOG2E, t)
        tables.append(jnp.where(col_ok, t, NEG_BIG))
    masked = jnp.full((GRID_W, LANES), NEG_BIG, F32)
    for p, i in enumerate(_NA_REPRESENTATIVE):
        _, rel, valid = _na_geometry(i)
        for j in range(NA_BAND_ROWS):
            for a in range(0, NA_QROWS, 2):
                t0 = tables[int(rel[a, j])] if valid[a, j] else masked
                t1 = tables[int(rel[a + 1, j])] if valid[a + 1, j] else masked
                blk = t0 if t0 is t1 else jnp.where(left, t0, t1)
                o_ref[0, p, j * GRID_W:(j + 1) * GRID_W, a * GRID_W:(a + 2) * GRID_W] = blk


def _na_bias(rpb, layer):
    n = N_HEADS * (2 * NA_WIN_H - 1) * (2 * NA_WIN_W - 1)
    return pl.pallas_call(
        _na_bias_kernel,
        grid=(N_HEADS,),
        in_specs=[pl.BlockSpec(memory_space=pltpu.SMEM)],
        out_specs=pl.BlockSpec((1, NA_PATTERNS, NA_BAND, NA_QBLOCK), lambda h: (h, 0, 0, 0)),
        out_shape=jax.ShapeDtypeStruct((N_HEADS, NA_PATTERNS, NA_BAND, NA_QBLOCK), F32),
        compiler_params=pltpu.CompilerParams(
            dimension_semantics=("arbitrary",), vmem_limit_bytes=VMEM_LIMIT),
        name="na_bias",
    )(rpb.reshape(DEPTH, n)[layer])


def _na_kernel(bound_ref, q_ref, k_ref, vt_ref, kc_ref, vct_ref, bias_ref, o_ref):
    def run(shift):
        kc = kc_ref[0]
        vct = vct_ref[0]
        for i in range(NA_BLOCKS):
            k0 = _na_geometry(i)[0] * GRID_W
            rows = slice(i * NA_QBLOCK, (i + 1) * NA_QBLOCK)
            chunks = [(k_ref[0, k0:k0 + NA_BAND, :], vt_ref[0, :, k0:k0 + NA_BAND],
                       bias_ref[0, _na_pattern_of(i)]),
                      (kc, vct, None)]
            o_ref[0, rows, :] = _attend_t(q_ref[0, rows, :], chunks, shift).T.astype(o_ref.dtype)

    _by_underflow_guard(bound_ref, run)


def _na_attention(bound, q, k, vt, kc, vct, bias):
    return pl.pallas_call(
        _na_kernel,
        grid=(BATCH, N_HEADS),
        in_specs=[
            pl.BlockSpec(memory_space=pltpu.SMEM),
            pl.BlockSpec((1, SEQ, LANES), lambda b, h: (h, b, 0)),
            pl.BlockSpec((1, SEQ, LANES), lambda b, h: (h, b, 0)),
            pl.BlockSpec((1, LANES, SEQ), lambda b, h: (h, 0, b)),
            pl.BlockSpec((1, CTX_LEN, LANES), lambda b, h: (h, b, 0)),
            pl.BlockSpec((1, LANES, CTX_LEN), lambda b, h: (h, 0, b)),
            pl.BlockSpec((1, NA_PATTERNS, NA_BAND, NA_QBLOCK), lambda b, h: (h, 0, 0, 0)),
        ],
        out_specs=pl.BlockSpec((1, SEQ, LANES), lambda b, h: (h, b, 0)),
        out_shape=jax.ShapeDtypeStruct((N_HEADS, BATCH * SEQ, LANES), BF16),
        compiler_params=pltpu.CompilerParams(
            dimension_semantics=("arbitrary", "arbitrary"), vmem_limit_bytes=VMEM_LIMIT),
        name="attn_na",
    )(bound, q, k, vt, kc, vct, bias)


def _ctx_attn_kernel(qa_ref, ka_ref, va_ref, qb_ref, kb_ref, vb_ref, qc_ref, kc_ref, vc_ref,
                     qd_ref, kd_ref, vd_ref, lam_ref, gout_ref,
                     oa_ref, ob_ref, oc_ref, od_ref, *, lam_init):
    n = CTX_LEN
    lam = _diff_lambda(lam_ref, lam_init)

    def attend(q, k_ref, vt_ref, hd):
        return _attend_t(q, [(k_ref[hd], vt_ref[hd], None)]).T

    for hd in range(N_HEADS):
        oa_ref[hd] = attend(qa_ref[hd], ka_ref, va_ref, hd).astype(BF16)
        kvh = hd // (N_HEADS // GQA_KV_HEADS)
        ob_ref[hd] = attend(qb_ref[hd], kb_ref, vb_ref, kvh).astype(BF16)
        q2 = jnp.concatenate([qc_ref[2 * hd], qc_ref[2 * hd + 1]], axis=0)
        o2 = attend(q2, kc_ref, vc_ref, hd)
        oc_ref[hd] = _diff_combine(o2, n, lam, gout_ref[...], lam_init).astype(BF16)
        od_ref[hd] = attend(qd_ref[hd], kd_ref, vd_ref, hd).astype(BF16)


def _ctx_attention(fc, lam_rows, g_out, lam_init):
    in_specs = [pl.BlockSpec((nh, w, CTX_LEN), lambda b: (0, 0, b)) if tr
                else pl.BlockSpec((nh, CTX_LEN, w), lambda b: (0, b, 0))
                for nh, w, tr in _FRONT_OUT]
    in_specs += [pl.BlockSpec(lam_rows.shape, lambda b: (0, 0)),
                 pl.BlockSpec(g_out.shape, lambda b: (0, 0))]
    o_spec = pl.BlockSpec((N_HEADS, CTX_LEN, LANES), lambda b: (0, b, 0))
    o_shape = jax.ShapeDtypeStruct((N_HEADS, BATCH * CTX_LEN, LANES), BF16)
    return pl.pallas_call(
        functools.partial(_ctx_attn_kernel, lam_init=lam_init),
        grid=(BATCH,),
        in_specs=in_specs,
        out_specs=[o_spec] * 4,
        out_shape=[o_shape] * 4,
        compiler_params=pltpu.CompilerParams(
            dimension_semantics=("arbitrary",), vmem_limit_bytes=VMEM_LIMIT),
        name="attn_ctx",
    )(*fc, lam_rows, g_out)


def _out_kernel(x_ref, oa_ref, ob_ref, oc_ref, od_ref, w_ref, gt_ref, g_ref, sc_ref, sh_ref,
                xo_ref, h_ref):
    parts = [r[hd] for r in (oa_ref, ob_ref, oc_ref, od_ref) for hd in range(N_HEADS)]
    o = jnp.concatenate(parts, axis=-1)
    x = x_ref[...] + gt_ref[0] * _dot(o, w_ref[0])
    xo_ref[...] = x
    h_ref[...] = (_rms(x) * (g_ref[0] * (1.0 + sc_ref[0])) + sh_ref[0]).astype(BF16)


def _out_proj(x2d, outs, w_out, layer, gt, g_mlp, sc, sh, *, rows_per_mod, tm=512):
    t = x2d.shape[0]
    tiles_per_mod = rows_per_mod // tm
    mod_spec = pl.BlockSpec((1, 1, D_MODEL), lambda i: (i // tiles_per_mod, 0, 0))
    o_spec = pl.BlockSpec((N_HEADS, tm, LANES), lambda i: (0, i, 0))
    row_spec = pl.BlockSpec((tm, D_MODEL), lambda i: (i, 0))
    return pl.pallas_call(
        _out_kernel,
        grid=(t // tm,),
        in_specs=[row_spec, o_spec, o_spec, o_spec, o_spec,
                  pl.BlockSpec((1, D_MODEL, D_MODEL), lambda i: (layer, 0, 0),
                               pipeline_mode=pl.Buffered(1)),
                  mod_spec, pl.BlockSpec((1, 1, D_MODEL), lambda i: (layer, 0, 0)),
                  mod_spec, mod_spec],
        out_specs=[row_spec, row_spec],
        out_shape=[jax.ShapeDtypeStruct((t, D_MODEL), F32),
                   jax.ShapeDtypeStruct((t, D_MODEL), BF16)],
        compiler_params=pltpu.CompilerParams(
            dimension_semantics=("arbitrary",), vmem_limit_bytes=VMEM_LIMIT),
        name="out_proj",
    )(x2d, *outs, w_out, gt, g_mlp, sc, sh)


def _mlp_kernel(h_ref, wu_ref, wd_ref, x_ref, gt_ref, o_ref, xs_ref):
    j = pl.program_id(1)
    last = pl.num_programs(1) - 1
    slab = x_ref.shape[0]
    xs_ref[pl.ds(pl.multiple_of(j * slab, slab), slab), :] = x_ref[...]

    def step(is_first, is_last):
        u = jnp.maximum(_dot(h_ref[...], wu_ref[0]), 0.0)
        y = _dot((u * u).astype(BF16), wd_ref[0])
        acc = y if is_first else o_ref[...] + y
        o_ref[...] = xs_ref[...] + gt_ref[0] * acc if is_last else acc

    pl.when(j == 0)(lambda: step(True, False))
    pl.when(jnp.logical_and(j > 0, j < last))(lambda: step(False, False))
    pl.when(j == last)(lambda: step(False, True))


def _mlp(h2d, x2d, w_up, w_down, layer, gt, *, rows_per_mod, tm=1024, tf=1024):
    t = x2d.shape[0]
    tiles_per_mod = rows_per_mod // tm
    nj = D_FF // tf
    slab = tm // nj
    return pl.pallas_call(
        _mlp_kernel,
        grid=(t // tm, nj),
        in_specs=[
            pl.BlockSpec((tm, D_MODEL), lambda i, j: (i, 0)),
            pl.BlockSpec((1, D_MODEL, tf), lambda i, j: (layer, 0, j)),
            pl.BlockSpec((1, tf, D_MODEL), lambda i, j: (layer, j, 0)),
            pl.BlockSpec((slab, D_MODEL), lambda i, j: (i * nj + j, 0)),
            pl.BlockSpec((1, 1, D_MODEL), lambda i, j: (i // tiles_per_mod, 0, 0)),
        ],
        out_specs=pl.BlockSpec((tm, D_MODEL), lambda i, j: (i, 0)),
        out_shape=jax.ShapeDtypeStruct((t, D_MODEL), F32),
        scratch_shapes=[pltpu.VMEM((tm, D_MODEL), F32)],
        compiler_params=pltpu.CompilerParams(
            dimension_semantics=("arbitrary", "arbitrary"), vmem_limit_bytes=VMEM_LIMIT),
        name="mlp",
    )(h2d, w_up, w_down, x2d, gt)


def _rope_tables(tm):
    t = jnp.arange(SEQ)
    row = (t // GRID_W).astype(F32)
    col = (t % GRID_W).astype(F32)

    def tables(rot_dim):
        half = rot_dim // 2
        inv_freq = ROPE_THETA ** (-jnp.arange(0, half, 2, dtype=F32) / half)
        ar = row[:, None] * inv_freq
        ac = col[:, None] * inv_freq
        cos = jnp.concatenate([jnp.cos(ar), jnp.cos(ar), jnp.cos(ac), jnp.cos(ac)], axis=-1)
        sin = jnp.concatenate([-jnp.sin(ar), jnp.sin(ar), -jnp.sin(ac), jnp.sin(ac)], axis=-1)
        return cos, sin

    cos128, sin128 = tables(HEAD_DIM)
    cos64, sin64 = tables(DIFF_HALF)
    lat = (cos128, sin128, jnp.tile(cos64, (1, 2)), jnp.tile(sin64, (1, 2)))
    ones = jnp.ones((tm, LANES), F32)
    zeros = jnp.zeros((tm, LANES), F32)
    return lat, (ones, zeros, ones, zeros)


def _prepare_params(w_in, mla_w_uq, mla_w_ukv, g_norm_mix, mla_g_qa, mla_g_kva, mla_g_q, mla_g_k,
                    gqa_g_q, gqa_g_k, diff_g_q, diff_g_k, diff_g_out, na_g_q, na_g_k, na_rpb):
    uq = mla_w_uq.reshape(DEPTH, MLA_Q_RANK, N_HEADS, MLA_NOPE + MLA_ROPE)
    w_uq = jnp.concatenate(
        [uq[..., :MLA_NOPE].reshape(DEPTH, MLA_Q_RANK, -1),
         uq[..., MLA_NOPE:].reshape(DEPTH, MLA_Q_RANK, -1)], axis=2).astype(BF16)
    ukv = mla_w_ukv.reshape(DEPTH, MLA_KV_RANK, N_HEADS, MLA_NOPE + HEAD_DIM)
    w_ukv = jnp.concatenate(
        [ukv[..., :MLA_NOPE].reshape(DEPTH, MLA_KV_RANK, -1),
         ukv[..., MLA_NOPE:].reshape(DEPTH, MLA_KV_RANK, -1)], axis=2).astype(BF16)

    scale_a = (MLA_NOPE + MLA_ROPE) ** -0.5 * LOG2E
    scale_b = HEAD_DIM ** -0.5 * LOG2E
    scale_c = DIFF_HALF ** -0.5 * LOG2E
    scale_d = HEAD_DIM ** -0.5 * LOG2E
    two = lambda g: jnp.tile(g, (1, 2))
    rows = [None] * 11
    rows[G_A_QN] = mla_g_q[:, :MLA_NOPE] * scale_a
    rows[G_A_QR] = two(mla_g_q[:, MLA_NOPE:]) * scale_a
    rows[G_A_KN] = mla_g_k[:, :MLA_NOPE]
    rows[G_A_KR] = two(mla_g_k[:, MLA_NOPE:])
    rows[G_B_Q] = gqa_g_q * scale_b
    rows[G_B_K] = gqa_g_k
    rows[G_C_Q] = two(diff_g_q) * scale_c
    rows[G_C_K] = two(diff_g_k)
    rows[G_D_Q] = na_g_q * scale_d
    rows[G_D_K] = na_g_k
    rows[G_C_OUT] = diff_g_out
    pad = [jnp.zeros((DEPTH, LANES), F32)] * (GAIN_ROWS - len(rows))
    gains = jnp.stack(rows + pad, axis=1).astype(F32)

    amax = lambda r: jnp.max(jnp.abs(rows[r]), axis=1)
    slack = 1.02
    b_a = slack * (jnp.sqrt(MLA_NOPE * amax(G_A_QN) ** 2 + MLA_ROPE * amax(G_A_QR) ** 2)
                   * jnp.sqrt(MLA_NOPE * amax(G_A_KN) ** 2 + MLA_ROPE * amax(G_A_KR) ** 2))
    b_b = slack * HEAD_DIM * amax(G_B_Q) * amax(G_B_K)
    b_c = slack * DIFF_HALF * amax(G_C_Q) * amax(G_C_K)
    b_d = slack * HEAD_DIM * amax(G_D_Q) * amax(G_D_K)
    bias_hi = jnp.maximum(jnp.max(na_rpb, axis=(1, 2, 3)), 0.0) * LOG2E
    bias_lo = jnp.minimum(jnp.min(na_rpb, axis=(1, 2, 3)), 0.0) * LOG2E
    span = lambda b: jnp.stack([b, 2.0 * b], axis=1).astype(F32)
    return {
        "bound_a": span(b_a), "bound_b": span(b_b), "bound_c": span(b_c),
        "bound_d": jnp.stack([b_d + bias_hi, 2.0 * b_d + bias_hi - bias_lo], axis=1).astype(F32),
        "w_uq": w_uq, "w_ukv": w_ukv,
        "g_mix": g_norm_mix.reshape(DEPTH, 1, D_MODEL),
        "g_qa": mla_g_qa.reshape(DEPTH, 1, -1), "g_kva": mla_g_kva.reshape(DEPTH, 1, -1),
        "gains": gains,
    }


def kernel(x, c, ctx, c_ctx, w_mod, b_mod, g_norm_mix, g_norm_mlp, w_in, mla_g_qa, mla_g_kva,
           mla_w_uq, mla_w_ukv, mla_g_q, mla_g_k, gqa_g_q, gqa_g_k, diff_g_q, diff_g_k,
           diff_lq1, diff_lk1, diff_lq2, diff_lk2, diff_g_out, na_g_q, na_g_k, na_rpb,
           w_out, w_up, w_down):
    tm_front = 512
    cc = jnp.concatenate([c, c_ctx[None, :], jnp.zeros((8 - BATCH - 1, D_MODEL), F32)], axis=0)
    mod = _modulation(cc, w_mod, b_mod)
    lat_tabs, ctx_tabs = _rope_tables(tm_front)
    pw = _prepare_params(w_in, mla_w_uq, mla_w_ukv, g_norm_mix, mla_g_qa, mla_g_kva, mla_g_q,
                         mla_g_k, gqa_g_q, gqa_g_k, diff_g_q, diff_g_k, diff_g_out, na_g_q,
                         na_g_k, na_rpb)
    g_mlp = g_norm_mlp.reshape(DEPTH, 1, D_MODEL)

    xs = x.reshape(BATCH * SEQ, D_MODEL)
    cs = ctx.reshape(BATCH * CTX_LEN, D_MODEL)
    for l in range(DEPTH):
        need_ctx = l < DEPTH - 1
        lam_init = 0.8 - 0.6 * math.exp(-0.3 * l)
        m6 = mod[l].reshape(8, 6, 1, D_MODEL)
        sh_a, sc_a, gt_a, sh_m, sc_m, gt_m = [m6[:BATCH, k] for k in range(6)]
        csh_a, csc_a, cgt_a, csh_m, csc_m, cgt_m = [m6[BATCH:BATCH + 1, k] for k in range(6)]
        lam_rows = jnp.stack([diff_lq1[l], diff_lk1[l], diff_lq2[l], diff_lk2[l]]).astype(F32)
        g_out = pw["gains"][l, G_C_OUT:G_C_OUT + 1]

        if l == 0:
            w_in_parts = _w_in_prep(w_in, 0)
        fl = _front(xs, sc_a, sh_a, pw, l, w_in_parts, lat_tabs, rows_per_mod=SEQ, tm=tm_front)
        fc = _front(cs, csc_a, csh_a, pw, l, w_in_parts, ctx_tabs, rows_per_mod=BATCH * CTX_LEN,
                    tm=tm_front)
        qa, ka, va, qb, kb, vb, qc, kc, vc, qd, kd, vd = fl
        cqa, cka, cva, cqb, ckb, cvb, cqc, ckc, cvc, cqd, ckd, cvd = fc

        o_a = _attention(pw["bound_a"][l], qa, ka, va, cka, cva, kind="mla", tq=2048)
        o_b = _attention(pw["bound_b"][l], qb, kb, vb, ckb, cvb, kind="gqa", tq=1024)
        first = l == 0
        o_c = _attention(pw["bound_c"][l], qc, kc, vc, ckc, cvc, kind="diff", tq=1024,
                         extra=(lam_rows, g_out), lam_init=lam_init,
                         cast=(w_out, w_up, w_down) if first else (),
                         prep_w_in=(w_in, l + 1) if first else None)
        if first:
            o_c, w_out_b, w_up_b, w_down_b, *w_in_parts = o_c
        o_d = _na_attention(pw["bound_d"][l], qd, kd, vd, ckd, cvd, _na_bias(na_rpb, l))

        x_mid, h_mlp = _out_proj(xs, (o_a, o_b, o_c, o_d), w_out_b, l, gt_a, g_mlp, sc_m, sh_m,
                                 rows_per_mod=SEQ)
        xs = _mlp(h_mlp, x_mid, w_up_b, w_down_b, l, gt_m, rows_per_mod=SEQ)

        if need_ctx:
            oc = _ctx_attention(fc, lam_rows, g_out, lam_init)
            c_mid, ch_mlp = _out_proj(cs, oc, w_out_b, l, cgt_a, g_mlp, csc_m, csh_m,
                                      rows_per_mod=BATCH * CTX_LEN)
            cs = _mlp(ch_mlp, c_mid, w_up_b, w_down_b, l, cgt_m, rows_per_mod=BATCH * CTX_LEN)
    return xs.reshape(BATCH, SEQ, D_MODEL)
```

```python
import functools
import math

import numpy as np
import jax
import jax.numpy as jnp
from jax import lax
from jax.experimental import pallas as pl
from jax.experimental.pallas import tpu as pltpu

D_MODEL = 2048
BATCH = 4
SEQ = 4096
DEPTH = 2
GRID_W = 64
CTX_LEN = 256
HEAD_DIM = 128
GROUP_W = D_MODEL // 4
N_HEADS = GROUP_W // HEAD_DIM
D_FF = 4 * D_MODEL
ROPE_THETA = 10000.0
NORM_EPS = 1e-6

MLA_Q_RANK = GROUP_W
MLA_KV_RANK = GROUP_W // 2
MLA_NOPE = 128
MLA_ROPE = 64
GQA_KV_HEADS = N_HEADS // 2
DIFF_HALF = HEAD_DIM // 2
NA_WIN_H = 8
NA_WIN_W = 16
NA_QROWS = 8
NA_QBLOCK = NA_QROWS * GRID_W
NA_BLOCKS = SEQ // NA_QBLOCK
NA_BAND_ROWS = NA_WIN_H + NA_QROWS
NA_BAND = NA_BAND_ROWS * GRID_W
NA_PATTERNS = 3

A_COLS = MLA_Q_RANK + MLA_KV_RANK + MLA_ROPE
B_COLS = (N_HEADS + 2 * GQA_KV_HEADS) * HEAD_DIM
C_COLS = 3 * N_HEADS * HEAD_DIM
D_COLS = 3 * N_HEADS * HEAD_DIM

LANES = 128
OFF_AQ = 0
OFF_AKV = OFF_AQ + MLA_Q_RANK
OFF_KPE = OFF_AKV + MLA_KV_RANK
OFF_B = A_COLS
OFF_C = OFF_B + B_COLS
OFF_D = OFF_C + C_COLS
IN_COLS = OFF_D + D_COLS

VMEM_LIMIT = 60 * 1024 * 1024
NEG_BIG = -1e30
LOG2E = math.log2(math.e)
MAX_EXP2_SPAN = 96.0
ONES_ROWS = 16

F32 = jnp.float32
BF16 = jnp.bfloat16

G_A_QN, G_A_QR, G_A_KN, G_A_KR, G_B_Q, G_B_K, G_C_Q, G_C_K, G_D_Q, G_D_K, G_C_OUT = range(11)
GAIN_ROWS = 16


def _lane_iota(shape):
    return lax.broadcasted_iota(jnp.int32, shape, len(shape) - 1)


def _rms(t):
    return t * lax.rsqrt(jnp.mean(t * t, axis=-1, keepdims=True) + NORM_EPS)


def _rms64(t):
    lo = _lane_iota(t.shape) < 64
    sq = t * t
    s_lo = jnp.sum(jnp.where(lo, sq, 0.0), axis=-1, keepdims=True)
    s_hi = jnp.sum(jnp.where(lo, 0.0, sq), axis=-1, keepdims=True)
    ms = jnp.where(lo, s_lo, s_hi) * (1.0 / 64.0)
    return t * lax.rsqrt(ms + NORM_EPS)


def _rope(t, cos, sin_signed, half):
    first = (_lane_iota(t.shape) % (2 * half)) < half
    partner = jnp.where(first, pltpu.roll(t, LANES - half, 1), pltpu.roll(t, half, 1))
    return t * cos + partner * sin_signed


def _dot(a, b):
    return jnp.dot(a, b, preferred_element_type=F32)


def _dot_nt(a, b):
    return lax.dot_general(a, b, (((1,), (1,)), ((), ())), preferred_element_type=F32)


def _mod_kernel(c_ref, w_ref, b_ref, o_ref):
    c = c_ref[...]
    a = (c * jax.nn.sigmoid(c)).astype(BF16)
    o_ref[0] = _dot(a, w_ref[0].astype(BF16)) + b_ref[0]


def _modulation(cc, w_mod, b_mod):
    tn = 1536
    n = 6 * D_MODEL
    return pl.pallas_call(
        _mod_kernel,
        grid=(DEPTH, n // tn),
        in_specs=[
            pl.BlockSpec((8, D_MODEL), lambda l, j: (0, 0)),
            pl.BlockSpec((1, D_MODEL, tn), lambda l, j: (l, 0, j)),
            pl.BlockSpec((1, 1, tn), lambda l, j: (l, 0, j)),
        ],
        out_specs=pl.BlockSpec((1, 8, tn), lambda l, j: (l, 0, j)),
        out_shape=jax.ShapeDtypeStruct((DEPTH, 8, n), F32),
        compiler_params=pltpu.CompilerParams(
            dimension_semantics=("arbitrary", "arbitrary"), vmem_limit_bytes=VMEM_LIMIT),
        name="modulation",
    )(cc, w_mod, b_mod.reshape(DEPTH, 1, n))


def _front_kernel(x_ref, gmix_ref, sc_ref, sh_ref, wt_ref, w_uq_ref, w_ukv_ref,
                  gqa_ref, gkva_ref, gains_ref, cos128_ref, sin128_ref, cos64_ref, sin64_ref,
                  qa_ref, ka_ref, va_ref, qb_ref, kb_ref, vb_ref,
                  qc_ref, kc_ref, vc_ref, qd_ref, kd_ref, vd_ref):
    x = x_ref[...]
    gm = gmix_ref[0] * (1.0 + sc_ref[0])
    h = (_rms(x) * gm + sh_ref[0]).astype(BF16)

    gains = gains_ref[0]

    def gain(row):
        return gains[row:row + 1, :]

    cos128 = cos128_ref[...]
    sin128 = sin128_ref[...]
    cos64 = cos64_ref[...]
    sin64 = sin64_ref[...]
    lo = _lane_iota((x.shape[0], LANES)) < 64

    def proj(off, width):
        return _dot_nt(h, wt_ref[0, off:off + width, :])

    def tile(t, j):
        return t[:, j * LANES:(j + 1) * LANES]

    def store_t(ref, hd, t):
        ref[hd] = t.T.astype(BF16)

    cq = (_rms(proj(OFF_AQ, MLA_Q_RANK)) * gqa_ref[0]).astype(BF16)
    ckv = (_rms(proj(OFF_AKV, MLA_KV_RANK)) * gkva_ref[0]).astype(BF16)
    w_kpe = wt_ref[0, OFF_KPE:A_COLS, :]
    p_kpe = _dot_nt(h, jnp.concatenate([w_kpe, w_kpe], axis=0))
    kr = _rope(_rms64(p_kpe) * gain(G_A_KR), cos64, sin64, 16).astype(BF16)
    q = _dot(cq, w_uq_ref[0])
    kv = _dot(ckv, w_ukv_ref[0])
    qr_pairs = [
        _rope(_rms64(tile(q, N_HEADS + j)) * gain(G_A_QR), cos64, sin64, 16)
        for j in range(N_HEADS // 2)
    ]
    for hd in range(N_HEADS):
        qa_ref[hd, :, 0:LANES] = (_rms(tile(q, hd)) * gain(G_A_QN)).astype(BF16)
        pair = qr_pairs[hd // 2]
        keep = lo if hd % 2 == 0 else jnp.logical_not(lo)
        qa_ref[hd, :, LANES:2 * LANES] = jnp.where(keep, pair, 0.0).astype(BF16)
        ka_ref[hd, :, 0:LANES] = (_rms(tile(kv, hd)) * gain(G_A_KN)).astype(BF16)
        ka_ref[hd, :, LANES:2 * LANES] = kr
        store_t(va_ref, hd, tile(kv, N_HEADS + hd))

    pb = proj(OFF_B, B_COLS)
    for hd in range(N_HEADS):
        qb_ref[hd] = _rope(_rms(tile(pb, hd)) * gain(G_B_Q), cos128, sin128, 32).astype(BF16)
    for hd in range(GQA_KV_HEADS):
        kb_ref[hd] = _rope(_rms(tile(pb, N_HEADS + hd)) * gain(G_B_K),
                           cos128, sin128, 32).astype(BF16)
        store_t(vb_ref, hd, tile(pb, N_HEADS + GQA_KV_HEADS + hd))

    pc = proj(OFF_C, C_COLS)
    for hd in range(N_HEADS):
        qh = _rope(_rms64(tile(pc, hd)) * gain(G_C_Q), cos64, sin64, 16)
        qc_ref[2 * hd] = jnp.where(lo, qh, 0.0).astype(BF16)
        qc_ref[2 * hd + 1] = jnp.where(lo, 0.0, qh).astype(BF16)
        kc_ref[hd] = _rope(_rms64(tile(pc, N_HEADS + hd)) * gain(G_C_K),
                           cos64, sin64, 16).astype(BF16)
        store_t(vc_ref, hd, tile(pc, 2 * N_HEADS + hd))

    pd = proj(OFF_D, D_COLS)
    for hd in range(N_HEADS):
        qd_ref[hd] = (_rms(tile(pd, hd)) * gain(G_D_Q)).astype(BF16)
        kd_ref[hd] = (_rms(tile(pd, N_HEADS + hd)) * gain(G_D_K)).astype(BF16)
        store_t(vd_ref, hd, tile(pd, 2 * N_HEADS + hd))


_FRONT_OUT = (
    (N_HEADS, 2 * LANES, False), (N_HEADS, 2 * LANES, False), (N_HEADS, LANES, True),
    (N_HEADS, LANES, False), (GQA_KV_HEADS, LANES, False), (GQA_KV_HEADS, LANES, True),
    (2 * N_HEADS, LANES, False), (N_HEADS, LANES, False), (N_HEADS, LANES, True),
    (N_HEADS, LANES, False), (N_HEADS, LANES, False), (N_HEADS, LANES, True),
)


def _front(x2d, sc, sh, pw, layer, tabs, *, rows_per_mod, tm=256):
    t = x2d.shape[0]
    n_pos_tiles = tabs[0].shape[0] // tm
    tiles_per_mod = rows_per_mod // tm

    def resident(a):
        return pl.BlockSpec((1,) + a.shape[1:], lambda i: (layer,) + (0,) * (a.ndim - 1),
                            pipeline_mode=pl.Buffered(1))

    mod_spec = pl.BlockSpec((1, 1, D_MODEL), lambda i: (i // tiles_per_mod, 0, 0))
    tab_spec = pl.BlockSpec((tm, LANES), lambda i: (i % n_pos_tiles, 0))
    params = [pw["g_mix"], None, None, pw["w_in_t"], pw["w_uq"], pw["w_ukv"],
              pw["g_qa"], pw["g_kva"], pw["gains"]]
    in_specs = [pl.BlockSpec((tm, D_MODEL), lambda i: (i, 0))]
    in_specs += [mod_spec if p is None else resident(p) for p in params]
    in_specs += [tab_spec] * 4
    params[1], params[2] = sc, sh
    out_specs = [pl.BlockSpec((nh, w, tm), lambda i: (0, 0, i)) if tr
                 else pl.BlockSpec((nh, tm, w), lambda i: (0, i, 0)) for nh, w, tr in _FRONT_OUT]
    out_shape = [jax.ShapeDtypeStruct((nh, w, t) if tr else (nh, t, w), BF16)
                 for nh, w, tr in _FRONT_OUT]
    return pl.pallas_call(
        _front_kernel,
        grid=(t // tm,),
        in_specs=in_specs,
        out_specs=out_specs,
        out_shape=out_shape,
        compiler_params=pltpu.CompilerParams(
            dimension_semantics=("arbitrary",), vmem_limit_bytes=VMEM_LIMIT),
        name="front",
    )(x2d, *params, *tabs)


def _attend_t(q, chunks, shift=None):
    cols = q.shape[0]
    dv = chunks[0][1].shape[0]

    def scores(k, bias):
        st = _dot_nt(k, q)
        return st if bias is None else st + bias

    if shift is None:
        m = jnp.full((1, cols), NEG_BIG, F32)
        l = jnp.zeros((1, cols), F32)
        acc = jnp.zeros((dv, cols), F32)
        for k, vt, bias in chunks:
            st = scores(k, bias)
            m_new = jnp.maximum(m, jnp.max(st, axis=0, keepdims=True))
            alpha = jnp.exp2(m - m_new)
            p = jnp.exp2(st - m_new)
            l = alpha * l + jnp.sum(p, axis=0, keepdims=True)
            acc = alpha * acc + _dot(vt, p.astype(BF16))
            m = m_new
        return acc / l
    acc = jnp.zeros((dv + ONES_ROWS, cols), F32)
    for k, vt, bias in chunks:
        p = jnp.exp2(scores(k, bias) - shift).astype(BF16)
        vt_aug = jnp.concatenate([vt, jnp.ones((ONES_ROWS, vt.shape[1]), BF16)], axis=0)
        acc = acc + _dot(vt_aug, p)
    return acc[:dv] / acc[dv:dv + 1]


def _by_underflow_guard(bound_ref, run):
    no_underflow = bound_ref[1] <= MAX_EXP2_SPAN

    @pl.when(no_underflow)
    def _():
        run(bound_ref[0])

    @pl.when(jnp.logical_not(no_underflow))
    def _():
        run(None)


def _diff_lambda(lam_ref, lam_init):
    lam1 = jnp.exp(jnp.sum(lam_ref[0:1, :] * lam_ref[1:2, :], axis=-1, keepdims=True))
    lam2 = jnp.exp(jnp.sum(lam_ref[2:3, :] * lam_ref[3:4, :], axis=-1, keepdims=True))
    return lam1 - lam2 + lam_init


def _diff_combine(o, tq, lam, g_out, lam_init):
    d = o[:tq] - lam * o[tq:]
    return _rms(d) * (g_out * (1.0 - lam_init))


def _attn_kernel(bound_ref, q_ref, k_ref, v_ref, kc_ref, vc_ref, *rest, stacked, diff,
                 lam_init, tk, n_cast):
    if diff:
        lam_ref, gout_ref = rest[:2]
        rest = rest[2:]
    cast_in, o_ref, cast_out = rest[:n_cast], rest[n_cast], rest[n_cast + 1:]
    tq = q_ref.shape[1]
    for src, dst in zip(cast_in, cast_out):
        dst[...] = src[...].astype(dst.dtype)

    def run(shift):
        if stacked:
            q = jnp.concatenate([q_ref[0], q_ref[1]], axis=0)
        else:
            q = q_ref[0]
        chunks = [(k_ref[0, c * tk:(c + 1) * tk, :], v_ref[0, :, c * tk:(c + 1) * tk], None)
                  for c in range(SEQ // tk)]
        chunks.append((kc_ref[0], vc_ref[0], None))
        o = _attend_t(q, chunks, shift).T
        if diff:
            lam = _diff_lambda(lam_ref, lam_init)
            o_ref[0] = _diff_combine(o, tq, lam, gout_ref[...], lam_init).astype(o_ref.dtype)
        elif stacked:
            o_ref[0] = o[:tq].astype(o_ref.dtype)
            o_ref[1] = o[tq:].astype(o_ref.dtype)
        else:
            o_ref[0] = o.astype(o_ref.dtype)

    _by_underflow_guard(bound_ref, run)


def _attention(bound, q, k, v, kc, vc, *, kind, extra=(), cast=(), lam_init=0.0, tq=512,
               tk=1024):
    nq = SEQ // tq
    dk = q.shape[-1]
    stacked = kind in ("gqa", "diff")
    n_outer = GQA_KV_HEADS if kind == "gqa" else N_HEADS
    q_heads = 2 if stacked else 1
    o_heads = 2 if kind == "gqa" else 1
    n_steps = BATCH * n_outer * nq
    cast2d = [w.reshape(-1, w.shape[-1]) for w in cast]
    cast_specs = [pl.BlockSpec((w.shape[0] // n_steps, w.shape[1]),
                               lambda b, h, i: ((b * n_outer + h) * nq + i, 0)) for w in cast2d]
    in_specs = [
        pl.BlockSpec(memory_space=pltpu.SMEM),
        pl.BlockSpec((q_heads, tq, dk), lambda b, h, i: (h, b * nq + i, 0)),
        pl.BlockSpec((1, SEQ, dk), lambda b, h, i: (h, b, 0)),
        pl.BlockSpec((1, LANES, SEQ), lambda b, h, i: (h, 0, b)),
        pl.BlockSpec((1, CTX_LEN, dk), lambda b, h, i: (h, b, 0)),
        pl.BlockSpec((1, LANES, CTX_LEN), lambda b, h, i: (h, 0, b)),
    ]
    for e in extra:
        in_specs.append(pl.BlockSpec(e.shape, lambda b, h, i, nd=e.ndim: (0,) * nd))
    outs = pl.pallas_call(
        functools.partial(_attn_kernel, stacked=stacked, diff=(kind == "diff"),
                          lam_init=lam_init, tk=tk, n_cast=len(cast)),
        grid=(BATCH, n_outer, nq),
        in_specs=in_specs + cast_specs,
        out_specs=[pl.BlockSpec((o_heads, tq, LANES), lambda b, h, i: (h, b * nq + i, 0))]
        + cast_specs,
        out_shape=[jax.ShapeDtypeStruct((N_HEADS, BATCH * SEQ, LANES), BF16)]
        + [jax.ShapeDtypeStruct(w.shape, BF16) for w in cast2d],
        compiler_params=pltpu.CompilerParams(
            dimension_semantics=("arbitrary", "arbitrary", "arbitrary"),
            vmem_limit_bytes=VMEM_LIMIT),
        name="attn_" + kind,
    )(bound, q, k, v, kc, vc, *extra, *cast2d)
    if not cast:
        return outs[0]
    return [outs[0]] + [o.reshape(w.shape) for o, w in zip(outs[1:], cast)]


def _na_geometry(i):
    rows = SEQ // GRID_W
    r0 = i * NA_QROWS
    band_start = min(max(r0 - NA_WIN_H // 2, 0), rows - NA_BAND_ROWS)
    rel = np.zeros((NA_QROWS, NA_BAND_ROWS), np.int64)
    valid = np.zeros((NA_QROWS, NA_BAND_ROWS), bool)
    for a in range(NA_QROWS):
        q_row = r0 + a
        row_start = min(max(q_row - NA_WIN_H // 2, 0), rows - NA_WIN_H)
        assert band_start <= row_start and row_start + NA_WIN_H <= band_start + NA_BAND_ROWS
        for j in range(NA_BAND_ROWS):
            k_row = band_start + j
            valid[a, j] = row_start <= k_row < row_start + NA_WIN_H
            rel[a, j] = min(max(k_row - q_row + NA_WIN_H - 1, 0), 2 * NA_WIN_H - 2)
    return band_start, rel, valid


def _na_pattern_of(i):
    return 0 if i == 0 else (NA_PATTERNS - 1 if i == NA_BLOCKS - 1 else 1)


_NA_REPRESENTATIVE = (0, 1, NA_BLOCKS - 1)
for _i in range(NA_BLOCKS):
    _g, _r = _na_geometry(_i), _na_geometry(_NA_REPRESENTATIVE[_na_pattern_of(_i)])
    assert (_g[1] == _r[1]).all() and (_g[2] == _r[2]).all()


def _na_bias_kernel(rpb_ref, o_ref):
    hd = pl.program_id(0)
    n_r = 2 * NA_WIN_H - 1
    n_c = 2 * NA_WIN_W - 1
    k_col = lax.broadcasted_iota(jnp.int32, (GRID_W, LANES), 0)
    lane = lax.broadcasted_iota(jnp.int32, (GRID_W, LANES), 1)
    q_col = lane % GRID_W
    left = lane < GRID_W
    col_idx = jnp.clip(k_col - q_col + NA_WIN_W - 1, 0, n_c - 1)
    col_start = jnp.clip(q_col - NA_WIN_W // 2, 0, GRID_W - NA_WIN_W)
    col_ok = (k_col >= col_start) & (k_col < col_start + NA_WIN_W)
    tables = []
    for r in range(n_r):
        t = jnp.zeros((GRID_W, LANES), F32)
        for c in range(n_c):
            t = jnp.where(col_idx == c, rpb_ref[hd * (n_r * n_c) + r * n_c + c] * LOG2E, t)
        tables.append(jnp.where(col_ok, t, NEG_BIG))
    masked = jnp.full((GRID_W, LANES), NEG_BIG, F32)
    for p, i in enumerate(_NA_REPRESENTATIVE):
        _, rel, valid = _na_geometry(i)
        for j in range(NA_BAND_ROWS):
            for a in range(0, NA_QROWS, 2):
                t0 = tables[int(rel[a, j])] if valid[a, j] else masked
                t1 = tables[int(rel[a + 1, j])] if valid[a + 1, j] else masked
                blk = t0 if t0 is t1 else jnp.where(left, t0, t1)
                o_ref[0, p, j * GRID_W:(j + 1) * GRID_W, a * GRID_W:(a + 2) * GRID_W] = blk


def _na_bias(rpb, layer):
    n = N_HEADS * (2 * NA_WIN_H - 1) * (2 * NA_WIN_W - 1)
    return pl.pallas_call(
        _na_bias_kernel,
        grid=(N_HEADS,),
        in_specs=[pl.BlockSpec(memory_space=pltpu.SMEM)],
        out_specs=pl.BlockSpec((1, NA_PATTERNS, NA_BAND, NA_QBLOCK), lambda h: (h, 0, 0, 0)),
        out_shape=jax.ShapeDtypeStruct((N_HEADS, NA_PATTERNS, NA_BAND, NA_QBLOCK), F32),
        compiler_params=pltpu.CompilerParams(
            dimension_semantics=("arbitrary",), vmem_limit_bytes=VMEM_LIMIT),
        name="na_bias",
    )(rpb.reshape(DEPTH, n)[layer])


def _na_kernel(bound_ref, q_ref, k_ref, vt_ref, kc_ref, vct_ref, bias_ref, o_ref):
    def run(shift):
        kc = kc_ref[0]
        vct = vct_ref[0]
        for i in range(NA_BLOCKS):
            k0 = _na_geometry(i)[0] * GRID_W
            rows = slice(i * NA_QBLOCK, (i + 1) * NA_QBLOCK)
            chunks = [(k_ref[0, k0:k0 + NA_BAND, :], vt_ref[0, :, k0:k0 + NA_BAND],
                       bias_ref[0, _na_pattern_of(i)]),
                      (kc, vct, None)]
            o_ref[0, rows, :] = _attend_t(q_ref[0, rows, :], chunks, shift).T.astype(o_ref.dtype)

    _by_underflow_guard(bound_ref, run)


def _na_attention(bound, q, k, vt, kc, vct, bias):
    return pl.pallas_call(
        _na_kernel,
        grid=(BATCH, N_HEADS),
        in_specs=[
            pl.BlockSpec(memory_space=pltpu.SMEM),
            pl.BlockSpec((1, SEQ, LANES), lambda b, h: (h, b, 0)),
            pl.BlockSpec((1, SEQ, LANES), lambda b, h: (h, b, 0)),
            pl.BlockSpec((1, LANES, SEQ), lambda b, h: (h, 0, b)),
            pl.BlockSpec((1, CTX_LEN, LANES), lambda b, h: (h, b, 0)),
            pl.BlockSpec((1, LANES, CTX_LEN), lambda b, h: (h, 0, b)),
            pl.BlockSpec((1, NA_PATTERNS, NA_BAND, NA_QBLOCK), lambda b, h: (h, 0, 0, 0)),
        ],
        out_specs=pl.BlockSpec((1, SEQ, LANES), lambda b, h: (h, b, 0)),
        out_shape=jax.ShapeDtypeStruct((N_HEADS, BATCH * SEQ, LANES), BF16),
        compiler_params=pltpu.CompilerParams(
            dimension_semantics=("arbitrary", "arbitrary"), vmem_limit_bytes=VMEM_LIMIT),
        name="attn_na",
    )(bound, q, k, vt, kc, vct, bias)


def _ctx_attn_kernel(qa_ref, ka_ref, va_ref, qb_ref, kb_ref, vb_ref, qc_ref, kc_ref, vc_ref,
                     qd_ref, kd_ref, vd_ref, lam_ref, gout_ref,
                     oa_ref, ob_ref, oc_ref, od_ref, *, lam_init):
    n = CTX_LEN
    lam = _diff_lambda(lam_ref, lam_init)

    def attend(q, k_ref, vt_ref, hd):
        return _attend_t(q, [(k_ref[hd], vt_ref[hd], None)]).T

    for hd in range(N_HEADS):
        oa_ref[hd] = attend(qa_ref[hd], ka_ref, va_ref, hd).astype(BF16)
        kvh = hd // (N_HEADS // GQA_KV_HEADS)
        ob_ref[hd] = attend(qb_ref[hd], kb_ref, vb_ref, kvh).astype(BF16)
        q2 = jnp.concatenate([qc_ref[2 * hd], qc_ref[2 * hd + 1]], axis=0)
        o2 = attend(q2, kc_ref, vc_ref, hd)
        oc_ref[hd] = _diff_combine(o2, n, lam, gout_ref[...], lam_init).astype(BF16)
        od_ref[hd] = attend(qd_ref[hd], kd_ref, vd_ref, hd).astype(BF16)


def _ctx_attention(fc, lam_rows, g_out, lam_init):
    in_specs = [pl.BlockSpec((nh, w, CTX_LEN), lambda b: (0, 0, b)) if tr
                else pl.BlockSpec((nh, CTX_LEN, w), lambda b: (0, b, 0))
                for nh, w, tr in _FRONT_OUT]
    in_specs += [pl.BlockSpec(lam_rows.shape, lambda b: (0, 0)),
                 pl.BlockSpec(g_out.shape, lambda b: (0, 0))]
    o_spec = pl.BlockSpec((N_HEADS, CTX_LEN, LANES), lambda b: (0, b, 0))
    o_shape = jax.ShapeDtypeStruct((N_HEADS, BATCH * CTX_LEN, LANES), BF16)
    return pl.pallas_call(
        functools.partial(_ctx_attn_kernel, lam_init=lam_init),
        grid=(BATCH,),
        in_specs=in_specs,
        out_specs=[o_spec] * 4,
        out_shape=[o_shape] * 4,
        compiler_params=pltpu.CompilerParams(
            dimension_semantics=("arbitrary",), vmem_limit_bytes=VMEM_LIMIT),
        name="attn_ctx",
    )(*fc, lam_rows, g_out)


def _out_kernel(x_ref, oa_ref, ob_ref, oc_ref, od_ref, w_ref, gt_ref, g_ref, sc_ref, sh_ref,
                xo_ref, h_ref):
    parts = [r[hd] for r in (oa_ref, ob_ref, oc_ref, od_ref) for hd in range(N_HEADS)]
    o = jnp.concatenate(parts, axis=-1)
    x = x_ref[...] + gt_ref[0] * _dot(o, w_ref[0])
    xo_ref[...] = x
    h_ref[...] = (_rms(x) * (g_ref[0] * (1.0 + sc_ref[0])) + sh_ref[0]).astype(BF16)


def _out_proj(x2d, outs, w_out, layer, gt, g_mlp, sc, sh, *, rows_per_mod, tm=512):
    t = x2d.shape[0]
    tiles_per_mod = rows_per_mod // tm
    mod_spec = pl.BlockSpec((1, 1, D_MODEL), lambda i: (i // tiles_per_mod, 0, 0))
    o_spec = pl.BlockSpec((N_HEADS, tm, LANES), lambda i: (0, i, 0))
    row_spec = pl.BlockSpec((tm, D_MODEL), lambda i: (i, 0))
    return pl.pallas_call(
        _out_kernel,
        grid=(t // tm,),
        in_specs=[row_spec, o_spec, o_spec, o_spec, o_spec,
                  pl.BlockSpec((1, D_MODEL, D_MODEL), lambda i: (layer, 0, 0),
                               pipeline_mode=pl.Buffered(1)),
                  mod_spec, pl.BlockSpec((1, 1, D_MODEL), lambda i: (layer, 0, 0)),
                  mod_spec, mod_spec],
        out_specs=[row_spec, row_spec],
        out_shape=[jax.ShapeDtypeStruct((t, D_MODEL), F32),
                   jax.ShapeDtypeStruct((t, D_MODEL), BF16)],
        compiler_params=pltpu.CompilerParams(
            dimension_semantics=("arbitrary",), vmem_limit_bytes=VMEM_LIMIT),
        name="out_proj",
    )(x2d, *outs, w_out, gt, g_mlp, sc, sh)


def _mlp_kernel(h_ref, wu_ref, wd_ref, x_ref, gt_ref, o_ref, xs_ref):
    j = pl.program_id(1)
    last = pl.num_programs(1) - 1
    slab = x_ref.shape[0]
    xs_ref[pl.ds(pl.multiple_of(j * slab, slab), slab), :] = x_ref[...]

    def step(is_first, is_last):
        u = jnp.maximum(_dot(h_ref[...], wu_ref[0]), 0.0)
        y = _dot((u * u).astype(BF16), wd_ref[0])
        acc = y if is_first else o_ref[...] + y
        o_ref[...] = xs_ref[...] + gt_ref[0] * acc if is_last else acc

    pl.when(j == 0)(lambda: step(True, False))
    pl.when(jnp.logical_and(j > 0, j < last))(lambda: step(False, False))
    pl.when(j == last)(lambda: step(False, True))


def _mlp(h2d, x2d, w_up, w_down, layer, gt, *, rows_per_mod, tm=1024, tf=1024):
    t = x2d.shape[0]
    tiles_per_mod = rows_per_mod // tm
    nj = D_FF // tf
    slab = tm // nj
    return pl.pallas_call(
        _mlp_kernel,
        grid=(t // tm, nj),
        in_specs=[
            pl.BlockSpec((tm, D_MODEL), lambda i, j: (i, 0)),
            pl.BlockSpec((1, D_MODEL, tf), lambda i, j: (layer, 0, j)),
            pl.BlockSpec((1, tf, D_MODEL), lambda i, j: (layer, j, 0)),
            pl.BlockSpec((slab, D_MODEL), lambda i, j: (i * nj + j, 0)),
            pl.BlockSpec((1, 1, D_MODEL), lambda i, j: (i // tiles_per_mod, 0, 0)),
        ],
        out_specs=pl.BlockSpec((tm, D_MODEL), lambda i, j: (i, 0)),
        out_shape=jax.ShapeDtypeStruct((t, D_MODEL), F32),
        scratch_shapes=[pltpu.VMEM((tm, D_MODEL), F32)],
        compiler_params=pltpu.CompilerParams(
            dimension_semantics=("arbitrary", "arbitrary"), vmem_limit_bytes=VMEM_LIMIT),
        name="mlp",
    )(h2d, w_up, w_down, x2d, gt)


def _rope_tables(tm):
    t = jnp.arange(SEQ)
    row = (t // GRID_W).astype(F32)
    col = (t % GRID_W).astype(F32)

    def tables(rot_dim):
        half = rot_dim // 2
        inv_freq = ROPE_THETA ** (-jnp.arange(0, half, 2, dtype=F32) / half)
        ar = row[:, None] * inv_freq
        ac = col[:, None] * inv_freq
        cos = jnp.concatenate([jnp.cos(ar), jnp.cos(ar), jnp.cos(ac), jnp.cos(ac)], axis=-1)
        sin = jnp.concatenate([-jnp.sin(ar), jnp.sin(ar), -jnp.sin(ac), jnp.sin(ac)], axis=-1)
        return cos, sin

    cos128, sin128 = tables(HEAD_DIM)
    cos64, sin64 = tables(DIFF_HALF)
    lat = (cos128, sin128, jnp.tile(cos64, (1, 2)), jnp.tile(sin64, (1, 2)))
    ones = jnp.ones((tm, LANES), F32)
    zeros = jnp.zeros((tm, LANES), F32)
    return lat, (ones, zeros, ones, zeros)


def _prepare_params(w_in, mla_w_uq, mla_w_ukv, g_norm_mix, mla_g_qa, mla_g_kva, mla_g_q, mla_g_k,
                    gqa_g_q, gqa_g_k, diff_g_q, diff_g_k, diff_g_out, na_g_q, na_g_k, na_rpb):
    w_in_t = jnp.swapaxes(w_in, 1, 2).astype(BF16)
    uq = mla_w_uq.reshape(DEPTH, MLA_Q_RANK, N_HEADS, MLA_NOPE + MLA_ROPE)
    w_uq = jnp.concatenate(
        [uq[..., :MLA_NOPE].reshape(DEPTH, MLA_Q_RANK, -1),
         uq[..., MLA_NOPE:].reshape(DEPTH, MLA_Q_RANK, -1)], axis=2).astype(BF16)
    ukv = mla_w_ukv.reshape(DEPTH, MLA_KV_RANK, N_HEADS, MLA_NOPE + HEAD_DIM)
    w_ukv = jnp.concatenate(
        [ukv[..., :MLA_NOPE].reshape(DEPTH, MLA_KV_RANK, -1),
         ukv[..., MLA_NOPE:].reshape(DEPTH, MLA_KV_RANK, -1)], axis=2).astype(BF16)

    scale_a = (MLA_NOPE + MLA_ROPE) ** -0.5 * LOG2E
    scale_b = HEAD_DIM ** -0.5 * LOG2E
    scale_c = DIFF_HALF ** -0.5 * LOG2E
    scale_d = HEAD_DIM ** -0.5 * LOG2E
    two = lambda g: jnp.tile(g, (1, 2))
    rows = [None] * 11
    rows[G_A_QN] = mla_g_q[:, :MLA_NOPE] * scale_a
    rows[G_A_QR] = two(mla_g_q[:, MLA_NOPE:]) * scale_a
    rows[G_A_KN] = mla_g_k[:, :MLA_NOPE]
    rows[G_A_KR] = two(mla_g_k[:, MLA_NOPE:])
    rows[G_B_Q] = gqa_g_q * scale_b
    rows[G_B_K] = gqa_g_k
    rows[G_C_Q] = two(diff_g_q) * scale_c
    rows[G_C_K] = two(diff_g_k)
    rows[G_D_Q] = na_g_q * scale_d
    rows[G_D_K] = na_g_k
    rows[G_C_OUT] = diff_g_out
    pad = [jnp.zeros((DEPTH, LANES), F32)] * (GAIN_ROWS - len(rows))
    gains = jnp.stack(rows + pad, axis=1).astype(F32)

    amax = lambda r: jnp.max(jnp.abs(rows[r]), axis=1)
    slack = 1.02
    b_a = slack * (jnp.sqrt(MLA_NOPE * amax(G_A_QN) ** 2 + MLA_ROPE * amax(G_A_QR) ** 2)
                   * jnp.sqrt(MLA_NOPE * amax(G_A_KN) ** 2 + MLA_ROPE * amax(G_A_KR) ** 2))
    b_b = slack * HEAD_DIM * amax(G_B_Q) * amax(G_B_K)
    b_c = slack * DIFF_HALF * amax(G_C_Q) * amax(G_C_K)
    b_d = slack * HEAD_DIM * amax(G_D_Q) * amax(G_D_K)
    bias_hi = jnp.maximum(jnp.max(na_rpb, axis=(1, 2, 3)), 0.0) * LOG2E
    bias_lo = jnp.minimum(jnp.min(na_rpb, axis=(1, 2, 3)), 0.0) * LOG2E
    span = lambda b: jnp.stack([b, 2.0 * b], axis=1).astype(F32)
    return {
        "bound_a": span(b_a), "bound_b": span(b_b), "bound_c": span(b_c),
        "bound_d": jnp.stack([b_d + bias_hi, 2.0 * b_d + bias_hi - bias_lo], axis=1).astype(F32),
        "w_in_t": w_in_t, "w_uq": w_uq, "w_ukv": w_ukv,
        "g_mix": g_norm_mix.reshape(DEPTH, 1, D_MODEL),
        "g_qa": mla_g_qa.reshape(DEPTH, 1, -1), "g_kva": mla_g_kva.reshape(DEPTH, 1, -1),
        "gains": gains,
    }


def kernel(x, c, ctx, c_ctx, w_mod, b_mod, g_norm_mix, g_norm_mlp, w_in, mla_g_qa, mla_g_kva,
           mla_w_uq, mla_w_ukv, mla_g_q, mla_g_k, gqa_g_q, gqa_g_k, diff_g_q, diff_g_k,
           diff_lq1, diff_lk1, diff_lq2, diff_lk2, diff_g_out, na_g_q, na_g_k, na_rpb,
           w_out, w_up, w_down):
    tm_front = 512
    cc = jnp.concatenate([c, c_ctx[None, :], jnp.zeros((8 - BATCH - 1, D_MODEL), F32)], axis=0)
    mod = _modulation(cc, w_mod, b_mod)
    lat_tabs, ctx_tabs = _rope_tables(tm_front)
    pw = _prepare_params(w_in, mla_w_uq, mla_w_ukv, g_norm_mix, mla_g_qa, mla_g_kva, mla_g_q,
                         mla_g_k, gqa_g_q, gqa_g_k, diff_g_q, diff_g_k, diff_g_out, na_g_q,
                         na_g_k, na_rpb)
    g_mlp = g_norm_mlp.reshape(DEPTH, 1, D_MODEL)

    xs = x.reshape(BATCH * SEQ, D_MODEL)
    cs = ctx.reshape(BATCH * CTX_LEN, D_MODEL)
    for l in range(DEPTH):
        need_ctx = l < DEPTH - 1
        lam_init = 0.8 - 0.6 * math.exp(-0.3 * l)
        m6 = mod[l].reshape(8, 6, 1, D_MODEL)
        sh_a, sc_a, gt_a, sh_m, sc_m, gt_m = [m6[:BATCH, k] for k in range(6)]
        csh_a, csc_a, cgt_a, csh_m, csc_m, cgt_m = [m6[BATCH:BATCH + 1, k] for k in range(6)]
        lam_rows = jnp.stack([diff_lq1[l], diff_lk1[l], diff_lq2[l], diff_lk2[l]]).astype(F32)
        g_out = pw["gains"][l, G_C_OUT:G_C_OUT + 1]

        fl = _front(xs, sc_a, sh_a, pw, l, lat_tabs, rows_per_mod=SEQ, tm=tm_front)
        fc = _front(cs, csc_a, csh_a, pw, l, ctx_tabs, rows_per_mod=BATCH * CTX_LEN, tm=tm_front)
        qa, ka, va, qb, kb, vb, qc, kc, vc, qd, kd, vd = fl
        cqa, cka, cva, cqb, ckb, cvb, cqc, ckc, cvc, cqd, ckd, cvd = fc

        o_a = _attention(pw["bound_a"][l], qa, ka, va, cka, cva, kind="mla", tq=2048)
        o_b = _attention(pw["bound_b"][l], qb, kb, vb, ckb, cvb, kind="gqa", tq=1024)
        o_c = _attention(pw["bound_c"][l], qc, kc, vc, ckc, cvc, kind="diff", tq=1024,
                         extra=(lam_rows, g_out), lam_init=lam_init,
                         cast=(w_out, w_up, w_down) if l == 0 else ())
        if l == 0:
            o_c, w_out_b, w_up_b, w_down_b = o_c
        o_d = _na_attention(pw["bound_d"][l], qd, kd, vd, ckd, cvd, _na_bias(na_rpb, l))

        x_mid, h_mlp = _out_proj(xs, (o_a, o_b, o_c, o_d), w_out_b, l, gt_a, g_mlp, sc_m, sh_m,
                                 rows_per_mod=SEQ)
        xs = _mlp(h_mlp, x_mid, w_up_b, w_down_b, l, gt_m, rows_per_mod=SEQ)

        if need_ctx:
            oc = _ctx_attention(fc, lam_rows, g_out, lam_init)
            c_mid, ch_mlp = _out_proj(cs, oc, w_out_b, l, cgt_a, g_mlp, csc_m, csh_m,
                                      rows_per_mod=BATCH * CTX_LEN)
            cs = _mlp(ch_mlp, c_mid, w_up_b, w_down_b, l, cgt_m, rows_per_mod=BATCH * CTX_LEN)
    return xs.reshape(BATCH, SEQ, D_MODEL)
```

```python
import functools
import math

import numpy as np
import jax
import jax.numpy as jnp
from jax import lax
from jax.experimental import pallas as pl
from jax.experimental.pallas import tpu as pltpu

D_MODEL = 2048
BATCH = 4
SEQ = 4096
DEPTH = 2
GRID_W = 64
CTX_LEN = 256
HEAD_DIM = 128
GROUP_W = D_MODEL // 4
N_HEADS = GROUP_W // HEAD_DIM
D_FF = 4 * D_MODEL
ROPE_THETA = 10000.0
NORM_EPS = 1e-6

MLA_Q_RANK = GROUP_W
MLA_KV_RANK = GROUP_W // 2
MLA_NOPE = 128
MLA_ROPE = 64
GQA_KV_HEADS = N_HEADS // 2
DIFF_HALF = HEAD_DIM // 2
NA_WIN_H = 8
NA_WIN_W = 16
NA_QROWS = 8
NA_QBLOCK = NA_QROWS * GRID_W
NA_BLOCKS = SEQ // NA_QBLOCK
NA_BAND_ROWS = NA_WIN_H + NA_QROWS
NA_BAND = NA_BAND_ROWS * GRID_W
NA_PATTERNS = 3

A_COLS = MLA_Q_RANK + MLA_KV_RANK + MLA_ROPE
B_COLS = (N_HEADS + 2 * GQA_KV_HEADS) * HEAD_DIM
C_COLS = 3 * N_HEADS * HEAD_DIM
D_COLS = 3 * N_HEADS * HEAD_DIM

LANES = 128
OFF_AQ = 0
OFF_AKV = OFF_AQ + MLA_Q_RANK
OFF_KPE = OFF_AKV + MLA_KV_RANK
OFF_B = A_COLS
OFF_C = OFF_B + B_COLS
OFF_D = OFF_C + C_COLS

VMEM_LIMIT = 60 * 1024 * 1024
NEG_BIG = -1e30
LOG2E = math.log2(math.e)
MAX_EXP2_SPAN = 96.0
ONES_ROWS = 16

F32 = jnp.float32
BF16 = jnp.bfloat16

G_A_QN, G_A_QR, G_A_KN, G_A_KR, G_B_Q, G_B_K, G_C_Q, G_C_K, G_D_Q, G_D_K, G_C_OUT = range(11)
GAIN_ROWS = 16


def _lane_iota(shape):
    return lax.broadcasted_iota(jnp.int32, shape, len(shape) - 1)


def _rms(t):
    return t * lax.rsqrt(jnp.mean(t * t, axis=-1, keepdims=True) + NORM_EPS)


def _rms64(t):
    lo = _lane_iota(t.shape) < 64
    sq = t * t
    s_lo = jnp.sum(jnp.where(lo, sq, 0.0), axis=-1, keepdims=True)
    s_hi = jnp.sum(jnp.where(lo, 0.0, sq), axis=-1, keepdims=True)
    ms = jnp.where(lo, s_lo, s_hi) * (1.0 / 64.0)
    return t * lax.rsqrt(ms + NORM_EPS)


def _rope(t, cos, sin_signed, half):
    first = (_lane_iota(t.shape) % (2 * half)) < half
    partner = jnp.where(first, pltpu.roll(t, LANES - half, 1), pltpu.roll(t, half, 1))
    return t * cos + partner * sin_signed


def _dot(a, b):
    return jnp.dot(a, b, preferred_element_type=F32)


def _dot_nt(a, b):
    return lax.dot_general(a, b, (((1,), (1,)), ((), ())), preferred_element_type=F32)


def _mod_kernel(c_ref, w_ref, b_ref, o_ref):
    c = c_ref[...]
    a = (c * jax.nn.sigmoid(c)).astype(BF16)
    o_ref[0] = _dot(a, w_ref[0].astype(BF16)) + b_ref[0]


def _modulation(cc, w_mod, b_mod):
    tn = 1536
    n = 6 * D_MODEL
    return pl.pallas_call(
        _mod_kernel,
        grid=(DEPTH, n // tn),
        in_specs=[
            pl.BlockSpec((8, D_MODEL), lambda l, j: (0, 0)),
            pl.BlockSpec((1, D_MODEL, tn), lambda l, j: (l, 0, j)),
            pl.BlockSpec((1, 1, tn), lambda l, j: (l, 0, j)),
        ],
        out_specs=pl.BlockSpec((1, 8, tn), lambda l, j: (l, 0, j)),
        out_shape=jax.ShapeDtypeStruct((DEPTH, 8, n), F32),
        compiler_params=pltpu.CompilerParams(
            dimension_semantics=("arbitrary", "arbitrary"), vmem_limit_bytes=VMEM_LIMIT),
        name="modulation",
    )(cc, w_mod, b_mod.reshape(DEPTH, 1, n))


def _front_kernel(x_ref, gmix_ref, sc_ref, sh_ref, wt_ref, w_uq_ref, w_ukv_ref,
                  gqa_ref, gkva_ref, gains_ref, cos128_ref, sin128_ref, cos64_ref, sin64_ref,
                  qa_ref, ka_ref, va_ref, qb_ref, kb_ref, vb_ref,
                  qc_ref, kc_ref, vc_ref, qd_ref, kd_ref, vd_ref):
    x = x_ref[...]
    gm = gmix_ref[0] * (1.0 + sc_ref[0])
    h = (_rms(x) * gm + sh_ref[0]).astype(BF16)

    gains = gains_ref[0]

    def gain(row):
        return gains[row:row + 1, :]

    cos128 = cos128_ref[...]
    sin128 = sin128_ref[...]
    cos64 = cos64_ref[...]
    sin64 = sin64_ref[...]
    lo = _lane_iota((x.shape[0], LANES)) < 64

    def proj(off, width):
        return _dot_nt(h, wt_ref[0, off:off + width, :])

    def tile(t, j):
        return t[:, j * LANES:(j + 1) * LANES]

    def store_t(ref, hd, t):
        ref[hd] = t.T.astype(BF16)

    cq = (_rms(proj(OFF_AQ, MLA_Q_RANK)) * gqa_ref[0]).astype(BF16)
    ckv = (_rms(proj(OFF_AKV, MLA_KV_RANK)) * gkva_ref[0]).astype(BF16)
    w_kpe = wt_ref[0, OFF_KPE:A_COLS, :]
    p_kpe = _dot_nt(h, jnp.concatenate([w_kpe, w_kpe], axis=0))
    kr = _rope(_rms64(p_kpe) * gain(G_A_KR), cos64, sin64, 16).astype(BF16)
    q = _dot(cq, w_uq_ref[0])
    kv = _dot(ckv, w_ukv_ref[0])
    qr_pairs = [
        _rope(_rms64(tile(q, N_HEADS + j)) * gain(G_A_QR), cos64, sin64, 16)
        for j in range(N_HEADS // 2)
    ]
    for hd in range(N_HEADS):
        qa_ref[hd, :, 0:LANES] = (_rms(tile(q, hd)) * gain(G_A_QN)).astype(BF16)
        pair = qr_pairs[hd // 2]
        keep = lo if hd % 2 == 0 else jnp.logical_not(lo)
        qa_ref[hd, :, LANES:2 * LANES] = jnp.where(keep, pair, 0.0).astype(BF16)
        ka_ref[hd, :, 0:LANES] = (_rms(tile(kv, hd)) * gain(G_A_KN)).astype(BF16)
        ka_ref[hd, :, LANES:2 * LANES] = kr
        store_t(va_ref, hd, tile(kv, N_HEADS + hd))

    pb = proj(OFF_B, B_COLS)
    for hd in range(N_HEADS):
        qb_ref[hd] = _rope(_rms(tile(pb, hd)) * gain(G_B_Q), cos128, sin128, 32).astype(BF16)
    for hd in range(GQA_KV_HEADS):
        kb_ref[hd] = _rope(_rms(tile(pb, N_HEADS + hd)) * gain(G_B_K),
                           cos128, sin128, 32).astype(BF16)
        store_t(vb_ref, hd, tile(pb, N_HEADS + GQA_KV_HEADS + hd))

    pc = proj(OFF_C, C_COLS)
    for hd in range(N_HEADS):
        qh = _rope(_rms64(tile(pc, hd)) * gain(G_C_Q), cos64, sin64, 16)
        qc_ref[2 * hd] = jnp.where(lo, qh, 0.0).astype(BF16)
        qc_ref[2 * hd + 1] = jnp.where(lo, 0.0, qh).astype(BF16)
        kc_ref[hd] = _rope(_rms64(tile(pc, N_HEADS + hd)) * gain(G_C_K),
                           cos64, sin64, 16).astype(BF16)
        store_t(vc_ref, hd, tile(pc, 2 * N_HEADS + hd))

    pd = proj(OFF_D, D_COLS)
    for hd in range(N_HEADS):
        qd_ref[hd] = (_rms(tile(pd, hd)) * gain(G_D_Q)).astype(BF16)
        kd_ref[hd] = (_rms(tile(pd, N_HEADS + hd)) * gain(G_D_K)).astype(BF16)
        store_t(vd_ref, hd, tile(pd, 2 * N_HEADS + hd))


_FRONT_OUT = (
    (N_HEADS, 2 * LANES, False), (N_HEADS, 2 * LANES, False), (N_HEADS, LANES, True),
    (N_HEADS, LANES, False), (GQA_KV_HEADS, LANES, False), (GQA_KV_HEADS, LANES, True),
    (2 * N_HEADS, LANES, False), (N_HEADS, LANES, False), (N_HEADS, LANES, True),
    (N_HEADS, LANES, False), (N_HEADS, LANES, False), (N_HEADS, LANES, True),
)


def _front(x2d, sc, sh, pw, layer, tabs, *, rows_per_mod, tm=256):
    t = x2d.shape[0]
    n_pos_tiles = tabs[0].shape[0] // tm
    tiles_per_mod = rows_per_mod // tm

    def resident(a):
        return pl.BlockSpec((1,) + a.shape[1:], lambda i: (layer,) + (0,) * (a.ndim - 1),
                            pipeline_mode=pl.Buffered(1))

    mod_spec = pl.BlockSpec((1, 1, D_MODEL), lambda i: (i // tiles_per_mod, 0, 0))
    tab_spec = pl.BlockSpec((tm, LANES), lambda i: (i % n_pos_tiles, 0))
    params = [pw["g_mix"], None, None, pw["w_in_t"], pw["w_uq"], pw["w_ukv"],
              pw["g_qa"], pw["g_kva"], pw["gains"]]
    in_specs = [pl.BlockSpec((tm, D_MODEL), lambda i: (i, 0))]
    in_specs += [mod_spec if p is None else resident(p) for p in params]
    in_specs += [tab_spec] * 4
    params[1], params[2] = sc, sh
    out_specs = [pl.BlockSpec((nh, w, tm), lambda i: (0, 0, i)) if tr
                 else pl.BlockSpec((nh, tm, w), lambda i: (0, i, 0)) for nh, w, tr in _FRONT_OUT]
    out_shape = [jax.ShapeDtypeStruct((nh, w, t) if tr else (nh, t, w), BF16)
                 for nh, w, tr in _FRONT_OUT]
    return pl.pallas_call(
        _front_kernel,
        grid=(t // tm,),
        in_specs=in_specs,
        out_specs=out_specs,
        out_shape=out_shape,
        compiler_params=pltpu.CompilerParams(
            dimension_semantics=("arbitrary",), vmem_limit_bytes=VMEM_LIMIT),
        name="front",
    )(x2d, *params, *tabs)


def _attend_t(q, chunks, shift=None):
    cols = q.shape[0]
    dv = chunks[0][1].shape[0]

    def scores(k, bias):
        st = _dot_nt(k, q)
        return st if bias is None else st + bias

    if shift is None:
        m = jnp.full((1, cols), NEG_BIG, F32)
        l = jnp.zeros((1, cols), F32)
        acc = jnp.zeros((dv, cols), F32)
        for k, vt, bias in chunks:
            st = scores(k, bias)
            m_new = jnp.maximum(m, jnp.max(st, axis=0, keepdims=True))
            alpha = jnp.exp2(m - m_new)
            p = jnp.exp2(st - m_new)
            l = alpha * l + jnp.sum(p, axis=0, keepdims=True)
            acc = alpha * acc + _dot(vt, p.astype(BF16))
            m = m_new
        return acc / l
    acc = jnp.zeros((dv + ONES_ROWS, cols), F32)
    for k, vt, bias in chunks:
        p = jnp.exp2(scores(k, bias) - shift).astype(BF16)
        vt_aug = jnp.concatenate([vt, jnp.ones((ONES_ROWS, vt.shape[1]), BF16)], axis=0)
        acc = acc + _dot(vt_aug, p)
    return acc[:dv] / acc[dv:dv + 1]


def _by_underflow_guard(bound_ref, run):
    no_underflow = bound_ref[1] <= MAX_EXP2_SPAN

    @pl.when(no_underflow)
    def _():
        run(bound_ref[0])

    @pl.when(jnp.logical_not(no_underflow))
    def _():
        run(None)


def _diff_lambda(lam_ref, lam_init):
    lam1 = jnp.exp(jnp.sum(lam_ref[0:1, :] * lam_ref[1:2, :], axis=-1, keepdims=True))
    lam2 = jnp.exp(jnp.sum(lam_ref[2:3, :] * lam_ref[3:4, :], axis=-1, keepdims=True))
    return lam1 - lam2 + lam_init


def _diff_combine(o, tq, lam, g_out, lam_init):
    d = o[:tq] - lam * o[tq:]
    return _rms(d) * (g_out * (1.0 - lam_init))


def _attn_kernel(bound_ref, q_ref, k_ref, v_ref, kc_ref, vc_ref, *rest, stacked, diff,
                 lam_init, tk, n_cast):
    if diff:
        lam_ref, gout_ref = rest[:2]
        rest = rest[2:]
    cast_in, o_ref, cast_out = rest[:n_cast], rest[n_cast], rest[n_cast + 1:]
    tq = q_ref.shape[1]
    for src, dst in zip(cast_in, cast_out):
        dst[...] = src[...].astype(dst.dtype)

    def run(shift):
        if stacked:
            q = jnp.concatenate([q_ref[0], q_ref[1]], axis=0)
        else:
            q = q_ref[0]
        chunks = [(k_ref[0, c * tk:(c + 1) * tk, :], v_ref[0, :, c * tk:(c + 1) * tk], None)
                  for c in range(SEQ // tk)]
        chunks.append((kc_ref[0], vc_ref[0], None))
        o = _attend_t(q, chunks, shift).T
        if diff:
            lam = _diff_lambda(lam_ref, lam_init)
            o_ref[0] = _diff_combine(o, tq, lam, gout_ref[...], lam_init).astype(o_ref.dtype)
        elif stacked:
            o_ref[0] = o[:tq].astype(o_ref.dtype)
            o_ref[1] = o[tq:].astype(o_ref.dtype)
        else:
            o_ref[0] = o.astype(o_ref.dtype)

    _by_underflow_guard(bound_ref, run)


def _attention(bound, q, k, v, kc, vc, *, kind, extra=(), cast=(), lam_init=0.0, tq=512,
               tk=1024):
    nq = SEQ // tq
    dk = q.shape[-1]
    stacked = kind in ("gqa", "diff")
    n_outer = GQA_KV_HEADS if kind == "gqa" else N_HEADS
    q_heads = 2 if stacked else 1
    o_heads = 2 if kind == "gqa" else 1
    n_steps = BATCH * n_outer * nq
    cast2d = [w.reshape(-1, w.shape[-1]) for w in cast]
    cast_specs = [pl.BlockSpec((w.shape[0] // n_steps, w.shape[1]),
                               lambda b, h, i: ((b * n_outer + h) * nq + i, 0)) for w in cast2d]
    in_specs = [
        pl.BlockSpec(memory_space=pltpu.SMEM),
        pl.BlockSpec((q_heads, tq, dk), lambda b, h, i: (h, b * nq + i, 0)),
        pl.BlockSpec((1, SEQ, dk), lambda b, h, i: (h, b, 0)),
        pl.BlockSpec((1, LANES, SEQ), lambda b, h, i: (h, 0, b)),
        pl.BlockSpec((1, CTX_LEN, dk), lambda b, h, i: (h, b, 0)),
        pl.BlockSpec((1, LANES, CTX_LEN), lambda b, h, i: (h, 0, b)),
    ]
    for e in extra:
        in_specs.append(pl.BlockSpec(e.shape, lambda b, h, i, nd=e.ndim: (0,) * nd))
    outs = pl.pallas_call(
        functools.partial(_attn_kernel, stacked=stacked, diff=(kind == "diff"),
                          lam_init=lam_init, tk=tk, n_cast=len(cast)),
        grid=(BATCH, n_outer, nq),
        in_specs=in_specs + cast_specs,
        out_specs=[pl.BlockSpec((o_heads, tq, LANES), lambda b, h, i: (h, b * nq + i, 0))]
        + cast_specs,
        out_shape=[jax.ShapeDtypeStruct((N_HEADS, BATCH * SEQ, LANES), BF16)]
        + [jax.ShapeDtypeStruct(w.shape, BF16) for w in cast2d],
        compiler_params=pltpu.CompilerParams(
            dimension_semantics=("arbitrary", "arbitrary", "arbitrary"),
            vmem_limit_bytes=VMEM_LIMIT),
        name="attn_" + kind,
    )(bound, q, k, v, kc, vc, *extra, *cast2d)
    if not cast:
        return outs[0]
    return [outs[0]] + [o.reshape(w.shape) for o, w in zip(outs[1:], cast)]


def _na_geometry(i):
    rows = SEQ // GRID_W
    r0 = i * NA_QROWS
    band_start = min(max(r0 - NA_WIN_H // 2, 0), rows - NA_BAND_ROWS)
    rel = np.zeros((NA_QROWS, NA_BAND_ROWS), np.int64)
    valid = np.zeros((NA_QROWS, NA_BAND_ROWS), bool)
    for a in range(NA_QROWS):
        q_row = r0 + a
        row_start = min(max(q_row - NA_WIN_H // 2, 0), rows - NA_WIN_H)
        assert band_start <= row_start and row_start + NA_WIN_H <= band_start + NA_BAND_ROWS
        for j in range(NA_BAND_ROWS):
            k_row = band_start + j
            valid[a, j] = row_start <= k_row < row_start + NA_WIN_H
            rel[a, j] = min(max(k_row - q_row + NA_WIN_H - 1, 0), 2 * NA_WIN_H - 2)
    return band_start, rel, valid


def _na_pattern_of(i):
    return 0 if i == 0 else (NA_PATTERNS - 1 if i == NA_BLOCKS - 1 else 1)


_NA_REPRESENTATIVE = (0, 1, NA_BLOCKS - 1)
for _i in range(NA_BLOCKS):
    _g, _r = _na_geometry(_i), _na_geometry(_NA_REPRESENTATIVE[_na_pattern_of(_i)])
    assert (_g[1] == _r[1]).all() and (_g[2] == _r[2]).all()


def _na_bias_kernel(rpb_ref, o_ref):
    hd = pl.program_id(0)
    n_r = 2 * NA_WIN_H - 1
    n_c = 2 * NA_WIN_W - 1
    k_col = lax.broadcasted_iota(jnp.int32, (GRID_W, LANES), 0)
    lane = lax.broadcasted_iota(jnp.int32, (GRID_W, LANES), 1)
    q_col = lane % GRID_W
    left = lane < GRID_W
    col_idx = jnp.clip(k_col - q_col + NA_WIN_W - 1, 0, n_c - 1)
    col_start = jnp.clip(q_col - NA_WIN_W // 2, 0, GRID_W - NA_WIN_W)
    col_ok = (k_col >= col_start) & (k_col < col_start + NA_WIN_W)
    tables = []
    for r in range(n_r):
        t = jnp.zeros((GRID_W, LANES), F32)
        for c in range(n_c):
            t = jnp.where(col_idx == c, rpb_ref[hd * (n_r * n_c) + r * n_c + c] * LOG2E, t)
        tables.append(jnp.where(col_ok, t, NEG_BIG))
    masked = jnp.full((GRID_W, LANES), NEG_BIG, F32)
    for p, i in enumerate(_NA_REPRESENTATIVE):
        _, rel, valid = _na_geometry(i)
        for j in range(NA_BAND_ROWS):
            for a in range(0, NA_QROWS, 2):
                t0 = tables[int(rel[a, j])] if valid[a, j] else masked
                t1 = tables[int(rel[a + 1, j])] if valid[a + 1, j] else masked
                blk = t0 if t0 is t1 else jnp.where(left, t0, t1)
                o_ref[0, p, j * GRID_W:(j + 1) * GRID_W, a * GRID_W:(a + 2) * GRID_W] = blk


def _na_bias(rpb, layer):
    n = N_HEADS * (2 * NA_WIN_H - 1) * (2 * NA_WIN_W - 1)
    return pl.pallas_call(
        _na_bias_kernel,
        grid=(N_HEADS,),
        in_specs=[pl.BlockSpec(memory_space=pltpu.SMEM)],
        out_specs=pl.BlockSpec((1, NA_PATTERNS, NA_BAND, NA_QBLOCK), lambda h: (h, 0, 0, 0)),
        out_shape=jax.ShapeDtypeStruct((N_HEADS, NA_PATTERNS, NA_BAND, NA_QBLOCK), F32),
        compiler_params=pltpu.CompilerParams(
            dimension_semantics=("arbitrary",), vmem_limit_bytes=VMEM_LIMIT),
        name="na_bias",
    )(rpb.reshape(DEPTH, n)[layer])


def _na_kernel(bound_ref, q_ref, k_ref, vt_ref, kc_ref, vct_ref, bias_ref, o_ref):
    def run(shift):
        kc = kc_ref[0]
        vct = vct_ref[0]
        for i in range(NA_BLOCKS):
            k0 = _na_geometry(i)[0] * GRID_W
            rows = slice(i * NA_QBLOCK, (i + 1) * NA_QBLOCK)
            chunks = [(k_ref[0, k0:k0 + NA_BAND, :], vt_ref[0, :, k0:k0 + NA_BAND],
                       bias_ref[0, _na_pattern_of(i)]),
                      (kc, vct, None)]
            o_ref[0, rows, :] = _attend_t(q_ref[0, rows, :], chunks, shift).T.astype(o_ref.dtype)

    _by_underflow_guard(bound_ref, run)


def _na_attention(bound, q, k, vt, kc, vct, bias):
    return pl.pallas_call(
        _na_kernel,
        grid=(N_HEADS, BATCH),
        in_specs=[
            pl.BlockSpec(memory_space=pltpu.SMEM),
            pl.BlockSpec((1, SEQ, LANES), lambda h, b: (h, b, 0)),
            pl.BlockSpec((1, SEQ, LANES), lambda h, b: (h, b, 0)),
            pl.BlockSpec((1, LANES, SEQ), lambda h, b: (h, 0, b)),
            pl.BlockSpec((1, CTX_LEN, LANES), lambda h, b: (h, b, 0)),
            pl.BlockSpec((1, LANES, CTX_LEN), lambda h, b: (h, 0, b)),
            pl.BlockSpec((1, NA_PATTERNS, NA_BAND, NA_QBLOCK), lambda h, b: (h, 0, 0, 0)),
        ],
        out_specs=pl.BlockSpec((1, SEQ, LANES), lambda h, b: (h, b, 0)),
        out_shape=jax.ShapeDtypeStruct((N_HEADS, BATCH * SEQ, LANES), BF16),
        compiler_params=pltpu.CompilerParams(
            dimension_semantics=("arbitrary", "arbitrary"), vmem_limit_bytes=VMEM_LIMIT),
        name="attn_na",
    )(bound, q, k, vt, kc, vct, bias)


def _ctx_attn_kernel(qa_ref, ka_ref, va_ref, qb_ref, kb_ref, vb_ref, qc_ref, kc_ref, vc_ref,
                     qd_ref, kd_ref, vd_ref, lam_ref, gout_ref,
                     oa_ref, ob_ref, oc_ref, od_ref, *, lam_init):
    n = CTX_LEN
    lam = _diff_lambda(lam_ref, lam_init)

    def attend(q, k_ref, vt_ref, hd):
        return _attend_t(q, [(k_ref[hd], vt_ref[hd], None)]).T

    for hd in range(N_HEADS):
        oa_ref[hd] = attend(qa_ref[hd], ka_ref, va_ref, hd).astype(BF16)
        kvh = hd // (N_HEADS // GQA_KV_HEADS)
        ob_ref[hd] = attend(qb_ref[hd], kb_ref, vb_ref, kvh).astype(BF16)
        q2 = jnp.concatenate([qc_ref[2 * hd], qc_ref[2 * hd + 1]], axis=0)
        o2 = attend(q2, kc_ref, vc_ref, hd)
        oc_ref[hd] = _diff_combine(o2, n, lam, gout_ref[...], lam_init).astype(BF16)
        od_ref[hd] = attend(qd_ref[hd], kd_ref, vd_ref, hd).astype(BF16)


def _ctx_attention(fc, lam_rows, g_out, lam_init):
    in_specs = [pl.BlockSpec((nh, w, CTX_LEN), lambda b: (0, 0, b)) if tr
                else pl.BlockSpec((nh, CTX_LEN, w), lambda b: (0, b, 0))
                for nh, w, tr in _FRONT_OUT]
    in_specs += [pl.BlockSpec(lam_rows.shape, lambda b: (0, 0)),
                 pl.BlockSpec(g_out.shape, lambda b: (0, 0))]
    o_spec = pl.BlockSpec((N_HEADS, CTX_LEN, LANES), lambda b: (0, b, 0))
    o_shape = jax.ShapeDtypeStruct((N_HEADS, BATCH * CTX_LEN, LANES), BF16)
    return pl.pallas_call(
        functools.partial(_ctx_attn_kernel, lam_init=lam_init),
        grid=(BATCH,),
        in_specs=in_specs,
        out_specs=[o_spec] * 4,
        out_shape=[o_shape] * 4,
        compiler_params=pltpu.CompilerParams(
            dimension_semantics=("arbitrary",), vmem_limit_bytes=VMEM_LIMIT),
        name="attn_ctx",
    )(*fc, lam_rows, g_out)


def _out_kernel(x_ref, oa_ref, ob_ref, oc_ref, od_ref, w_ref, gt_ref, g_ref, sc_ref, sh_ref,
                xo_ref, h_ref):
    parts = [r[hd] for r in (oa_ref, ob_ref, oc_ref, od_ref) for hd in range(N_HEADS)]
    o = jnp.concatenate(parts, axis=-1)
    x = x_ref[...] + gt_ref[0] * _dot(o, w_ref[0])
    xo_ref[...] = x
    h_ref[...] = (_rms(x) * (g_ref[0] * (1.0 + sc_ref[0])) + sh_ref[0]).astype(BF16)


def _out_proj(x2d, outs, w_out, layer, gt, g_mlp, sc, sh, *, rows_per_mod, tm=512):
    t = x2d.shape[0]
    tiles_per_mod = rows_per_mod // tm
    mod_spec = pl.BlockSpec((1, 1, D_MODEL), lambda i: (i // tiles_per_mod, 0, 0))
    o_spec = pl.BlockSpec((N_HEADS, tm, LANES), lambda i: (0, i, 0))
    row_spec = pl.BlockSpec((tm, D_MODEL), lambda i: (i, 0))
    return pl.pallas_call(
        _out_kernel,
        grid=(t // tm,),
        in_specs=[row_spec, o_spec, o_spec, o_spec, o_spec,
                  pl.BlockSpec((1, D_MODEL, D_MODEL), lambda i: (layer, 0, 0),
                               pipeline_mode=pl.Buffered(1)),
                  mod_spec, pl.BlockSpec((1, 1, D_MODEL), lambda i: (layer, 0, 0)),
                  mod_spec, mod_spec],
        out_specs=[row_spec, row_spec],
        out_shape=[jax.ShapeDtypeStruct((t, D_MODEL), F32),
                   jax.ShapeDtypeStruct((t, D_MODEL), BF16)],
        compiler_params=pltpu.CompilerParams(
            dimension_semantics=("arbitrary",), vmem_limit_bytes=VMEM_LIMIT),
        name="out_proj",
    )(x2d, *outs, w_out, gt, g_mlp, sc, sh)


def _mlp_kernel(h_ref, wu_ref, wd_ref, x_ref, gt_ref, o_ref, xs_ref):
    j = pl.program_id(1)
    last = pl.num_programs(1) - 1
    slab = x_ref.shape[0]
    xs_ref[pl.ds(pl.multiple_of(j * slab, slab), slab), :] = x_ref[...]

    def step(is_first, is_last):
        u = jnp.maximum(_dot(h_ref[...], wu_ref[0]), 0.0)
        y = _dot((u * u).astype(BF16), wd_ref[0])
        acc = y if is_first else o_ref[...] + y
        o_ref[...] = xs_ref[...] + gt_ref[0] * acc if is_last else acc

    pl.when(j == 0)(lambda: step(True, False))
    pl.when(jnp.logical_and(j > 0, j < last))(lambda: step(False, False))
    pl.when(j == last)(lambda: step(False, True))


def _mlp(h2d, x2d, w_up, w_down, layer, gt, *, rows_per_mod, tm=1024, tf=1024):
    t = x2d.shape[0]
    tiles_per_mod = rows_per_mod // tm
    nj = D_FF // tf
    slab = tm // nj
    return pl.pallas_call(
        _mlp_kernel,
        grid=(t // tm, nj),
        in_specs=[
            pl.BlockSpec((tm, D_MODEL), lambda i, j: (i, 0)),
            pl.BlockSpec((1, D_MODEL, tf), lambda i, j: (layer, 0, j)),
            pl.BlockSpec((1, tf, D_MODEL), lambda i, j: (layer, j, 0)),
            pl.BlockSpec((slab, D_MODEL), lambda i, j: (i * nj + j, 0)),
            pl.BlockSpec((1, 1, D_MODEL), lambda i, j: (i // tiles_per_mod, 0, 0)),
        ],
        out_specs=pl.BlockSpec((tm, D_MODEL), lambda i, j: (i, 0)),
        out_shape=jax.ShapeDtypeStruct((t, D_MODEL), F32),
        scratch_shapes=[pltpu.VMEM((tm, D_MODEL), F32)],
        compiler_params=pltpu.CompilerParams(
            dimension_semantics=("arbitrary", "arbitrary"), vmem_limit_bytes=VMEM_LIMIT),
        name="mlp",
    )(h2d, w_up, w_down, x2d, gt)


def _rope_tables(tm):
    rows = SEQ // GRID_W
    pos = jnp.arange(max(rows, GRID_W)).astype(F32)

    def tables(rot_dim):
        half = rot_dim // 2
        inv_freq = ROPE_THETA ** (-jnp.arange(0, half, 2, dtype=F32) / half)
        ang = pos[:, None] * inv_freq
        cos, sin = jnp.cos(ang), jnp.sin(ang)
        by_row = lambda a: jnp.repeat(a[:rows], GRID_W, axis=0)
        by_col = lambda a: jnp.tile(a[:GRID_W], (rows, 1))
        cos = jnp.concatenate([by_row(cos), by_row(cos), by_col(cos), by_col(cos)], axis=-1)
        sin = jnp.concatenate([-by_row(sin), by_row(sin), -by_col(sin), by_col(sin)], axis=-1)
        return cos, sin

    cos128, sin128 = tables(HEAD_DIM)
    cos64, sin64 = tables(DIFF_HALF)
    lat = (cos128, sin128, jnp.tile(cos64, (1, 2)), jnp.tile(sin64, (1, 2)))
    ones = jnp.ones((tm, LANES), F32)
    zeros = jnp.zeros((tm, LANES), F32)
    return lat, (ones, zeros, ones, zeros)


def _prepare_params(w_in, mla_w_uq, mla_w_ukv, g_norm_mix, mla_g_qa, mla_g_kva, mla_g_q, mla_g_k,
                    gqa_g_q, gqa_g_k, diff_g_q, diff_g_k, diff_g_out, na_g_q, na_g_k, na_rpb):
    w_in_t = jnp.swapaxes(w_in, 1, 2).astype(BF16)
    uq = mla_w_uq.reshape(DEPTH, MLA_Q_RANK, N_HEADS, MLA_NOPE + MLA_ROPE)
    w_uq = jnp.concatenate(
        [uq[..., :MLA_NOPE].reshape(DEPTH, MLA_Q_RANK, -1),
         uq[..., MLA_NOPE:].reshape(DEPTH, MLA_Q_RANK, -1)], axis=2).astype(BF16)
    ukv = mla_w_ukv.reshape(DEPTH, MLA_KV_RANK, N_HEADS, MLA_NOPE + HEAD_DIM)
    w_ukv = jnp.concatenate(
        [ukv[..., :MLA_NOPE].reshape(DEPTH, MLA_KV_RANK, -1),
         ukv[..., MLA_NOPE:].reshape(DEPTH, MLA_KV_RANK, -1)], axis=2).astype(BF16)

    scale_a = (MLA_NOPE + MLA_ROPE) ** -0.5 * LOG2E
    scale_b = HEAD_DIM ** -0.5 * LOG2E
    scale_c = DIFF_HALF ** -0.5 * LOG2E
    scale_d = HEAD_DIM ** -0.5 * LOG2E
    two = lambda g: jnp.tile(g, (1, 2))
    rows = [None] * 11
    rows[G_A_QN] = mla_g_q[:, :MLA_NOPE] * scale_a
    rows[G_A_QR] = two(mla_g_q[:, MLA_NOPE:]) * scale_a
    rows[G_A_KN] = mla_g_k[:, :MLA_NOPE]
    rows[G_A_KR] = two(mla_g_k[:, MLA_NOPE:])
    rows[G_B_Q] = gqa_g_q * scale_b
    rows[G_B_K] = gqa_g_k
    rows[G_C_Q] = two(diff_g_q) * scale_c
    rows[G_C_K] = two(diff_g_k)
    rows[G_D_Q] = na_g_q * scale_d
    rows[G_D_K] = na_g_k
    rows[G_C_OUT] = diff_g_out
    pad = [jnp.zeros((DEPTH, LANES), F32)] * (GAIN_ROWS - len(rows))
    gains = jnp.stack(rows + pad, axis=1).astype(F32)

    amax = lambda r: jnp.max(jnp.abs(rows[r]), axis=1)
    slack = 1.02
    b_a = slack * (jnp.sqrt(MLA_NOPE * amax(G_A_QN) ** 2 + MLA_ROPE * amax(G_A_QR) ** 2)
                   * jnp.sqrt(MLA_NOPE * amax(G_A_KN) ** 2 + MLA_ROPE * amax(G_A_KR) ** 2))
    b_b = slack * HEAD_DIM * amax(G_B_Q) * amax(G_B_K)
    b_c = slack * DIFF_HALF * amax(G_C_Q) * amax(G_C_K)
    b_d = slack * HEAD_DIM * amax(G_D_Q) * amax(G_D_K)
    bias_hi = jnp.maximum(jnp.max(na_rpb, axis=(1, 2, 3)), 0.0) * LOG2E
    bias_lo = jnp.minimum(jnp.min(na_rpb, axis=(1, 2, 3)), 0.0) * LOG2E
    span = lambda b: jnp.stack([b, 2.0 * b], axis=1).astype(F32)
    return {
        "bound_a": span(b_a), "bound_b": span(b_b), "bound_c": span(b_c),
        "bound_d": jnp.stack([b_d + bias_hi, 2.0 * b_d + bias_hi - bias_lo], axis=1).astype(F32),
        "w_in_t": w_in_t, "w_uq": w_uq, "w_ukv": w_ukv,
        "g_mix": g_norm_mix.reshape(DEPTH, 1, D_MODEL),
        "g_qa": mla_g_qa.reshape(DEPTH, 1, -1), "g_kva": mla_g_kva.reshape(DEPTH, 1, -1),
        "gains": gains,
    }


def kernel(x, c, ctx, c_ctx, w_mod, b_mod, g_norm_mix, g_norm_mlp, w_in, mla_g_qa, mla_g_kva,
           mla_w_uq, mla_w_ukv, mla_g_q, mla_g_k, gqa_g_q, gqa_g_k, diff_g_q, diff_g_k,
           diff_lq1, diff_lk1, diff_lq2, diff_lk2, diff_g_out, na_g_q, na_g_k, na_rpb,
           w_out, w_up, w_down):
    tm_front = 512
    cc = jnp.concatenate([c, c_ctx[None, :], jnp.zeros((8 - BATCH - 1, D_MODEL), F32)], axis=0)
    mod = _modulation(cc, w_mod, b_mod)
    lat_tabs, ctx_tabs = _rope_tables(tm_front)
    pw = _prepare_params(w_in, mla_w_uq, mla_w_ukv, g_norm_mix, mla_g_qa, mla_g_kva, mla_g_q,
                         mla_g_k, gqa_g_q, gqa_g_k, diff_g_q, diff_g_k, diff_g_out, na_g_q,
                         na_g_k, na_rpb)
    g_mlp = g_norm_mlp.reshape(DEPTH, 1, D_MODEL)

    xs = x.reshape(BATCH * SEQ, D_MODEL)
    cs = ctx.reshape(BATCH * CTX_LEN, D_MODEL)
    for l in range(DEPTH):
        need_ctx = l < DEPTH - 1
        lam_init = 0.8 - 0.6 * math.exp(-0.3 * l)
        m6 = mod[l].reshape(8, 6, 1, D_MODEL)
        sh_a, sc_a, gt_a, sh_m, sc_m, gt_m = [m6[:BATCH, k] for k in range(6)]
        csh_a, csc_a, cgt_a, csh_m, csc_m, cgt_m = [m6[BATCH:BATCH + 1, k] for k in range(6)]
        lam_rows = jnp.stack([diff_lq1[l], diff_lk1[l], diff_lq2[l], diff_lk2[l]]).astype(F32)
        g_out = pw["gains"][l, G_C_OUT:G_C_OUT + 1]

        fl = _front(xs, sc_a, sh_a, pw, l, lat_tabs, rows_per_mod=SEQ, tm=tm_front)
        fc = _front(cs, csc_a, csh_a, pw, l, ctx_tabs, rows_per_mod=BATCH * CTX_LEN, tm=tm_front)
        qa, ka, va, qb, kb, vb, qc, kc, vc, qd, kd, vd = fl
        cqa, cka, cva, cqb, ckb, cvb, cqc, ckc, cvc, cqd, ckd, cvd = fc

        o_a = _attention(pw["bound_a"][l], qa, ka, va, cka, cva, kind="mla", tq=2048)
        o_b = _attention(pw["bound_b"][l], qb, kb, vb, ckb, cvb, kind="gqa", tq=1024)
        o_c = _attention(pw["bound_c"][l], qc, kc, vc, ckc, cvc, kind="diff", tq=1024,
                         extra=(lam_rows, g_out), lam_init=lam_init,
                         cast=(w_out, w_up, w_down) if l == 0 else ())
        if l == 0:
            o_c, w_out_b, w_up_b, w_down_b = o_c
        o_d = _na_attention(pw["bound_d"][l], qd, kd, vd, ckd, cvd, _na_bias(na_rpb, l))

        x_mid, h_mlp = _out_proj(xs, (o_a, o_b, o_c, o_d), w_out_b, l, gt_a, g_mlp, sc_m, sh_m,
                                 rows_per_mod=SEQ)
        xs = _mlp(h_mlp, x_mid, w_up_b, w_down_b, l, gt_m, rows_per_mod=SEQ)

        if need_ctx:
            oc = _ctx_attention(fc, lam_rows, g_out, lam_init)
            c_mid, ch_mlp = _out_proj(cs, oc, w_out_b, l, cgt_a, g_mlp, csc_m, csh_m,
                                      rows_per_mod=BATCH * CTX_LEN)
            cs = _mlp(ch_mlp, c_mid, w_up_b, w_down_b, l, cgt_m, rows_per_mod=BATCH * CTX_LEN)
    return xs.reshape(BATCH, SEQ, D_MODEL)
```

```python
import functools
import math

import numpy as np
import jax
import jax.numpy as jnp
from jax import lax
from jax.experimental import pallas as pl
from jax.experimental.pallas import tpu as pltpu

D_MODEL = 2048
BATCH = 4
SEQ = 4096
DEPTH = 2
GRID_W = 64
CTX_LEN = 256
HEAD_DIM = 128
GROUP_W = D_MODEL // 4
N_HEADS = GROUP_W // HEAD_DIM
D_FF = 4 * D_MODEL
ROPE_THETA = 10000.0
NORM_EPS = 1e-6

MLA_Q_RANK = GROUP_W
MLA_KV_RANK = GROUP_W // 2
MLA_NOPE = 128
MLA_ROPE = 64
GQA_KV_HEADS = N_HEADS // 2
DIFF_HALF = HEAD_DIM // 2
NA_WIN_H = 8
NA_WIN_W = 16
NA_QROWS = 8
NA_QBLOCK = NA_QROWS * GRID_W
NA_BLOCKS = SEQ // NA_QBLOCK
NA_BAND_ROWS = NA_WIN_H + NA_QROWS
NA_BAND = NA_BAND_ROWS * GRID_W
NA_PATTERNS = 3

A_COLS = MLA_Q_RANK + MLA_KV_RANK + MLA_ROPE
B_COLS = (N_HEADS + 2 * GQA_KV_HEADS) * HEAD_DIM
C_COLS = 3 * N_HEADS * HEAD_DIM
D_COLS = 3 * N_HEADS * HEAD_DIM

LANES = 128
OFF_AQ = 0
OFF_AKV = OFF_AQ + MLA_Q_RANK
OFF_KPE = OFF_AKV + MLA_KV_RANK
OFF_B = A_COLS
OFF_C = OFF_B + B_COLS
OFF_D = OFF_C + C_COLS

VMEM_LIMIT = 60 * 1024 * 1024
NEG_BIG = -1e30
LOG2E = math.log2(math.e)
MAX_EXP2_SPAN = 96.0
SUBLANES = 8

F32 = jnp.float32
BF16 = jnp.bfloat16

G_A_QN, G_A_QR, G_A_KN, G_A_KR, G_B_Q, G_B_K, G_C_Q, G_C_K, G_D_Q, G_D_K, G_C_OUT = range(11)
GAIN_ROWS = 16


def _lane_iota(shape):
    return lax.broadcasted_iota(jnp.int32, shape, len(shape) - 1)


def _rms(t):
    return t * lax.rsqrt(jnp.mean(t * t, axis=-1, keepdims=True) + NORM_EPS)


def _rms64(t):
    lo = _lane_iota(t.shape) < 64
    sq = t * t
    s_lo = jnp.sum(jnp.where(lo, sq, 0.0), axis=-1, keepdims=True)
    s_hi = jnp.sum(jnp.where(lo, 0.0, sq), axis=-1, keepdims=True)
    ms = jnp.where(lo, s_lo, s_hi) * (1.0 / 64.0)
    return t * lax.rsqrt(ms + NORM_EPS)


def _rope(t, cos, sin_signed, half):
    first = (_lane_iota(t.shape) % (2 * half)) < half
    partner = jnp.where(first, pltpu.roll(t, LANES - half, 1), pltpu.roll(t, half, 1))
    return t * cos + partner * sin_signed


def _dot(a, b):
    return jnp.dot(a, b, preferred_element_type=F32)


def _dot_nt(a, b):
    return lax.dot_general(a, b, (((1,), (1,)), ((), ())), preferred_element_type=F32)


def _mod_kernel(c_ref, w_ref, b_ref, o_ref):
    c = c_ref[...]
    a = (c * jax.nn.sigmoid(c)).astype(BF16)
    o_ref[0] = _dot(a, w_ref[0].astype(BF16)) + b_ref[0]


def _modulation(cc, w_mod, b_mod):
    tn = 1536
    n = 6 * D_MODEL
    return pl.pallas_call(
        _mod_kernel,
        grid=(DEPTH, n // tn),
        in_specs=[
            pl.BlockSpec((8, D_MODEL), lambda l, j: (0, 0)),
            pl.BlockSpec((1, D_MODEL, tn), lambda l, j: (l, 0, j)),
            pl.BlockSpec((1, 1, tn), lambda l, j: (l, 0, j)),
        ],
        out_specs=pl.BlockSpec((1, 8, tn), lambda l, j: (l, 0, j)),
        out_shape=jax.ShapeDtypeStruct((DEPTH, 8, n), F32),
        compiler_params=pltpu.CompilerParams(
            dimension_semantics=("arbitrary", "arbitrary"), vmem_limit_bytes=VMEM_LIMIT),
        name="modulation",
    )(cc, w_mod, b_mod.reshape(DEPTH, 1, n))


def _front_kernel(x_ref, gmix_ref, sc_ref, sh_ref, wt_ref, w_uq_ref, w_ukv_ref,
                  gqa_ref, gkva_ref, gains_ref, cos128_ref, sin128_ref, cos64_ref, sin64_ref,
                  qa_ref, ka_ref, va_ref, qb_ref, kb_ref, vb_ref,
                  qc_ref, kc_ref, vc_ref, qd_ref, kd_ref, vd_ref):
    x = x_ref[...]
    gm = gmix_ref[0] * (1.0 + sc_ref[0])
    h = (_rms(x) * gm + sh_ref[0]).astype(BF16)

    gains = gains_ref[0]

    def gain(row):
        return gains[row:row + 1, :]

    cos128 = cos128_ref[...]
    sin128 = sin128_ref[...]
    cos64 = cos64_ref[...]
    sin64 = sin64_ref[...]
    lo = _lane_iota((x.shape[0], LANES)) < 64

    def proj(off, width):
        return _dot_nt(h, wt_ref[0, off:off + width, :])

    def tile(t, j):
        return t[:, j * LANES:(j + 1) * LANES]

    def store_t(ref, hd, t):
        ref[hd] = t.T.astype(BF16)

    cq = (_rms(proj(OFF_AQ, MLA_Q_RANK)) * gqa_ref[0]).astype(BF16)
    ckv = (_rms(proj(OFF_AKV, MLA_KV_RANK)) * gkva_ref[0]).astype(BF16)
    w_kpe = wt_ref[0, OFF_KPE:A_COLS, :]
    p_kpe = _dot_nt(h, jnp.concatenate([w_kpe, w_kpe], axis=0))
    kr = _rope(_rms64(p_kpe) * gain(G_A_KR), cos64, sin64, 16).astype(BF16)
    q = _dot(cq, w_uq_ref[0])
    kv = _dot(ckv, w_ukv_ref[0])
    qr_pairs = [
        _rope(_rms64(tile(q, N_HEADS + j)) * gain(G_A_QR), cos64, sin64, 16)
        for j in range(N_HEADS // 2)
    ]
    for hd in range(N_HEADS):
        qa_ref[hd, :, 0:LANES] = (_rms(tile(q, hd)) * gain(G_A_QN)).astype(BF16)
        pair = qr_pairs[hd // 2]
        keep = lo if hd % 2 == 0 else jnp.logical_not(lo)
        qa_ref[hd, :, LANES:2 * LANES] = jnp.where(keep, pair, 0.0).astype(BF16)
        ka_ref[hd, :, 0:LANES] = (_rms(tile(kv, hd)) * gain(G_A_KN)).astype(BF16)
        ka_ref[hd, :, LANES:2 * LANES] = kr
        store_t(va_ref, hd, tile(kv, N_HEADS + hd))

    pb = proj(OFF_B, B_COLS)
    for hd in range(N_HEADS):
        qb_ref[hd] = _rope(_rms(tile(pb, hd)) * gain(G_B_Q), cos128, sin128, 32).astype(BF16)
    for hd in range(GQA_KV_HEADS):
        kb_ref[hd] = _rope(_rms(tile(pb, N_HEADS + hd)) * gain(G_B_K),
                           cos128, sin128, 32).astype(BF16)
        store_t(vb_ref, hd, tile(pb, N_HEADS + GQA_KV_HEADS + hd))

    pc = proj(OFF_C, C_COLS)
    for hd in range(N_HEADS):
        qh = _rope(_rms64(tile(pc, hd)) * gain(G_C_Q), cos64, sin64, 16)
        qc_ref[2 * hd] = jnp.where(lo, qh, 0.0).astype(BF16)
        qc_ref[2 * hd + 1] = jnp.where(lo, 0.0, qh).astype(BF16)
        kc_ref[hd] = _rope(_rms64(tile(pc, N_HEADS + hd)) * gain(G_C_K),
                           cos64, sin64, 16).astype(BF16)
        store_t(vc_ref, hd, tile(pc, 2 * N_HEADS + hd))

    pd = proj(OFF_D, D_COLS)
    for hd in range(N_HEADS):
        qd_ref[hd] = (_rms(tile(pd, hd)) * gain(G_D_Q)).astype(BF16)
        kd_ref[hd] = (_rms(tile(pd, N_HEADS + hd)) * gain(G_D_K)).astype(BF16)
        store_t(vd_ref, hd, tile(pd, 2 * N_HEADS + hd))


_FRONT_OUT = (
    (N_HEADS, 2 * LANES, False), (N_HEADS, 2 * LANES, False), (N_HEADS, LANES, True),
    (N_HEADS, LANES, False), (GQA_KV_HEADS, LANES, False), (GQA_KV_HEADS, LANES, True),
    (2 * N_HEADS, LANES, False), (N_HEADS, LANES, False), (N_HEADS, LANES, True),
    (N_HEADS, LANES, False), (N_HEADS, LANES, False), (N_HEADS, LANES, True),
)


def _front(x2d, sc, sh, pw, layer, tabs, *, rows_per_mod, tm=256):
    t = x2d.shape[0]
    n_pos_tiles = tabs[0].shape[0] // tm
    tiles_per_mod = rows_per_mod // tm

    def resident(a):
        return pl.BlockSpec((1,) + a.shape[1:], lambda i: (layer,) + (0,) * (a.ndim - 1),
                            pipeline_mode=pl.Buffered(1))

    mod_spec = pl.BlockSpec((1, 1, D_MODEL), lambda i: (i // tiles_per_mod, 0, 0))
    tab_spec = pl.BlockSpec((tm, LANES), lambda i: (i % n_pos_tiles, 0))
    params = [pw["g_mix"], None, None, pw["w_in_t"], pw["w_uq"], pw["w_ukv"],
              pw["g_qa"], pw["g_kva"], pw["gains"]]
    in_specs = [pl.BlockSpec((tm, D_MODEL), lambda i: (i, 0))]
    in_specs += [mod_spec if p is None else resident(p) for p in params]
    in_specs += [tab_spec] * 4
    params[1], params[2] = sc, sh
    out_specs = [pl.BlockSpec((nh, w, tm), lambda i: (0, 0, i)) if tr
                 else pl.BlockSpec((nh, tm, w), lambda i: (0, i, 0)) for nh, w, tr in _FRONT_OUT]
    out_shape = [jax.ShapeDtypeStruct((nh, w, t) if tr else (nh, t, w), BF16)
                 for nh, w, tr in _FRONT_OUT]
    return pl.pallas_call(
        _front_kernel,
        grid=(t // tm,),
        in_specs=in_specs,
        out_specs=out_specs,
        out_shape=out_shape,
        compiler_params=pltpu.CompilerParams(
            dimension_semantics=("arbitrary",), vmem_limit_bytes=VMEM_LIMIT),
        name="front",
    )(x2d, *params, *tabs)


def _attend_t(q, chunks, shift=None):
    cols = q.shape[0]
    dv = chunks[0][1].shape[0]

    def scores(k, bias):
        st = _dot_nt(k, q)
        return st if bias is None else st + bias

    if shift is None:
        m = jnp.full((1, cols), NEG_BIG, F32)
        l = jnp.zeros((1, cols), F32)
        acc = jnp.zeros((dv, cols), F32)
        for k, vt, bias in chunks:
            st = scores(k, bias)
            m_new = jnp.maximum(m, jnp.max(st, axis=0, keepdims=True))
            alpha = jnp.exp2(m - m_new)
            p = jnp.exp2(st - m_new)
            l = alpha * l + jnp.sum(p, axis=0, keepdims=True)
            acc = alpha * acc + _dot(vt, p.astype(BF16))
            m = m_new
        return acc / l
    acc = jnp.zeros((dv, cols), F32)
    l8 = jnp.zeros((SUBLANES, cols), F32)
    for k, vt, bias in chunks:
        p = jnp.exp2(scores(k, bias) - shift)
        l8 = l8 + jnp.sum(p.reshape(p.shape[0] // SUBLANES, SUBLANES, cols), axis=0)
        acc = acc + _dot(vt, p.astype(BF16))
    return acc / jnp.sum(l8, axis=0, keepdims=True)


def _by_underflow_guard(bound_ref, run):
    no_underflow = bound_ref[1] <= MAX_EXP2_SPAN

    @pl.when(no_underflow)
    def _():
        run(bound_ref[0])

    @pl.when(jnp.logical_not(no_underflow))
    def _():
        run(None)


def _diff_lambda(lam_ref, lam_init):
    lam1 = jnp.exp(jnp.sum(lam_ref[0:1, :] * lam_ref[1:2, :], axis=-1, keepdims=True))
    lam2 = jnp.exp(jnp.sum(lam_ref[2:3, :] * lam_ref[3:4, :], axis=-1, keepdims=True))
    return lam1 - lam2 + lam_init


def _diff_combine(o, tq, lam, g_out, lam_init):
    d = o[:tq] - lam * o[tq:]
    return _rms(d) * (g_out * (1.0 - lam_init))


def _attn_kernel(bound_ref, q_ref, k_ref, v_ref, kc_ref, vc_ref, *rest, stacked, diff,
                 lam_init, tk, n_cast):
    if diff:
        lam_ref, gout_ref = rest[:2]
        rest = rest[2:]
    cast_in, o_ref, cast_out = rest[:n_cast], rest[n_cast], rest[n_cast + 1:]
    tq = q_ref.shape[1]
    for src, dst in zip(cast_in, cast_out):
        dst[...] = src[...].astype(dst.dtype)

    def run(shift):
        if stacked:
            q = jnp.concatenate([q_ref[0], q_ref[1]], axis=0)
        else:
            q = q_ref[0]
        chunks = [(k_ref[0, c * tk:(c + 1) * tk, :], v_ref[0, :, c * tk:(c + 1) * tk], None)
                  for c in range(SEQ // tk)]
        chunks.append((kc_ref[0], vc_ref[0], None))
        o = _attend_t(q, chunks, shift).T
        if diff:
            lam = _diff_lambda(lam_ref, lam_init)
            o_ref[0] = _diff_combine(o, tq, lam, gout_ref[...], lam_init).astype(o_ref.dtype)
        elif stacked:
            o_ref[0] = o[:tq].astype(o_ref.dtype)
            o_ref[1] = o[tq:].astype(o_ref.dtype)
        else:
            o_ref[0] = o.astype(o_ref.dtype)

    _by_underflow_guard(bound_ref, run)


def _attention(bound, q, k, v, kc, vc, *, kind, extra=(), cast=(), lam_init=0.0, tq=512,
               tk=1024):
    nq = SEQ // tq
    dk = q.shape[-1]
    stacked = kind in ("gqa", "diff")
    n_outer = GQA_KV_HEADS if kind == "gqa" else N_HEADS
    q_heads = 2 if stacked else 1
    o_heads = 2 if kind == "gqa" else 1
    n_steps = BATCH * n_outer * nq
    cast2d = [w.reshape(-1, w.shape[-1]) for w in cast]
    cast_specs = [pl.BlockSpec((w.shape[0] // n_steps, w.shape[1]),
                               lambda b, h, i: ((b * n_outer + h) * nq + i, 0)) for w in cast2d]
    in_specs = [
        pl.BlockSpec(memory_space=pltpu.SMEM),
        pl.BlockSpec((q_heads, tq, dk), lambda b, h, i: (h, b * nq + i, 0)),
        pl.BlockSpec((1, SEQ, dk), lambda b, h, i: (h, b, 0)),
        pl.BlockSpec((1, LANES, SEQ), lambda b, h, i: (h, 0, b)),
        pl.BlockSpec((1, CTX_LEN, dk), lambda b, h, i: (h, b, 0)),
        pl.BlockSpec((1, LANES, CTX_LEN), lambda b, h, i: (h, 0, b)),
    ]
    for e in extra:
        in_specs.append(pl.BlockSpec(e.shape, lambda b, h, i, nd=e.ndim: (0,) * nd))
    outs = pl.pallas_call(
        functools.partial(_attn_kernel, stacked=stacked, diff=(kind == "diff"),
                          lam_init=lam_init, tk=tk, n_cast=len(cast)),
        grid=(BATCH, n_outer, nq),
        in_specs=in_specs + cast_specs,
        out_specs=[pl.BlockSpec((o_heads, tq, LANES), lambda b, h, i: (h, b * nq + i, 0))]
        + cast_specs,
        out_shape=[jax.ShapeDtypeStruct((N_HEADS, BATCH * SEQ, LANES), BF16)]
        + [jax.ShapeDtypeStruct(w.shape, BF16) for w in cast2d],
        compiler_params=pltpu.CompilerParams(
            dimension_semantics=("arbitrary", "arbitrary", "arbitrary"),
            vmem_limit_bytes=VMEM_LIMIT),
        name="attn_" + kind,
    )(bound, q, k, v, kc, vc, *extra, *cast2d)
    if not cast:
        return outs[0]
    return [outs[0]] + [o.reshape(w.shape) for o, w in zip(outs[1:], cast)]


def _na_geometry(i):
    rows = SEQ // GRID_W
    r0 = i * NA_QROWS
    band_start = min(max(r0 - NA_WIN_H // 2, 0), rows - NA_BAND_ROWS)
    rel = np.zeros((NA_QROWS, NA_BAND_ROWS), np.int64)
    valid = np.zeros((NA_QROWS, NA_BAND_ROWS), bool)
    for a in range(NA_QROWS):
        q_row = r0 + a
        row_start = min(max(q_row - NA_WIN_H // 2, 0), rows - NA_WIN_H)
        assert band_start <= row_start and row_start + NA_WIN_H <= band_start + NA_BAND_ROWS
        for j in range(NA_BAND_ROWS):
            k_row = band_start + j
            valid[a, j] = row_start <= k_row < row_start + NA_WIN_H
            rel[a, j] = min(max(k_row - q_row + NA_WIN_H - 1, 0), 2 * NA_WIN_H - 2)
    return band_start, rel, valid


def _na_pattern_of(i):
    return 0 if i == 0 else (NA_PATTERNS - 1 if i == NA_BLOCKS - 1 else 1)


_NA_REPRESENTATIVE = (0, 1, NA_BLOCKS - 1)
for _i in range(NA_BLOCKS):
    _g, _r = _na_geometry(_i), _na_geometry(_NA_REPRESENTATIVE[_na_pattern_of(_i)])
    assert (_g[1] == _r[1]).all() and (_g[2] == _r[2]).all()


def _na_bias_kernel(rpb_ref, o_ref):
    hd = pl.program_id(0)
    n_r = 2 * NA_WIN_H - 1
    n_c = 2 * NA_WIN_W - 1
    k_col = lax.broadcasted_iota(jnp.int32, (GRID_W, LANES), 0)
    lane = lax.broadcasted_iota(jnp.int32, (GRID_W, LANES), 1)
    q_col = lane % GRID_W
    left = lane < GRID_W
    col_idx = jnp.clip(k_col - q_col + NA_WIN_W - 1, 0, n_c - 1)
    col_start = jnp.clip(q_col - NA_WIN_W // 2, 0, GRID_W - NA_WIN_W)
    col_ok = (k_col >= col_start) & (k_col < col_start + NA_WIN_W)
    tables = []
    for r in range(n_r):
        t = jnp.zeros((GRID_W, LANES), F32)
        for c in range(n_c):
            t = jnp.where(col_idx == c, rpb_ref[hd * (n_r * n_c) + r * n_c + c] * LOG2E, t)
        tables.append(jnp.where(col_ok, t, NEG_BIG))
    masked = jnp.full((GRID_W, LANES), NEG_BIG, F32)
    for p, i in enumerate(_NA_REPRESENTATIVE):
        _, rel, valid = _na_geometry(i)
        for j in range(NA_BAND_ROWS):
            for a in range(0, NA_QROWS, 2):
                t0 = tables[int(rel[a, j])] if valid[a, j] else masked
                t1 = tables[int(rel[a + 1, j])] if valid[a + 1, j] else masked
                blk = t0 if t0 is t1 else jnp.where(left, t0, t1)
                o_ref[0, p, j * GRID_W:(j + 1) * GRID_W, a * GRID_W:(a + 2) * GRID_W] = blk


def _na_bias(rpb, layer):
    n = N_HEADS * (2 * NA_WIN_H - 1) * (2 * NA_WIN_W - 1)
    return pl.pallas_call(
        _na_bias_kernel,
        grid=(N_HEADS,),
        in_specs=[pl.BlockSpec(memory_space=pltpu.SMEM)],
        out_specs=pl.BlockSpec((1, NA_PATTERNS, NA_BAND, NA_QBLOCK), lambda h: (h, 0, 0, 0)),
        out_shape=jax.ShapeDtypeStruct((N_HEADS, NA_PATTERNS, NA_BAND, NA_QBLOCK), F32),
        compiler_params=pltpu.CompilerParams(
            dimension_semantics=("arbitrary",), vmem_limit_bytes=VMEM_LIMIT),
        name="na_bias",
    )(rpb.reshape(DEPTH, n)[layer])


def _na_kernel(bound_ref, q_ref, k_ref, vt_ref, kc_ref, vct_ref, bias_ref, o_ref):
    def run(shift):
        kc = kc_ref[0]
        vct = vct_ref[0]
        for i in range(NA_BLOCKS):
            k0 = _na_geometry(i)[0] * GRID_W
            rows = slice(i * NA_QBLOCK, (i + 1) * NA_QBLOCK)
            chunks = [(k_ref[0, k0:k0 + NA_BAND, :], vt_ref[0, :, k0:k0 + NA_BAND],
                       bias_ref[0, _na_pattern_of(i)]),
                      (kc, vct, None)]
            o_ref[0, rows, :] = _attend_t(q_ref[0, rows, :], chunks, shift).T.astype(o_ref.dtype)

    _by_underflow_guard(bound_ref, run)


def _na_attention(bound, q, k, vt, kc, vct, bias):
    return pl.pallas_call(
        _na_kernel,
        grid=(N_HEADS, BATCH),
        in_specs=[
            pl.BlockSpec(memory_space=pltpu.SMEM),
            pl.BlockSpec((1, SEQ, LANES), lambda h, b: (h, b, 0)),
            pl.BlockSpec((1, SEQ, LANES), lambda h, b: (h, b, 0)),
            pl.BlockSpec((1, LANES, SEQ), lambda h, b: (h, 0, b)),
            pl.BlockSpec((1, CTX_LEN, LANES), lambda h, b: (h, b, 0)),
            pl.BlockSpec((1, LANES, CTX_LEN), lambda h, b: (h, 0, b)),
            pl.BlockSpec((1, NA_PATTERNS, NA_BAND, NA_QBLOCK), lambda h, b: (h, 0, 0, 0)),
        ],
        out_specs=pl.BlockSpec((1, SEQ, LANES), lambda h, b: (h, b, 0)),
        out_shape=jax.ShapeDtypeStruct((N_HEADS, BATCH * SEQ, LANES), BF16),
        compiler_params=pltpu.CompilerParams(
            dimension_semantics=("arbitrary", "arbitrary"), vmem_limit_bytes=VMEM_LIMIT),
        name="attn_na",
    )(bound, q, k, vt, kc, vct, bias)


def _ctx_attn_kernel(qa_ref, ka_ref, va_ref, qb_ref, kb_ref, vb_ref, qc_ref, kc_ref, vc_ref,
                     qd_ref, kd_ref, vd_ref, lam_ref, gout_ref,
                     oa_ref, ob_ref, oc_ref, od_ref, *, lam_init):
    n = CTX_LEN
    lam = _diff_lambda(lam_ref, lam_init)

    def attend(q, k_ref, vt_ref, hd):
        return _attend_t(q, [(k_ref[hd], vt_ref[hd], None)]).T

    for hd in range(N_HEADS):
        oa_ref[hd] = attend(qa_ref[hd], ka_ref, va_ref, hd).astype(BF16)
        kvh = hd // (N_HEADS // GQA_KV_HEADS)
        ob_ref[hd] = attend(qb_ref[hd], kb_ref, vb_ref, kvh).astype(BF16)
        q2 = jnp.concatenate([qc_ref[2 * hd], qc_ref[2 * hd + 1]], axis=0)
        o2 = attend(q2, kc_ref, vc_ref, hd)
        oc_ref[hd] = _diff_combine(o2, n, lam, gout_ref[...], lam_init).astype(BF16)
        od_ref[hd] = attend(qd_ref[hd], kd_ref, vd_ref, hd).astype(BF16)


def _ctx_attention(fc, lam_rows, g_out, lam_init):
    in_specs = [pl.BlockSpec((nh, w, CTX_LEN), lambda b: (0, 0, b)) if tr
                else pl.BlockSpec((nh, CTX_LEN, w), lambda b: (0, b, 0))
                for nh, w, tr in _FRONT_OUT]
    in_specs += [pl.BlockSpec(lam_rows.shape, lambda b: (0, 0)),
                 pl.BlockSpec(g_out.shape, lambda b: (0, 0))]
    o_spec = pl.BlockSpec((N_HEADS, CTX_LEN, LANES), lambda b: (0, b, 0))
    o_shape = jax.ShapeDtypeStruct((N_HEADS, BATCH * CTX_LEN, LANES), BF16)
    return pl.pallas_call(
        functools.partial(_ctx_attn_kernel, lam_init=lam_init),
        grid=(BATCH,),
        in_specs=in_specs,
        out_specs=[o_spec] * 4,
        out_shape=[o_shape] * 4,
        compiler_params=pltpu.CompilerParams(
            dimension_semantics=("arbitrary",), vmem_limit_bytes=VMEM_LIMIT),
        name="attn_ctx",
    )(*fc, lam_rows, g_out)


def _out_kernel(x_ref, oa_ref, ob_ref, oc_ref, od_ref, w_ref, gt_ref, g_ref, sc_ref, sh_ref,
                xo_ref, h_ref):
    parts = [r[hd] for r in (oa_ref, ob_ref, oc_ref, od_ref) for hd in range(N_HEADS)]
    o = jnp.concatenate(parts, axis=-1)
    x = x_ref[...] + gt_ref[0] * _dot(o, w_ref[0])
    xo_ref[...] = x
    h_ref[...] = (_rms(x) * (g_ref[0] * (1.0 + sc_ref[0])) + sh_ref[0]).astype(BF16)


def _out_proj(x2d, outs, w_out, layer, gt, g_mlp, sc, sh, *, rows_per_mod, tm=512):
    t = x2d.shape[0]
    tiles_per_mod = rows_per_mod // tm
    mod_spec = pl.BlockSpec((1, 1, D_MODEL), lambda i: (i // tiles_per_mod, 0, 0))
    o_spec = pl.BlockSpec((N_HEADS, tm, LANES), lambda i: (0, i, 0))
    row_spec = pl.BlockSpec((tm, D_MODEL), lambda i: (i, 0))
    return pl.pallas_call(
        _out_kernel,
        grid=(t // tm,),
        in_specs=[row_spec, o_spec, o_spec, o_spec, o_spec,
                  pl.BlockSpec((1, D_MODEL, D_MODEL), lambda i: (layer, 0, 0),
                               pipeline_mode=pl.Buffered(1)),
                  mod_spec, pl.BlockSpec((1, 1, D_MODEL), lambda i: (layer, 0, 0)),
                  mod_spec, mod_spec],
        out_specs=[row_spec, row_spec],
        out_shape=[jax.ShapeDtypeStruct((t, D_MODEL), F32),
                   jax.ShapeDtypeStruct((t, D_MODEL), BF16)],
        compiler_params=pltpu.CompilerParams(
            dimension_semantics=("arbitrary",), vmem_limit_bytes=VMEM_LIMIT),
        name="out_proj",
    )(x2d, *outs, w_out, gt, g_mlp, sc, sh)


def _mlp_kernel(h_ref, wu_ref, wd_ref, x_ref, gt_ref, o_ref, xs_ref):
    j = pl.program_id(1)
    last = pl.num_programs(1) - 1
    slab = x_ref.shape[0]
    xs_ref[pl.ds(pl.multiple_of(j * slab, slab), slab), :] = x_ref[...]

    def step(is_first, is_last):
        u = jnp.maximum(_dot(h_ref[...], wu_ref[0]), 0.0)
        y = _dot((u * u).astype(BF16), wd_ref[0])
        acc = y if is_first else o_ref[...] + y
        o_ref[...] = xs_ref[...] + gt_ref[0] * acc if is_last else acc

    pl.when(j == 0)(lambda: step(True, False))
    pl.when(jnp.logical_and(j > 0, j < last))(lambda: step(False, False))
    pl.when(j == last)(lambda: step(False, True))


def _mlp(h2d, x2d, w_up, w_down, layer, gt, *, rows_per_mod, tm=1024, tf=1024):
    t = x2d.shape[0]
    tiles_per_mod = rows_per_mod // tm
    nj = D_FF // tf
    slab = tm // nj
    return pl.pallas_call(
        _mlp_kernel,
        grid=(t // tm, nj),
        in_specs=[
            pl.BlockSpec((tm, D_MODEL), lambda i, j: (i, 0)),
            pl.BlockSpec((1, D_MODEL, tf), lambda i, j: (layer, 0, j)),
            pl.BlockSpec((1, tf, D_MODEL), lambda i, j: (layer, j, 0)),
            pl.BlockSpec((slab, D_MODEL), lambda i, j: (i * nj + j, 0)),
            pl.BlockSpec((1, 1, D_MODEL), lambda i, j: (i // tiles_per_mod, 0, 0)),
        ],
        out_specs=pl.BlockSpec((tm, D_MODEL), lambda i, j: (i, 0)),
        out_shape=jax.ShapeDtypeStruct((t, D_MODEL), F32),
        scratch_shapes=[pltpu.VMEM((tm, D_MODEL), F32)],
        compiler_params=pltpu.CompilerParams(
            dimension_semantics=("arbitrary", "arbitrary"), vmem_limit_bytes=VMEM_LIMIT),
        name="mlp",
    )(h2d, w_up, w_down, x2d, gt)


def _rope_tables(tm):
    rows = SEQ // GRID_W
    pos = jnp.arange(max(rows, GRID_W)).astype(F32)

    def tables(rot_dim):
        half = rot_dim // 2
        inv_freq = ROPE_THETA ** (-jnp.arange(0, half, 2, dtype=F32) / half)
        ang = pos[:, None] * inv_freq
        cos, sin = jnp.cos(ang), jnp.sin(ang)
        by_row = lambda a: jnp.repeat(a[:rows], GRID_W, axis=0)
        by_col = lambda a: jnp.tile(a[:GRID_W], (rows, 1))
        cos = jnp.concatenate([by_row(cos), by_row(cos), by_col(cos), by_col(cos)], axis=-1)
        sin = jnp.concatenate([-by_row(sin), by_row(sin), -by_col(sin), by_col(sin)], axis=-1)
        return cos, sin

    cos128, sin128 = tables(HEAD_DIM)
    cos64, sin64 = tables(DIFF_HALF)
    lat = (cos128, sin128, jnp.tile(cos64, (1, 2)), jnp.tile(sin64, (1, 2)))
    ones = jnp.ones((tm, LANES), F32)
    zeros = jnp.zeros((tm, LANES), F32)
    return lat, (ones, zeros, ones, zeros)


def _prepare_params(w_in, mla_w_uq, mla_w_ukv, g_norm_mix, mla_g_qa, mla_g_kva, mla_g_q, mla_g_k,
                    gqa_g_q, gqa_g_k, diff_g_q, diff_g_k, diff_g_out, na_g_q, na_g_k, na_rpb):
    w_in_t = jnp.swapaxes(w_in, 1, 2).astype(BF16)
    uq = mla_w_uq.reshape(DEPTH, MLA_Q_RANK, N_HEADS, MLA_NOPE + MLA_ROPE)
    w_uq = jnp.concatenate(
        [uq[..., :MLA_NOPE].reshape(DEPTH, MLA_Q_RANK, -1),
         uq[..., MLA_NOPE:].reshape(DEPTH, MLA_Q_RANK, -1)], axis=2).astype(BF16)
    ukv = mla_w_ukv.reshape(DEPTH, MLA_KV_RANK, N_HEADS, MLA_NOPE + HEAD_DIM)
    w_ukv = jnp.concatenate(
        [ukv[..., :MLA_NOPE].reshape(DEPTH, MLA_KV_RANK, -1),
         ukv[..., MLA_NOPE:].reshape(DEPTH, MLA_KV_RANK, -1)], axis=2).astype(BF16)

    scale_a = (MLA_NOPE + MLA_ROPE) ** -0.5 * LOG2E
    scale_b = HEAD_DIM ** -0.5 * LOG2E
    scale_c = DIFF_HALF ** -0.5 * LOG2E
    scale_d = HEAD_DIM ** -0.5 * LOG2E
    two = lambda g: jnp.tile(g, (1, 2))
    rows = [None] * 11
    rows[G_A_QN] = mla_g_q[:, :MLA_NOPE] * scale_a
    rows[G_A_QR] = two(mla_g_q[:, MLA_NOPE:]) * scale_a
    rows[G_A_KN] = mla_g_k[:, :MLA_NOPE]
    rows[G_A_KR] = two(mla_g_k[:, MLA_NOPE:])
    rows[G_B_Q] = gqa_g_q * scale_b
    rows[G_B_K] = gqa_g_k
    rows[G_C_Q] = two(diff_g_q) * scale_c
    rows[G_C_K] = two(diff_g_k)
    rows[G_D_Q] = na_g_q * scale_d
    rows[G_D_K] = na_g_k
    rows[G_C_OUT] = diff_g_out
    pad = [jnp.zeros((DEPTH, LANES), F32)] * (GAIN_ROWS - len(rows))
    gains = jnp.stack(rows + pad, axis=1).astype(F32)

    amax = lambda r: jnp.max(jnp.abs(rows[r]), axis=1)
    slack = 1.02
    b_a = slack * (jnp.sqrt(MLA_NOPE * amax(G_A_QN) ** 2 + MLA_ROPE * amax(G_A_QR) ** 2)
                   * jnp.sqrt(MLA_NOPE * amax(G_A_KN) ** 2 + MLA_ROPE * amax(G_A_KR) ** 2))
    b_b = slack * HEAD_DIM * amax(G_B_Q) * amax(G_B_K)
    b_c = slack * DIFF_HALF * amax(G_C_Q) * amax(G_C_K)
    b_d = slack * HEAD_DIM * amax(G_D_Q) * amax(G_D_K)
    bias_hi = jnp.maximum(jnp.max(na_rpb, axis=(1, 2, 3)), 0.0) * LOG2E
    bias_lo = jnp.minimum(jnp.min(na_rpb, axis=(1, 2, 3)), 0.0) * LOG2E
    span = lambda b: jnp.stack([b, 2.0 * b], axis=1).astype(F32)
    return {
        "bound_a": span(b_a), "bound_b": span(b_b), "bound_c": span(b_c),
        "bound_d": jnp.stack([b_d + bias_hi, 2.0 * b_d + bias_hi - bias_lo], axis=1).astype(F32),
        "w_in_t": w_in_t, "w_uq": w_uq, "w_ukv": w_ukv,
        "g_mix": g_norm_mix.reshape(DEPTH, 1, D_MODEL),
        "g_qa": mla_g_qa.reshape(DEPTH, 1, -1), "g_kva": mla_g_kva.reshape(DEPTH, 1, -1),
        "gains": gains,
    }


def kernel(x, c, ctx, c_ctx, w_mod, b_mod, g_norm_mix, g_norm_mlp, w_in, mla_g_qa, mla_g_kva,
           mla_w_uq, mla_w_ukv, mla_g_q, mla_g_k, gqa_g_q, gqa_g_k, diff_g_q, diff_g_k,
           diff_lq1, diff_lk1, diff_lq2, diff_lk2, diff_g_out, na_g_q, na_g_k, na_rpb,
           w_out, w_up, w_down):
    tm_front = 512
    cc = jnp.concatenate([c, c_ctx[None, :], jnp.zeros((8 - BATCH - 1, D_MODEL), F32)], axis=0)
    mod = _modulation(cc, w_mod, b_mod)
    lat_tabs, ctx_tabs = _rope_tables(tm_front)
    pw = _prepare_params(w_in, mla_w_uq, mla_w_ukv, g_norm_mix, mla_g_qa, mla_g_kva, mla_g_q,
                         mla_g_k, gqa_g_q, gqa_g_k, diff_g_q, diff_g_k, diff_g_out, na_g_q,
                         na_g_k, na_rpb)
    g_mlp = g_norm_mlp.reshape(DEPTH, 1, D_MODEL)

    xs = x.reshape(BATCH * SEQ, D_MODEL)
    cs = ctx.reshape(BATCH * CTX_LEN, D_MODEL)
    for l in range(DEPTH):
        need_ctx = l < DEPTH - 1
        lam_init = 0.8 - 0.6 * math.exp(-0.3 * l)
        m6 = mod[l].reshape(8, 6, 1, D_MODEL)
        sh_a, sc_a, gt_a, sh_m, sc_m, gt_m = [m6[:BATCH, k] for k in range(6)]
        csh_a, csc_a, cgt_a, csh_m, csc_m, cgt_m = [m6[BATCH:BATCH + 1, k] for k in range(6)]
        lam_rows = jnp.stack([diff_lq1[l], diff_lk1[l], diff_lq2[l], diff_lk2[l]]).astype(F32)
        g_out = pw["gains"][l, G_C_OUT:G_C_OUT + 1]

        fl = _front(xs, sc_a, sh_a, pw, l, lat_tabs, rows_per_mod=SEQ, tm=tm_front)
        fc = _front(cs, csc_a, csh_a, pw, l, ctx_tabs, rows_per_mod=BATCH * CTX_LEN, tm=tm_front)
        qa, ka, va, qb, kb, vb, qc, kc, vc, qd, kd, vd = fl
        cqa, cka, cva, cqb, ckb, cvb, cqc, ckc, cvc, cqd, ckd, cvd = fc

        o_a = _attention(pw["bound_a"][l], qa, ka, va, cka, cva, kind="mla", tq=2048)
        o_b = _attention(pw["bound_b"][l], qb, kb, vb, ckb, cvb, kind="gqa", tq=1024)
        o_c = _attention(pw["bound_c"][l], qc, kc, vc, ckc, cvc, kind="diff", tq=1024,
                         extra=(lam_rows, g_out), lam_init=lam_init,
                         cast=(w_out, w_up, w_down) if l == 0 else ())
        if l == 0:
            o_c, w_out_b, w_up_b, w_down_b = o_c
        o_d = _na_attention(pw["bound_d"][l], qd, kd, vd, ckd, cvd, _na_bias(na_rpb, l))

        x_mid, h_mlp = _out_proj(xs, (o_a, o_b, o_c, o_d), w_out_b, l, gt_a, g_mlp, sc_m, sh_m,
                                 rows_per_mod=SEQ)
        xs = _mlp(h_mlp, x_mid, w_up_b, w_down_b, l, gt_m, rows_per_mod=SEQ)

        if need_ctx:
            oc = _ctx_attention(fc, lam_rows, g_out, lam_init)
            c_mid, ch_mlp = _out_proj(cs, oc, w_out_b, l, cgt_a, g_mlp, csc_m, csh_m,
                                      rows_per_mod=BATCH * CTX_LEN)
            cs = _mlp(ch_mlp, c_mid, w_up_b, w_down_b, l, cgt_m, rows_per_mod=BATCH * CTX_LEN)
    return xs.reshape(BATCH, SEQ, D_MODEL)
```

```python
import functools
import math

import numpy as np
import jax
import jax.numpy as jnp
from jax import lax
from jax.experimental import pallas as pl
from jax.experimental.pallas import tpu as pltpu

D_MODEL = 2048
BATCH = 4
SEQ = 4096
DEPTH = 2
GRID_W = 64
CTX_LEN = 256
HEAD_DIM = 128
GROUP_W = D_MODEL // 4
N_HEADS = GROUP_W // HEAD_DIM
D_FF = 4 * D_MODEL
ROPE_THETA = 10000.0
NORM_EPS = 1e-6

MLA_Q_RANK = GROUP_W
MLA_KV_RANK = GROUP_W // 2
MLA_NOPE = 128
MLA_ROPE = 64
GQA_KV_HEADS = N_HEADS // 2
DIFF_HALF = HEAD_DIM // 2
NA_WIN_H = 8
NA_WIN_W = 16
NA_QROWS = 8
NA_QBLOCK = NA_QROWS * GRID_W
NA_BLOCKS = SEQ // NA_QBLOCK
NA_BAND_ROWS = NA_WIN_H + NA_QROWS
NA_BAND = NA_BAND_ROWS * GRID_W
NA_PATTERNS = 3

A_COLS = MLA_Q_RANK + MLA_KV_RANK + MLA_ROPE
B_COLS = (N_HEADS + 2 * GQA_KV_HEADS) * HEAD_DIM
C_COLS = 3 * N_HEADS * HEAD_DIM
D_COLS = 3 * N_HEADS * HEAD_DIM

LANES = 128
OFF_AQ = 0
OFF_AKV = OFF_AQ + MLA_Q_RANK
OFF_KPE = OFF_AKV + MLA_KV_RANK
OFF_B = A_COLS
OFF_C = OFF_B + B_COLS
OFF_D = OFF_C + C_COLS

VMEM_LIMIT = 60 * 1024 * 1024
NEG_BIG = -1e30
LOG2E = math.log2(math.e)
MAX_EXP2_SPAN = 96.0
SUBLANES = 8

F32 = jnp.float32
BF16 = jnp.bfloat16

G_A_QN, G_A_QR, G_A_KN, G_A_KR, G_B_Q, G_B_K, G_C_Q, G_C_K, G_D_Q, G_D_K, G_C_OUT = range(11)
GAIN_ROWS = 16


def _lane_iota(shape):
    return lax.broadcasted_iota(jnp.int32, shape, len(shape) - 1)


def _rms(t):
    return t * lax.rsqrt(jnp.mean(t * t, axis=-1, keepdims=True) + NORM_EPS)


def _rms64(t):
    lo = _lane_iota(t.shape) < 64
    sq = t * t
    s_lo = jnp.sum(jnp.where(lo, sq, 0.0), axis=-1, keepdims=True)
    s_hi = jnp.sum(jnp.where(lo, 0.0, sq), axis=-1, keepdims=True)
    ms = jnp.where(lo, s_lo, s_hi) * (1.0 / 64.0)
    return t * lax.rsqrt(ms + NORM_EPS)


def _rope(t, cos, sin_signed, half):
    first = (_lane_iota(t.shape) % (2 * half)) < half
    partner = jnp.where(first, pltpu.roll(t, LANES - half, 1), pltpu.roll(t, half, 1))
    return t * cos + partner * sin_signed


def _dot(a, b):
    return jnp.dot(a, b, preferred_element_type=F32)


def _dot_nt(a, b):
    return lax.dot_general(a, b, (((1,), (1,)), ((), ())), preferred_element_type=F32)


def _mod_kernel(c_ref, w_ref, b_ref, o_ref):
    c = c_ref[...]
    a = (c * jax.nn.sigmoid(c)).astype(BF16)
    o_ref[0] = _dot(a, w_ref[0].astype(BF16)) + b_ref[0]


def _modulation(cc, w_mod, b_mod):
    tn = 1536
    n = 6 * D_MODEL
    return pl.pallas_call(
        _mod_kernel,
        grid=(DEPTH, n // tn),
        in_specs=[
            pl.BlockSpec((8, D_MODEL), lambda l, j: (0, 0)),
            pl.BlockSpec((1, D_MODEL, tn), lambda l, j: (l, 0, j)),
            pl.BlockSpec((1, 1, tn), lambda l, j: (l, 0, j)),
        ],
        out_specs=pl.BlockSpec((1, 8, tn), lambda l, j: (l, 0, j)),
        out_shape=jax.ShapeDtypeStruct((DEPTH, 8, n), F32),
        compiler_params=pltpu.CompilerParams(
            dimension_semantics=("arbitrary", "arbitrary"), vmem_limit_bytes=VMEM_LIMIT),
        name="modulation",
    )(cc, w_mod, b_mod.reshape(DEPTH, 1, n))


def _front_kernel(x_ref, gmix_ref, sc_ref, sh_ref, wt_ref, w_uq_ref, w_ukv_ref,
                  gqa_ref, gkva_ref, gains_ref, cos128_ref, sin128_ref, cos64_ref, sin64_ref,
                  qa_ref, ka_ref, va_ref, qb_ref, kb_ref, vb_ref,
                  qc_ref, kc_ref, vc_ref, qd_ref, kd_ref, vd_ref):
    x = x_ref[...]
    gm = gmix_ref[0] * (1.0 + sc_ref[0])
    h = (_rms(x) * gm + sh_ref[0]).astype(BF16)

    gains = gains_ref[0]

    def gain(row):
        return gains[row:row + 1, :]

    cos128 = cos128_ref[...]
    sin128 = sin128_ref[...]
    cos64 = cos64_ref[...]
    sin64 = sin64_ref[...]
    lo = _lane_iota((x.shape[0], LANES)) < 64

    def proj(off, width):
        return _dot_nt(h, wt_ref[0, off:off + width, :])

    def tile(t, j):
        return t[:, j * LANES:(j + 1) * LANES]

    def store_t(ref, hd, t):
        ref[hd] = t.T.astype(BF16)

    cq = (_rms(proj(OFF_AQ, MLA_Q_RANK)) * gqa_ref[0]).astype(BF16)
    ckv = (_rms(proj(OFF_AKV, MLA_KV_RANK)) * gkva_ref[0]).astype(BF16)
    w_kpe = wt_ref[0, OFF_KPE:A_COLS, :]
    p_kpe = _dot_nt(h, jnp.concatenate([w_kpe, w_kpe], axis=0))
    kr = _rope(_rms64(p_kpe) * gain(G_A_KR), cos64, sin64, 16).astype(BF16)
    q = _dot(cq, w_uq_ref[0])
    kv = _dot(ckv, w_ukv_ref[0])
    qr_pairs = [
        _rope(_rms64(tile(q, N_HEADS + j)) * gain(G_A_QR), cos64, sin64, 16)
        for j in range(N_HEADS // 2)
    ]
    for hd in range(N_HEADS):
        qa_ref[hd, :, 0:LANES] = (_rms(tile(q, hd)) * gain(G_A_QN)).astype(BF16)
        pair = qr_pairs[hd // 2]
        keep = lo if hd % 2 == 0 else jnp.logical_not(lo)
        qa_ref[hd, :, LANES:2 * LANES] = jnp.where(keep, pair, 0.0).astype(BF16)
        ka_ref[hd, :, 0:LANES] = (_rms(tile(kv, hd)) * gain(G_A_KN)).astype(BF16)
        ka_ref[hd, :, LANES:2 * LANES] = kr
        store_t(va_ref, hd, tile(kv, N_HEADS + hd))

    pb = proj(OFF_B, B_COLS)
    for hd in range(N_HEADS):
        qb_ref[hd] = _rope(_rms(tile(pb, hd)) * gain(G_B_Q), cos128, sin128, 32).astype(BF16)
    for hd in range(GQA_KV_HEADS):
        kb_ref[hd] = _rope(_rms(tile(pb, N_HEADS + hd)) * gain(G_B_K),
                           cos128, sin128, 32).astype(BF16)
        store_t(vb_ref, hd, tile(pb, N_HEADS + GQA_KV_HEADS + hd))

    pc = proj(OFF_C, C_COLS)
    for hd in range(N_HEADS):
        qh = _rope(_rms64(tile(pc, hd)) * gain(G_C_Q), cos64, sin64, 16)
        qc_ref[2 * hd] = jnp.where(lo, qh, 0.0).astype(BF16)
        qc_ref[2 * hd + 1] = jnp.where(lo, 0.0, qh).astype(BF16)
        kc_ref[hd] = _rope(_rms64(tile(pc, N_HEADS + hd)) * gain(G_C_K),
                           cos64, sin64, 16).astype(BF16)
        store_t(vc_ref, hd, tile(pc, 2 * N_HEADS + hd))

    pd = proj(OFF_D, D_COLS)
    for hd in range(N_HEADS):
        qd_ref[hd] = (_rms(tile(pd, hd)) * gain(G_D_Q)).astype(BF16)
        kd_ref[hd] = (_rms(tile(pd, N_HEADS + hd)) * gain(G_D_K)).astype(BF16)
        store_t(vd_ref, hd, tile(pd, 2 * N_HEADS + hd))


_FRONT_OUT = (
    (N_HEADS, 2 * LANES, False), (N_HEADS, 2 * LANES, False), (N_HEADS, LANES, True),
    (N_HEADS, LANES, False), (GQA_KV_HEADS, LANES, False), (GQA_KV_HEADS, LANES, True),
    (2 * N_HEADS, LANES, False), (N_HEADS, LANES, False), (N_HEADS, LANES, True),
    (N_HEADS, LANES, False), (N_HEADS, LANES, False), (N_HEADS, LANES, True),
)


def _front(x2d, sc, sh, pw, layer, tabs, *, rows_per_mod, tm=256):
    t = x2d.shape[0]
    n_pos_tiles = tabs[0].shape[0] // tm
    tiles_per_mod = rows_per_mod // tm

    def resident(a):
        return pl.BlockSpec((1,) + a.shape[1:], lambda i: (layer,) + (0,) * (a.ndim - 1),
                            pipeline_mode=pl.Buffered(1))

    mod_spec = pl.BlockSpec((1, 1, D_MODEL), lambda i: (i // tiles_per_mod, 0, 0))
    tab_spec = pl.BlockSpec((tm, LANES), lambda i: (i % n_pos_tiles, 0))
    params = [pw["g_mix"], None, None, pw["w_in_t"], pw["w_uq"], pw["w_ukv"],
              pw["g_qa"], pw["g_kva"], pw["gains"]]
    in_specs = [pl.BlockSpec((tm, D_MODEL), lambda i: (i, 0))]
    in_specs += [mod_spec if p is None else resident(p) for p in params]
    in_specs += [tab_spec] * 4
    params[1], params[2] = sc, sh
    out_specs = [pl.BlockSpec((nh, w, tm), lambda i: (0, 0, i)) if tr
                 else pl.BlockSpec((nh, tm, w), lambda i: (0, i, 0)) for nh, w, tr in _FRONT_OUT]
    out_shape = [jax.ShapeDtypeStruct((nh, w, t) if tr else (nh, t, w), BF16)
                 for nh, w, tr in _FRONT_OUT]
    return pl.pallas_call(
        _front_kernel,
        grid=(t // tm,),
        in_specs=in_specs,
        out_specs=out_specs,
        out_shape=out_shape,
        compiler_params=pltpu.CompilerParams(
            dimension_semantics=("arbitrary",), vmem_limit_bytes=VMEM_LIMIT),
        name="front",
    )(x2d, *params, *tabs)


def _attend_t(q, chunks, shift=None):
    cols = q.shape[0]
    dv = chunks[0][1].shape[0]

    def scores(k, bias):
        st = _dot_nt(k, q)
        return st if bias is None else st + bias

    if shift is None:
        m = jnp.full((1, cols), NEG_BIG, F32)
        l = jnp.zeros((1, cols), F32)
        acc = jnp.zeros((dv, cols), F32)
        for k, vt, bias in chunks:
            st = scores(k, bias)
            m_new = jnp.maximum(m, jnp.max(st, axis=0, keepdims=True))
            alpha = jnp.exp2(m - m_new)
            p = jnp.exp2(st - m_new)
            l = alpha * l + jnp.sum(p, axis=0, keepdims=True)
            acc = alpha * acc + _dot(vt, p.astype(BF16))
            m = m_new
        return acc / l
    acc = jnp.zeros((dv, cols), F32)
    l8 = jnp.zeros((SUBLANES, cols), F32)
    for k, vt, bias in chunks:
        p = jnp.exp2(scores(k, bias) - shift)
        l8 = l8 + jnp.sum(p.reshape(p.shape[0] // SUBLANES, SUBLANES, cols), axis=0)
        acc = acc + _dot(vt, p.astype(BF16))
    return acc / jnp.sum(l8, axis=0, keepdims=True)


def _by_underflow_guard(bound, call):
    return lax.cond(bound[1] <= MAX_EXP2_SPAN, lambda: call(True), lambda: call(False))


def _diff_lambda(lam_ref, lam_init):
    lam1 = jnp.exp(jnp.sum(lam_ref[0:1, :] * lam_ref[1:2, :], axis=-1, keepdims=True))
    lam2 = jnp.exp(jnp.sum(lam_ref[2:3, :] * lam_ref[3:4, :], axis=-1, keepdims=True))
    return lam1 - lam2 + lam_init


def _diff_combine(o, tq, lam, g_out, lam_init):
    d = o[:tq] - lam * o[tq:]
    return _rms(d) * (g_out * (1.0 - lam_init))


def _attn_kernel(bound_ref, q_ref, k_ref, v_ref, kc_ref, vc_ref, *rest, stacked, diff,
                 lam_init, tk, n_cast, fixed_shift):
    if diff:
        lam_ref, gout_ref = rest[:2]
        rest = rest[2:]
    cast_in, o_ref, cast_out = rest[:n_cast], rest[n_cast], rest[n_cast + 1:]
    tq = q_ref.shape[1]
    for src, dst in zip(cast_in, cast_out):
        dst[...] = src[...].astype(dst.dtype)

    def run(shift):
        if stacked:
            q = jnp.concatenate([q_ref[0], q_ref[1]], axis=0)
        else:
            q = q_ref[0]
        chunks = [(k_ref[0, c * tk:(c + 1) * tk, :], v_ref[0, :, c * tk:(c + 1) * tk], None)
                  for c in range(SEQ // tk)]
        chunks.append((kc_ref[0], vc_ref[0], None))
        o = _attend_t(q, chunks, shift).T
        if diff:
            lam = _diff_lambda(lam_ref, lam_init)
            o_ref[0] = _diff_combine(o, tq, lam, gout_ref[...], lam_init).astype(o_ref.dtype)
        elif stacked:
            o_ref[0] = o[:tq].astype(o_ref.dtype)
            o_ref[1] = o[tq:].astype(o_ref.dtype)
        else:
            o_ref[0] = o.astype(o_ref.dtype)

    run(bound_ref[0] if fixed_shift else None)


def _attention(bound, q, k, v, kc, vc, *, kind, extra=(), cast=(), lam_init=0.0, tq=512,
               tk=1024):
    nq = SEQ // tq
    dk = q.shape[-1]
    stacked = kind in ("gqa", "diff")
    n_outer = GQA_KV_HEADS if kind == "gqa" else N_HEADS
    q_heads = 2 if stacked else 1
    o_heads = 2 if kind == "gqa" else 1
    n_steps = BATCH * n_outer * nq
    cast2d = [w.reshape(-1, w.shape[-1]) for w in cast]
    cast_specs = [pl.BlockSpec((w.shape[0] // n_steps, w.shape[1]),
                               lambda b, h, i: ((b * n_outer + h) * nq + i, 0)) for w in cast2d]
    in_specs = [
        pl.BlockSpec(memory_space=pltpu.SMEM),
        pl.BlockSpec((q_heads, tq, dk), lambda b, h, i: (h, b * nq + i, 0)),
        pl.BlockSpec((1, SEQ, dk), lambda b, h, i: (h, b, 0)),
        pl.BlockSpec((1, LANES, SEQ), lambda b, h, i: (h, 0, b)),
        pl.BlockSpec((1, CTX_LEN, dk), lambda b, h, i: (h, b, 0)),
        pl.BlockSpec((1, LANES, CTX_LEN), lambda b, h, i: (h, 0, b)),
    ]
    for e in extra:
        in_specs.append(pl.BlockSpec(e.shape, lambda b, h, i, nd=e.ndim: (0,) * nd))
    call = lambda fixed_shift: pl.pallas_call(
        functools.partial(_attn_kernel, stacked=stacked, diff=(kind == "diff"),
                          lam_init=lam_init, tk=tk, n_cast=len(cast), fixed_shift=fixed_shift),
        grid=(BATCH, n_outer, nq),
        in_specs=in_specs + cast_specs,
        out_specs=[pl.BlockSpec((o_heads, tq, LANES), lambda b, h, i: (h, b * nq + i, 0))]
        + cast_specs,
        out_shape=[jax.ShapeDtypeStruct((N_HEADS, BATCH * SEQ, LANES), BF16)]
        + [jax.ShapeDtypeStruct(w.shape, BF16) for w in cast2d],
        compiler_params=pltpu.CompilerParams(
            dimension_semantics=("arbitrary", "arbitrary", "arbitrary"),
            vmem_limit_bytes=VMEM_LIMIT),
        name="attn_" + kind,
    )(bound, q, k, v, kc, vc, *extra, *cast2d)
    outs = _by_underflow_guard(bound, call)
    if not cast:
        return outs[0]
    return [outs[0]] + [o.reshape(w.shape) for o, w in zip(outs[1:], cast)]


def _na_geometry(i):
    rows = SEQ // GRID_W
    r0 = i * NA_QROWS
    band_start = min(max(r0 - NA_WIN_H // 2, 0), rows - NA_BAND_ROWS)
    rel = np.zeros((NA_QROWS, NA_BAND_ROWS), np.int64)
    valid = np.zeros((NA_QROWS, NA_BAND_ROWS), bool)
    for a in range(NA_QROWS):
        q_row = r0 + a
        row_start = min(max(q_row - NA_WIN_H // 2, 0), rows - NA_WIN_H)
        assert band_start <= row_start and row_start + NA_WIN_H <= band_start + NA_BAND_ROWS
        for j in range(NA_BAND_ROWS):
            k_row = band_start + j
            valid[a, j] = row_start <= k_row < row_start + NA_WIN_H
            rel[a, j] = min(max(k_row - q_row + NA_WIN_H - 1, 0), 2 * NA_WIN_H - 2)
    return band_start, rel, valid


def _na_pattern_of(i):
    return 0 if i == 0 else (NA_PATTERNS - 1 if i == NA_BLOCKS - 1 else 1)


_NA_REPRESENTATIVE = (0, 1, NA_BLOCKS - 1)
for _i in range(NA_BLOCKS):
    _g, _r = _na_geometry(_i), _na_geometry(_NA_REPRESENTATIVE[_na_pattern_of(_i)])
    assert (_g[1] == _r[1]).all() and (_g[2] == _r[2]).all()


def _na_bias_kernel(rpb_ref, o_ref):
    hd = pl.program_id(0)
    n_r = 2 * NA_WIN_H - 1
    n_c = 2 * NA_WIN_W - 1
    k_col = lax.broadcasted_iota(jnp.int32, (GRID_W, LANES), 0)
    lane = lax.broadcasted_iota(jnp.int32, (GRID_W, LANES), 1)
    q_col = lane % GRID_W
    left = lane < GRID_W
    col_idx = jnp.clip(k_col - q_col + NA_WIN_W - 1, 0, n_c - 1)
    col_start = jnp.clip(q_col - NA_WIN_W // 2, 0, GRID_W - NA_WIN_W)
    col_ok = (k_col >= col_start) & (k_col < col_start + NA_WIN_W)
    tables = []
    for r in range(n_r):
        t = jnp.zeros((GRID_W, LANES), F32)
        for c in range(n_c):
            t = jnp.where(col_idx == c, rpb_ref[hd * (n_r * n_c) + r * n_c + c] * LOG2E, t)
        tables.append(jnp.where(col_ok, t, NEG_BIG))
    masked = jnp.full((GRID_W, LANES), NEG_BIG, F32)
    for p, i in enumerate(_NA_REPRESENTATIVE):
        _, rel, valid = _na_geometry(i)
        for j in range(NA_BAND_ROWS):
            for a in range(0, NA_QROWS, 2):
                t0 = tables[int(rel[a, j])] if valid[a, j] else masked
                t1 = tables[int(rel[a + 1, j])] if valid[a + 1, j] else masked
                blk = t0 if t0 is t1 else jnp.where(left, t0, t1)
                o_ref[0, p, j * GRID_W:(j + 1) * GRID_W, a * GRID_W:(a + 2) * GRID_W] = blk


def _na_bias(rpb, layer):
    n = N_HEADS * (2 * NA_WIN_H - 1) * (2 * NA_WIN_W - 1)
    return pl.pallas_call(
        _na_bias_kernel,
        grid=(N_HEADS,),
        in_specs=[pl.BlockSpec(memory_space=pltpu.SMEM)],
        out_specs=pl.BlockSpec((1, NA_PATTERNS, NA_BAND, NA_QBLOCK), lambda h: (h, 0, 0, 0)),
        out_shape=jax.ShapeDtypeStruct((N_HEADS, NA_PATTERNS, NA_BAND, NA_QBLOCK), F32),
        compiler_params=pltpu.CompilerParams(
            dimension_semantics=("arbitrary",), vmem_limit_bytes=VMEM_LIMIT),
        name="na_bias",
    )(rpb.reshape(DEPTH, n)[layer])


def _na_kernel(bound_ref, q_ref, k_ref, vt_ref, kc_ref, vct_ref, bias_ref, o_ref, *,
               fixed_shift):
    def run(shift):
        kc = kc_ref[0]
        vct = vct_ref[0]
        for i in range(NA_BLOCKS):
            k0 = _na_geometry(i)[0] * GRID_W
            rows = slice(i * NA_QBLOCK, (i + 1) * NA_QBLOCK)
            chunks = [(k_ref[0, k0:k0 + NA_BAND, :], vt_ref[0, :, k0:k0 + NA_BAND],
                       bias_ref[0, _na_pattern_of(i)]),
                      (kc, vct, None)]
            o_ref[0, rows, :] = _attend_t(q_ref[0, rows, :], chunks, shift).T.astype(o_ref.dtype)

    run(bound_ref[0] if fixed_shift else None)


def _na_attention(bound, q, k, vt, kc, vct, bias):
    call = lambda fixed_shift: pl.pallas_call(
        functools.partial(_na_kernel, fixed_shift=fixed_shift),
        grid=(N_HEADS, BATCH),
        in_specs=[
            pl.BlockSpec(memory_space=pltpu.SMEM),
            pl.BlockSpec((1, SEQ, LANES), lambda h, b: (h, b, 0)),
            pl.BlockSpec((1, SEQ, LANES), lambda h, b: (h, b, 0)),
            pl.BlockSpec((1, LANES, SEQ), lambda h, b: (h, 0, b)),
            pl.BlockSpec((1, CTX_LEN, LANES), lambda h, b: (h, b, 0)),
            pl.BlockSpec((1, LANES, CTX_LEN), lambda h, b: (h, 0, b)),
            pl.BlockSpec((1, NA_PATTERNS, NA_BAND, NA_QBLOCK), lambda h, b: (h, 0, 0, 0)),
        ],
        out_specs=pl.BlockSpec((1, SEQ, LANES), lambda h, b: (h, b, 0)),
        out_shape=jax.ShapeDtypeStruct((N_HEADS, BATCH * SEQ, LANES), BF16),
        compiler_params=pltpu.CompilerParams(
            dimension_semantics=("arbitrary", "arbitrary"), vmem_limit_bytes=VMEM_LIMIT),
        name="attn_na",
    )(bound, q, k, vt, kc, vct, bias)
    return _by_underflow_guard(bound, call)


def _ctx_attn_kernel(qa_ref, ka_ref, va_ref, qb_ref, kb_ref, vb_ref, qc_ref, kc_ref, vc_ref,
                     qd_ref, kd_ref, vd_ref, lam_ref, gout_ref,
                     oa_ref, ob_ref, oc_ref, od_ref, *, lam_init):
    n = CTX_LEN
    lam = _diff_lambda(lam_ref, lam_init)

    def attend(q, k_ref, vt_ref, hd):
        return _attend_t(q, [(k_ref[hd], vt_ref[hd], None)]).T

    for hd in range(N_HEADS):
        oa_ref[hd] = attend(qa_ref[hd], ka_ref, va_ref, hd).astype(BF16)
        kvh = hd // (N_HEADS // GQA_KV_HEADS)
        ob_ref[hd] = attend(qb_ref[hd], kb_ref, vb_ref, kvh).astype(BF16)
        q2 = jnp.concatenate([qc_ref[2 * hd], qc_ref[2 * hd + 1]], axis=0)
        o2 = attend(q2, kc_ref, vc_ref, hd)
        oc_ref[hd] = _diff_combine(o2, n, lam, gout_ref[...], lam_init).astype(BF16)
        od_ref[hd] = attend(qd_ref[hd], kd_ref, vd_ref, hd).astype(BF16)


def _ctx_attention(fc, lam_rows, g_out, lam_init):
    in_specs = [pl.BlockSpec((nh, w, CTX_LEN), lambda b: (0, 0, b)) if tr
                else pl.BlockSpec((nh, CTX_LEN, w), lambda b: (0, b, 0))
                for nh, w, tr in _FRONT_OUT]
    in_specs += [pl.BlockSpec(lam_rows.shape, lambda b: (0, 0)),
                 pl.BlockSpec(g_out.shape, lambda b: (0, 0))]
    o_spec = pl.BlockSpec((N_HEADS, CTX_LEN, LANES), lambda b: (0, b, 0))
    o_shape = jax.ShapeDtypeStruct((N_HEADS, BATCH * CTX_LEN, LANES), BF16)
    return pl.pallas_call(
        functools.partial(_ctx_attn_kernel, lam_init=lam_init),
        grid=(BATCH,),
        in_specs=in_specs,
        out_specs=[o_spec] * 4,
        out_shape=[o_shape] * 4,
        compiler_params=pltpu.CompilerParams(
            dimension_semantics=("arbitrary",), vmem_limit_bytes=VMEM_LIMIT),
        name="attn_ctx",
    )(*fc, lam_rows, g_out)


def _out_kernel(x_ref, oa_ref, ob_ref, oc_ref, od_ref, w_ref, gt_ref, g_ref, sc_ref, sh_ref,
                xo_ref, h_ref):
    parts = [r[hd] for r in (oa_ref, ob_ref, oc_ref, od_ref) for hd in range(N_HEADS)]
    o = jnp.concatenate(parts, axis=-1)
    x = x_ref[...] + gt_ref[0] * _dot(o, w_ref[0])
    xo_ref[...] = x
    h_ref[...] = (_rms(x) * (g_ref[0] * (1.0 + sc_ref[0])) + sh_ref[0]).astype(BF16)


def _out_proj(x2d, outs, w_out, layer, gt, g_mlp, sc, sh, *, rows_per_mod, tm=512):
    t = x2d.shape[0]
    tiles_per_mod = rows_per_mod // tm
    mod_spec = pl.BlockSpec((1, 1, D_MODEL), lambda i: (i // tiles_per_mod, 0, 0))
    o_spec = pl.BlockSpec((N_HEADS, tm, LANES), lambda i: (0, i, 0))
    row_spec = pl.BlockSpec((tm, D_MODEL), lambda i: (i, 0))
    return pl.pallas_call(
        _out_kernel,
        grid=(t // tm,),
        in_specs=[row_spec, o_spec, o_spec, o_spec, o_spec,
                  pl.BlockSpec((1, D_MODEL, D_MODEL), lambda i: (layer, 0, 0),
                               pipeline_mode=pl.Buffered(1)),
                  mod_spec, pl.BlockSpec((1, 1, D_MODEL), lambda i: (layer, 0, 0)),
                  mod_spec, mod_spec],
        out_specs=[row_spec, row_spec],
        out_shape=[jax.ShapeDtypeStruct((t, D_MODEL), F32),
                   jax.ShapeDtypeStruct((t, D_MODEL), BF16)],
        compiler_params=pltpu.CompilerParams(
            dimension_semantics=("arbitrary",), vmem_limit_bytes=VMEM_LIMIT),
        name="out_proj",
    )(x2d, *outs, w_out, gt, g_mlp, sc, sh)


def _mlp_kernel(h_ref, wu_ref, wd_ref, x_ref, gt_ref, o_ref, xs_ref):
    j = pl.program_id(1)
    last = pl.num_programs(1) - 1
    slab = x_ref.shape[0]
    xs_ref[pl.ds(pl.multiple_of(j * slab, slab), slab), :] = x_ref[...]

    def step(is_first, is_last):
        u = jnp.maximum(_dot(h_ref[...], wu_ref[0]), 0.0)
        y = _dot((u * u).astype(BF16), wd_ref[0])
        acc = y if is_first else o_ref[...] + y
        o_ref[...] = xs_ref[...] + gt_ref[0] * acc if is_last else acc

    pl.when(j == 0)(lambda: step(True, False))
    pl.when(jnp.logical_and(j > 0, j < last))(lambda: step(False, False))
    pl.when(j == last)(lambda: step(False, True))


def _mlp(h2d, x2d, w_up, w_down, layer, gt, *, rows_per_mod, tm=1024, tf=1024):
    t = x2d.shape[0]
    tiles_per_mod = rows_per_mod // tm
    nj = D_FF // tf
    slab = tm // nj
    return pl.pallas_call(
        _mlp_kernel,
        grid=(t // tm, nj),
        in_specs=[
            pl.BlockSpec((tm, D_MODEL), lambda i, j: (i, 0)),
            pl.BlockSpec((1, D_MODEL, tf), lambda i, j: (layer, 0, j)),
            pl.BlockSpec((1, tf, D_MODEL), lambda i, j: (layer, j, 0)),
            pl.BlockSpec((slab, D_MODEL), lambda i, j: (i * nj + j, 0)),
            pl.BlockSpec((1, 1, D_MODEL), lambda i, j: (i // tiles_per_mod, 0, 0)),
        ],
        out_specs=pl.BlockSpec((tm, D_MODEL), lambda i, j: (i, 0)),
        out_shape=jax.ShapeDtypeStruct((t, D_MODEL), F32),
        scratch_shapes=[pltpu.VMEM((tm, D_MODEL), F32)],
        compiler_params=pltpu.CompilerParams(
            dimension_semantics=("arbitrary", "arbitrary"), vmem_limit_bytes=VMEM_LIMIT),
        name="mlp",
    )(h2d, w_up, w_down, x2d, gt)


def _rope_tables(tm):
    rows = SEQ // GRID_W
    pos = jnp.arange(max(rows, GRID_W)).astype(F32)

    def tables(rot_dim):
        half = rot_dim // 2
        inv_freq = ROPE_THETA ** (-jnp.arange(0, half, 2, dtype=F32) / half)
        ang = pos[:, None] * inv_freq
        cos, sin = jnp.cos(ang), jnp.sin(ang)
        by_row = lambda a: jnp.repeat(a[:rows], GRID_W, axis=0)
        by_col = lambda a: jnp.tile(a[:GRID_W], (rows, 1))
        cos = jnp.concatenate([by_row(cos), by_row(cos), by_col(cos), by_col(cos)], axis=-1)
        sin = jnp.concatenate([-by_row(sin), by_row(sin), -by_col(sin), by_col(sin)], axis=-1)
        return cos, sin

    cos128, sin128 = tables(HEAD_DIM)
    cos64, sin64 = tables(DIFF_HALF)
    lat = (cos128, sin128, jnp.tile(cos64, (1, 2)), jnp.tile(sin64, (1, 2)))
    ones = jnp.ones((tm, LANES), F32)
    zeros = jnp.zeros((tm, LANES), F32)
    return lat, (ones, zeros, ones, zeros)


def _prepare_params(w_in, mla_w_uq, mla_w_ukv, g_norm_mix, mla_g_qa, mla_g_kva, mla_g_q, mla_g_k,
                    gqa_g_q, gqa_g_k, diff_g_q, diff_g_k, diff_g_out, na_g_q, na_g_k, na_rpb):
    w_in_t = jnp.swapaxes(w_in, 1, 2).astype(BF16)
    uq = mla_w_uq.reshape(DEPTH, MLA_Q_RANK, N_HEADS, MLA_NOPE + MLA_ROPE)
    w_uq = jnp.concatenate(
        [uq[..., :MLA_NOPE].reshape(DEPTH, MLA_Q_RANK, -1),
         uq[..., MLA_NOPE:].reshape(DEPTH, MLA_Q_RANK, -1)], axis=2).astype(BF16)
    ukv = mla_w_ukv.reshape(DEPTH, MLA_KV_RANK, N_HEADS, MLA_NOPE + HEAD_DIM)
    w_ukv = jnp.concatenate(
        [ukv[..., :MLA_NOPE].reshape(DEPTH, MLA_KV_RANK, -1),
         ukv[..., MLA_NOPE:].reshape(DEPTH, MLA_KV_RANK, -1)], axis=2).astype(BF16)

    scale_a = (MLA_NOPE + MLA_ROPE) ** -0.5 * LOG2E
    scale_b = HEAD_DIM ** -0.5 * LOG2E
    scale_c = DIFF_HALF ** -0.5 * LOG2E
    scale_d = HEAD_DIM ** -0.5 * LOG2E
    two = lambda g: jnp.tile(g, (1, 2))
    rows = [None] * 11
    rows[G_A_QN] = mla_g_q[:, :MLA_NOPE] * scale_a
    rows[G_A_QR] = two(mla_g_q[:, MLA_NOPE:]) * scale_a
    rows[G_A_KN] = mla_g_k[:, :MLA_NOPE]
    rows[G_A_KR] = two(mla_g_k[:, MLA_NOPE:])
    rows[G_B_Q] = gqa_g_q * scale_b
    rows[G_B_K] = gqa_g_k
    rows[G_C_Q] = two(diff_g_q) * scale_c
    rows[G_C_K] = two(diff_g_k)
    rows[G_D_Q] = na_g_q * scale_d
    rows[G_D_K] = na_g_k
    rows[G_C_OUT] = diff_g_out
    pad = [jnp.zeros((DEPTH, LANES), F32)] * (GAIN_ROWS - len(rows))
    gains = jnp.stack(rows + pad, axis=1).astype(F32)

    amax = lambda r: jnp.max(jnp.abs(rows[r]), axis=1)
    slack = 1.02
    b_a = slack * (jnp.sqrt(MLA_NOPE * amax(G_A_QN) ** 2 + MLA_ROPE * amax(G_A_QR) ** 2)
                   * jnp.sqrt(MLA_NOPE * amax(G_A_KN) ** 2 + MLA_ROPE * amax(G_A_KR) ** 2))
    b_b = slack * HEAD_DIM * amax(G_B_Q) * amax(G_B_K)
    b_c = slack * DIFF_HALF * amax(G_C_Q) * amax(G_C_K)
    b_d = slack * HEAD_DIM * amax(G_D_Q) * amax(G_D_K)
    bias_hi = jnp.maximum(jnp.max(na_rpb, axis=(1, 2, 3)), 0.0) * LOG2E
    bias_lo = jnp.minimum(jnp.min(na_rpb, axis=(1, 2, 3)), 0.0) * LOG2E
    span = lambda b: jnp.stack([b, 2.0 * b], axis=1).astype(F32)
    return {
        "bound_a": span(b_a), "bound_b": span(b_b), "bound_c": span(b_c),
        "bound_d": jnp.stack([b_d + bias_hi, 2.0 * b_d + bias_hi - bias_lo], axis=1).astype(F32),
        "w_in_t": w_in_t, "w_uq": w_uq, "w_ukv": w_ukv,
        "g_mix": g_norm_mix.reshape(DEPTH, 1, D_MODEL),
        "g_qa": mla_g_qa.reshape(DEPTH, 1, -1), "g_kva": mla_g_kva.reshape(DEPTH, 1, -1),
        "gains": gains,
    }


def kernel(x, c, ctx, c_ctx, w_mod, b_mod, g_norm_mix, g_norm_mlp, w_in, mla_g_qa, mla_g_kva,
           mla_w_uq, mla_w_ukv, mla_g_q, mla_g_k, gqa_g_q, gqa_g_k, diff_g_q, diff_g_k,
           diff_lq1, diff_lk1, diff_lq2, diff_lk2, diff_g_out, na_g_q, na_g_k, na_rpb,
           w_out, w_up, w_down):
    tm_front = 512
    cc = jnp.concatenate([c, c_ctx[None, :], jnp.zeros((8 - BATCH - 1, D_MODEL), F32)], axis=0)
    mod = _modulation(cc, w_mod, b_mod)
    lat_tabs, ctx_tabs = _rope_tables(tm_front)
    pw = _prepare_params(w_in, mla_w_uq, mla_w_ukv, g_norm_mix, mla_g_qa, mla_g_kva, mla_g_q,
                         mla_g_k, gqa_g_q, gqa_g_k, diff_g_q, diff_g_k, diff_g_out, na_g_q,
                         na_g_k, na_rpb)
    g_mlp = g_norm_mlp.reshape(DEPTH, 1, D_MODEL)

    xs = x.reshape(BATCH * SEQ, D_MODEL)
    cs = ctx.reshape(BATCH * CTX_LEN, D_MODEL)
    for l in range(DEPTH):
        need_ctx = l < DEPTH - 1
        lam_init = 0.8 - 0.6 * math.exp(-0.3 * l)
        m6 = mod[l].reshape(8, 6, 1, D_MODEL)
        sh_a, sc_a, gt_a, sh_m, sc_m, gt_m = [m6[:BATCH, k] for k in range(6)]
        csh_a, csc_a, cgt_a, csh_m, csc_m, cgt_m = [m6[BATCH:BATCH + 1, k] for k in range(6)]
        lam_rows = jnp.stack([diff_lq1[l], diff_lk1[l], diff_lq2[l], diff_lk2[l]]).astype(F32)
        g_out = pw["gains"][l, G_C_OUT:G_C_OUT + 1]

        fl = _front(xs, sc_a, sh_a, pw, l, lat_tabs, rows_per_mod=SEQ, tm=tm_front)
        fc = _front(cs, csc_a, csh_a, pw, l, ctx_tabs, rows_per_mod=BATCH * CTX_LEN, tm=tm_front)
        qa, ka, va, qb, kb, vb, qc, kc, vc, qd, kd, vd = fl
        cqa, cka, cva, cqb, ckb, cvb, cqc, ckc, cvc, cqd, ckd, cvd = fc

        o_a = _attention(pw["bound_a"][l], qa, ka, va, cka, cva, kind="mla", tq=2048)
        o_b = _attention(pw["bound_b"][l], qb, kb, vb, ckb, cvb, kind="gqa", tq=1024)
        o_c = _attention(pw["bound_c"][l], qc, kc, vc, ckc, cvc, kind="diff", tq=1024,
                         extra=(lam_rows, g_out), lam_init=lam_init,
                         cast=(w_out, w_up, w_down) if l == 0 else ())
        if l == 0:
            o_c, w_out_b, w_up_b, w_down_b = o_c
        o_d = _na_attention(pw["bound_d"][l], qd, kd, vd, ckd, cvd, _na_bias(na_rpb, l))

        x_mid, h_mlp = _out_proj(xs, (o_a, o_b, o_c, o_d), w_out_b, l, gt_a, g_mlp, sc_m, sh_m,
                                 rows_per_mod=SEQ)
        xs = _mlp(h_mlp, x_mid, w_up_b, w_down_b, l, gt_m, rows_per_mod=SEQ)

        if need_ctx:
            oc = _ctx_attention(fc, lam_rows, g_out, lam_init)
            c_mid, ch_mlp = _out_proj(cs, oc, w_out_b, l, cgt_a, g_mlp, csc_m, csh_m,
                                      rows_per_mod=BATCH * CTX_LEN)
            cs = _mlp(ch_mlp, c_mid, w_up_b, w_down_b, l, cgt_m, rows_per_mod=BATCH * CTX_LEN)
    return xs.reshape(BATCH, SEQ, D_MODEL)
```
